```python
import jax, jax.numpy as jnp
from jax import lax
import numpy as np

D_MODEL = 1024
BATCH = 8
SEQ = 4096
DEPTH = 2

PLE_DIM = 256
N_EVEN = (DEPTH + 1) // 2
N_ODD = DEPTH // 2
RG_WIDTH = 512
RG_BLOCKS = 8
RG_BLOCK = RG_WIDTH // RG_BLOCKS
RG_CONV = 4
RG_C = 8.0
ML_HEADS = 4
ML_HEAD_DIM = 128
ML_WIDTH = ML_HEADS * ML_HEAD_DIM
ML_CONV = 4
ML_CHUNK = 64
HY_IN = 2 * RG_WIDTH + 4 * ML_WIDTH + 2 * ML_HEADS
HY_MIX = RG_WIDTH + ML_WIDTH
AT_HEADS = 16
AT_KV_HEADS = 4
AT_HEAD_DIM = 64
WINDOW = 128
ROPE_THETA = 10000.0
AT_QKV = (AT_HEADS + 2 * AT_KV_HEADS) * AT_HEAD_DIM
FF_DIM = 3 * D_MODEL
FF_CONV = 3
EPS = 1e-6

kernel_name = 'hybrid_rglru_mlstm_swa_trunk'


def rms_norm(x, g):
    xf = x.astype(jnp.float32)
    y = xf * lax.rsqrt(jnp.mean(xf * xf, axis=-1, keepdims=True) + EPS)
    return (y * g.astype(jnp.float32)).astype(x.dtype)


def causal_dwconv(x, w, b):
    k_w = w.shape[0]
    s = x.shape[1]
    xp = jnp.pad(x, ((0, 0), (k_w - 1, 0), (0, 0)))
    y = xp[:, 0:s] * w[0]
    for j in range(1, k_w):
        y = y + xp[:, j:j + s] * w[j]
    return y + b


def _linear_recurrence_combine(left, right):
    a_l, b_l = left
    a_r, b_r = right
    return a_l * a_r, a_r * b_l + b_r


def rg_lru(x, w_a, b_a, w_x, b_x, lam):
    f32 = jnp.float32
    bsz, s, _ = x.shape
    xf = x.astype(f32)
    xb = xf.reshape(bsz, s, RG_BLOCKS, RG_BLOCK)
    r = jax.nn.sigmoid(jnp.einsum('bsni,nij->bsnj', xb, w_a.astype(f32)).reshape(bsz, s, RG_WIDTH) + b_a.astype(f32))
    i = jax.nn.sigmoid(jnp.einsum('bsni,nij->bsnj', xb, w_x.astype(f32)).reshape(bsz, s, RG_WIDTH) + b_x.astype(f32))
    log_a = -RG_C * r * jax.nn.softplus(-lam.astype(f32))
    a = jnp.exp(log_a)
    u = jnp.sqrt(-jnp.expm1(2.0 * log_a)) * (i * xf)
    _, h = lax.associative_scan(_linear_recurrence_combine, (a, u), axis=1)
    return h


def mlstm_chunkwise(q, k, v, i_pre, f_pre):
    f32 = jnp.float32
    bsz, s, nh, dh = q.shape
    nc = s // ML_CHUNK

    def to_chunks(t):
        return t.astype(f32).reshape(bsz, nc, ML_CHUNK, nh, dh).transpose(0, 3, 1, 2, 4)

    def gate_chunks(t):
        return t.astype(f32).reshape(bsz, nc, ML_CHUNK, nh).transpose(0, 3, 1, 2)

    qc = to_chunks(q)
    kc = to_chunks(k) * (dh ** -0.5)
    vc = to_chunks(v)
    log_i = gate_chunks(i_pre)
    log_f = jax.nn.log_sigmoid(gate_chunks(f_pre))
    b = jnp.cumsum(log_f, axis=-1)
    b_last = b[..., -1]

    g = b_last[..., None] - b + log_i
    m_loc = jnp.max(g, axis=-1)
    w_loc = jnp.exp(g - m_loc[..., None])
    c_loc = jnp.einsum('bhcl,bhcld,bhcle->bhcde', w_loc, kc, vc)
    n_loc = jnp.einsum('bhcl,bhcld->bhcd', w_loc, kc)

    def step(carry, inp):
        c_st, n_st, m_st = carry
        c_l, n_l, m_l, b_l = inp
        m_new = jnp.maximum(b_l + m_st, m_l)
        a_prev = jnp.exp(b_l + m_st - m_new)
        a_loc = jnp.exp(m_l - m_new)
        c_new = a_prev[..., None, None] * c_st + a_loc[..., None, None] * c_l
        n_new = a_prev[..., None] * n_st + a_loc[..., None] * n_l
        return (c_new, n_new, m_new), (c_st, n_st, m_st)

    init = (jnp.zeros((bsz, nh, dh, dh), f32), jnp.zeros((bsz, nh, dh), f32), jnp.full((bsz, nh), -jnp.inf, f32))
    xs = (jnp.moveaxis(c_loc, 2, 0), jnp.moveaxis(n_loc, 2, 0), jnp.moveaxis(m_loc, 2, 0), jnp.moveaxis(b_last, 2, 0))
    _, (c_prev, n_prev, m_prev) = lax.scan(step, init, xs)
    c_prev = jnp.moveaxis(c_prev, 0, 2)
    n_prev = jnp.moveaxis(n_prev, 0, 2)
    m_prev = jnp.moveaxis(m_prev, 0, 2)

    idx = jnp.arange(ML_CHUNK)
    causal = idx[:, None] >= idx[None, :]
    d_log = b[..., :, None] - b[..., None, :] + log_i[..., None, :]
    d_log = jnp.where(causal, d_log, -jnp.inf)
    inter_log = b + m_prev[..., None]
    m_t = jnp.maximum(inter_log, jnp.max(d_log, axis=-1))
    w_intra = jnp.exp(d_log - m_t[..., None])
    a_inter = jnp.exp(inter_log - m_t)
    s_qk = jnp.einsum('bhctd,bhcsd->bhcts', qc, kc) * w_intra
    num = jnp.einsum('bhcts,bhcse->bhcte', s_qk, vc) + a_inter[..., None] * jnp.einsum('bhctd,bhcde->bhcte', qc, c_prev)
    den = jnp.sum(s_qk, axis=-1) + a_inter * jnp.einsum('bhctd,bhcd->bhct', qc, n_prev)
    h = num / jnp.maximum(jnp.abs(den), jnp.exp(-m_t))[..., None]
    return h.transpose(0, 2, 3, 1, 4).reshape(bsz, s, nh, dh)


def rglru_mlstm_mixer(h, w_in, b_in, rg_conv_w, rg_conv_b, rg_w_a, rg_b_a, rg_w_x, rg_b_x, rg_lambda,
                      ml_conv_w, ml_conv_b, ml_norm, w_out):
    f32 = jnp.float32
    bsz, s, _ = h.shape
    z = h @ w_in + b_in
    cuts = [RG_WIDTH, 2 * RG_WIDTH, 2 * RG_WIDTH + ML_WIDTH, 2 * RG_WIDTH + 2 * ML_WIDTH,
            2 * RG_WIDTH + 3 * ML_WIDTH, 2 * RG_WIDTH + 4 * ML_WIDTH, 2 * RG_WIDTH + 4 * ML_WIDTH + ML_HEADS]
    x_rg, g_rg, q, k, v, o, i_pre, f_pre = jnp.split(z, cuts, axis=-1)
    xc = causal_dwconv(x_rg, rg_conv_w, rg_conv_b)
    y_rg = rg_lru(xc, rg_w_a, rg_b_a, rg_w_x, rg_b_x, rg_lambda) * jax.nn.gelu(g_rg.astype(f32), approximate=True)
    qk = jax.nn.silu(causal_dwconv(jnp.concatenate([q, k], axis=-1), ml_conv_w, ml_conv_b))
    q, k = jnp.split(qk, 2, axis=-1)
    hm = mlstm_chunkwise(q.reshape(bsz, s, ML_HEADS, ML_HEAD_DIM), k.reshape(bsz, s, ML_HEADS, ML_HEAD_DIM),
                         v.reshape(bsz, s, ML_HEADS, ML_HEAD_DIM), i_pre, f_pre)
    hm = rms_norm(hm, ml_norm.reshape(ML_HEADS, ML_HEAD_DIM))
    y_ml = jax.nn.sigmoid(o.astype(f32)) * hm.reshape(bsz, s, ML_WIDTH)
    y = jnp.concatenate([y_rg, y_ml], axis=-1).astype(h.dtype)
    return y @ w_out


def rope_tables(positions):
    half = AT_HEAD_DIM // 2
    inv_freq = ROPE_THETA ** (-jnp.arange(half, dtype=jnp.float32) * (2.0 / AT_HEAD_DIM))
    ang = positions.astype(jnp.float32)[..., None] * inv_freq
    return jnp.cos(ang)[:, :, None, :], jnp.sin(ang)[:, :, None, :]


def apply_rope(x, cos, sin):
    xf = x.astype(jnp.float32)
    x1, x2 = jnp.split(xf, 2, axis=-1)
    return jnp.concatenate([x1 * cos - x2 * sin, x2 * cos + x1 * sin], axis=-1)


def banded_sink_attention(q, k, v, sinks):
    f32 = jnp.float32
    bsz, s, _, dh = q.shape
    nb = s // WINDOW
    grp = AT_HEADS // AT_KV_HEADS
    qb = q.astype(f32).reshape(bsz, nb, WINDOW, AT_KV_HEADS, grp, dh)

    def key_window(t):
        tb = t.astype(f32).reshape(bsz, nb, WINDOW, AT_KV_HEADS, dh)
        prev = jnp.pad(tb, ((0, 0), (1, 0), (0, 0), (0, 0), (0, 0)))[:, :-1]
        return jnp.concatenate([prev, tb], axis=2)

    kw = key_window(k)
    vw = key_window(v)
    logits = jnp.einsum('bnqhgd,bnkhd->bnhgqk', qb, kw) * (dh ** -0.5)
    qi = jnp.arange(WINDOW)[:, None]
    kj = jnp.arange(2 * WINDOW)[None, :]
    band = (kj > qi) & (kj <= qi + WINDOW)
    blk = jnp.arange(nb)[:, None, None]
    valid = band[None] & ((blk > 0) | (kj >= WINDOW)[None])
    logits = jnp.where(valid[None, :, None, None], logits, -jnp.inf)
    sink = sinks.astype(f32).reshape(AT_KV_HEADS, grp)[:, :, None]
    m = jnp.maximum(jnp.max(logits, axis=-1), sink)
    probs = jnp.exp(logits - m[..., None])
    denom = jnp.sum(probs, axis=-1) + jnp.exp(sink - m)
    out = jnp.einsum('bnhgqk,bnkhd->bnhgqd', probs, vw) / denom[..., None]
    return out.transpose(0, 1, 4, 2, 3, 5).reshape(bsz, s, AT_HEADS * dh)


def swa_sink_mixer(h, positions, w_qkv, q_norm, k_norm, sinks, w_out):
    bsz, s, _ = h.shape
    qkv = h @ w_qkv
    q, k, v = jnp.split(qkv, [AT_HEADS * AT_HEAD_DIM, (AT_HEADS + AT_KV_HEADS) * AT_HEAD_DIM], axis=-1)
    q = q.reshape(bsz, s, AT_HEADS, AT_HEAD_DIM)
    k = k.reshape(bsz, s, AT_KV_HEADS, AT_HEAD_DIM)
    v = v.reshape(bsz, s, AT_KV_HEADS, AT_HEAD_DIM)
    cos, sin = rope_tables(positions)
    q = apply_rope(rms_norm(q, q_norm), cos, sin)
    k = apply_rope(rms_norm(k, k_norm), cos, sin)
    o = banded_sink_attention(q, k, v, sinks)
    return o.astype(h.dtype) @ w_out


def conv_ffn(h, w_up, conv_w, conv_b, w_down):
    g, u = jnp.split(h @ w_up, 2, axis=-1)
    g = causal_dwconv(g, conv_w, conv_b)
    return (jax.nn.gelu(g, approximate=True) * u) @ w_down


def _normal(key, shape, scale):
    return jax.random.normal(key, shape, jnp.float32) * scale


def setup_inputs(seed: int = 0) -> dict:
    key = jax.random.key(seed)
    ks = jax.random.split(key, 30)
    x = _normal(ks[0], (BATCH, SEQ, D_MODEL), 1.0)
    p = _normal(ks[1], (DEPTH, BATCH, SEQ, PLE_DIM), 1.0)
    offs = jax.random.randint(ks[2], (BATCH, 1), 0, 1024, dtype=jnp.int32)
    positions = offs + jnp.arange(SEQ, dtype=jnp.int32)[None, :]
    norm_mix = 1.0 + _normal(ks[3], (DEPTH, D_MODEL), 0.05)
    norm_ffn = 1.0 + _normal(ks[4], (DEPTH, D_MODEL), 0.05)
    norm_ple = 1.0 + _normal(ks[5], (DEPTH, D_MODEL), 0.05)
    hy_w_in = _normal(ks[6], (N_EVEN, D_MODEL, HY_IN), D_MODEL ** -0.5)
    hy_b_in = _normal(ks[7], (N_EVEN, HY_IN), 0.02)
    hy_b_in = hy_b_in.at[:, HY_IN - ML_HEADS:].add(jnp.linspace(3.0, 6.0, ML_HEADS))
    rg_conv_w = _normal(ks[8], (N_EVEN, RG_CONV, RG_WIDTH), RG_CONV ** -0.5)
    rg_conv_b = _normal(ks[9], (N_EVEN, RG_WIDTH), 0.02)
    rg_w_a = _normal(ks[10], (N_EVEN, RG_BLOCKS, RG_BLOCK, RG_BLOCK), RG_BLOCK ** -0.5)
    rg_b_a = _normal(ks[11], (N_EVEN, RG_WIDTH), 0.02)
    rg_w_x = _normal(ks[12], (N_EVEN, RG_BLOCKS, RG_BLOCK, RG_BLOCK), RG_BLOCK ** -0.5)
    rg_b_x = _normal(ks[13], (N_EVEN, RG_WIDTH), 0.02)
    a_c = jax.random.uniform(ks[14], (N_EVEN, RG_WIDTH), jnp.float32, 0.9, 0.999)
    a_base = a_c ** (1.0 / RG_C)
    rg_lambda = jnp.log(a_base) - jnp.log1p(-a_base)
    ml_conv_w = _normal(ks[15], (N_EVEN, ML_CONV, 2 * ML_WIDTH), ML_CONV ** -0.5)
    ml_conv_b = _normal(ks[16], (N_EVEN, 2 * ML_WIDTH), 0.02)
    ml_norm = 1.0 + _normal(ks[17], (N_EVEN, ML_WIDTH), 0.05)
    hy_w_out = _normal(ks[18], (N_EVEN, HY_MIX, D_MODEL), HY_MIX ** -0.5)
    at_w_qkv = _normal(ks[19], (N_ODD, D_MODEL, AT_QKV), D_MODEL ** -0.5)
    at_q_norm = 1.0 + _normal(ks[20], (N_ODD, AT_HEAD_DIM), 0.05)
    at_k_norm = 1.0 + _normal(ks[21], (N_ODD, AT_HEAD_DIM), 0.05)
    at_sinks = _normal(ks[22], (N_ODD, AT_HEADS), 0.5)
    at_w_out = _normal(ks[23], (N_ODD, AT_HEADS * AT_HEAD_DIM, D_MODEL), (AT_HEADS * AT_HEAD_DIM) ** -0.5)
    ff_w_up = _normal(ks[24], (DEPTH, D_MODEL, 2 * FF_DIM), D_MODEL ** -0.5)
    ff_conv_w = _normal(ks[25], (DEPTH, FF_CONV, FF_DIM), FF_CONV ** -0.5)
    ff_conv_b = _normal(ks[26], (DEPTH, FF_DIM), 0.02)
    ff_w_down = _normal(ks[27], (DEPTH, FF_DIM, D_MODEL), FF_DIM ** -0.5)
    ple_w_gate = _normal(ks[28], (DEPTH, D_MODEL, D_MODEL), D_MODEL ** -0.5)
    ple_w_proj = _normal(ks[29], (DEPTH, PLE_DIM, D_MODEL), PLE_DIM ** -0.5)
    return {'x': x, 'p': p, 'positions': positions,
            'norm_mix': norm_mix, 'norm_ffn': norm_ffn, 'norm_ple': norm_ple,
            'hy_w_in': hy_w_in, 'hy_b_in': hy_b_in,
            'rg_conv_w': rg_conv_w, 'rg_conv_b': rg_conv_b, 'rg_w_a': rg_w_a, 'rg_b_a': rg_b_a,
            'rg_w_x': rg_w_x, 'rg_b_x': rg_b_x, 'rg_lambda': rg_lambda,
            'ml_conv_w': ml_conv_w, 'ml_conv_b': ml_conv_b, 'ml_norm': ml_norm, 'hy_w_out': hy_w_out,
            'at_w_qkv': at_w_qkv, 'at_q_norm': at_q_norm, 'at_k_norm': at_k_norm, 'at_sinks': at_sinks,
            'at_w_out': at_w_out,
            'ff_w_up': ff_w_up, 'ff_conv_w': ff_conv_w, 'ff_conv_b': ff_conv_b, 'ff_w_down': ff_w_down,
            'ple_w_gate': ple_w_gate, 'ple_w_proj': ple_w_proj}


def reference(x, p, positions, norm_mix, norm_ffn, norm_ple, hy_w_in, hy_b_in,
              rg_conv_w, rg_conv_b, rg_w_a, rg_b_a, rg_w_x, rg_b_x, rg_lambda,
              ml_conv_w, ml_conv_b, ml_norm, hy_w_out,
              at_w_qkv, at_q_norm, at_k_norm, at_sinks, at_w_out,
              ff_w_up, ff_conv_w, ff_conv_b, ff_w_down, ple_w_gate, ple_w_proj):
    for layer in range(DEPTH):
        h = rms_norm(x, norm_mix[layer])
        if layer % 2 == 0:
            e = layer // 2
            mix = rglru_mlstm_mixer(h, hy_w_in[e], hy_b_in[e], rg_conv_w[e], rg_conv_b[e], rg_w_a[e], rg_b_a[e],
                                    rg_w_x[e], rg_b_x[e], rg_lambda[e], ml_conv_w[e], ml_conv_b[e], ml_norm[e],
                                    hy_w_out[e])
        else:
            o = layer // 2
            mix = swa_sink_mixer(h, positions, at_w_qkv[o], at_q_norm[o], at_k_norm[o], at_sinks[o], at_w_out[o])
        x = x + mix.astype(x.dtype)
        x = x + conv_ffn(rms_norm(x, norm_ffn[layer]), ff_w_up[layer], ff_conv_w[layer], ff_conv_b[layer],
                         ff_w_down[layer]).astype(x.dtype)
        gate = jax.nn.sigmoid(rms_norm(x, norm_ple[layer]) @ ple_w_gate[layer])
        x = x + (gate * (p[layer] @ ple_w_proj[layer])).astype(x.dtype)
    return x
```

```python
import functools
import math

import jax
import jax.numpy as jnp
import numpy as np
from jax import lax
from jax.experimental import pallas as pl
from jax.experimental.pallas import tpu as pltpu

F32 = jnp.float32
BF16 = jnp.bfloat16

D_MODEL = 1024
PLE_DIM = 256
RG_WIDTH = 512
RG_BLOCKS = 8
RG_C = 8.0
ML_HEADS = 4
ML_HEAD_DIM = 128
ML_WIDTH = ML_HEADS * ML_HEAD_DIM
ML_CHUNK = 128
AT_HEADS = 16
AT_KV_HEADS = 4
AT_HEAD_DIM = 64
WINDOW = 128
ROPE_THETA = 10000.0
FF_DIM = 3 * D_MODEL
EPS = 1e-6

LANES = 128
SUBLANES = 8
VMEM_LIMIT_BYTES = 56 * 1024 * 1024

TM_PROJ = 512
TS_RG = 512
TS_ML = 512
TQ_ATTN = 512
TM_POST = 512
FF_CHUNK = 512


def _params(n_axes):
    return pltpu.CompilerParams(dimension_semantics=("arbitrary",) * n_axes,
                                vmem_limit_bytes=VMEM_LIMIT_BYTES)


def _resident(shape):
    n = len(shape)
    return pl.BlockSpec(shape, lambda *_: (0,) * n, pipeline_mode=pl.Buffered(1))


def _rms(x, g):
    return x * lax.rsqrt(jnp.mean(x * x, axis=-1, keepdims=True) + EPS) * g


def _shift_rows(x, tail, d):
    rolled = pltpu.roll(x, d, 0)
    tail_rolled = pltpu.roll(tail, d, 0)
    row = lax.broadcasted_iota(jnp.int32, tail.shape, 0)
    head = jnp.where(row < d, tail_rolled, rolled[:SUBLANES])
    return jnp.concatenate([head, rolled[SUBLANES:]], axis=0)


def _causal_conv(x, tail, w_ref, b_ref):
    k_w = w_ref.shape[0]
    y = x * w_ref[k_w - 1:k_w, :] + b_ref[...]
    for j in range(k_w - 1):
        y = y + _shift_rows(x, tail, k_w - 1 - j) * w_ref[j:j + 1, :]
    return y


def _log_sigmoid(x):
    return jnp.minimum(x, 0.0) - jnp.log1p(jnp.exp(-jnp.abs(x)))


def _norm_proj_kernel(x_ref, g_ref, w_ref, b_ref, *o_refs, splits):
    h = _rms(x_ref[0], g_ref[...]).astype(BF16)
    for o_ref, (lo, hi) in zip(o_refs, splits):
        z = jnp.dot(h, w_ref[:, lo:hi], preferred_element_type=F32) + b_ref[:, lo:hi]
        o_ref[0] = z.astype(o_ref.dtype)


def _norm_proj(x, g, w, b, splits, name):
    bsz, s, d = x.shape
    n = w.shape[1]
    tm = min(TM_PROJ, s)
    outs = [jax.ShapeDtypeStruct((bsz, s, hi - lo), F32) for lo, hi in splits]
    return pl.pallas_call(
        functools.partial(_norm_proj_kernel, splits=splits),
        out_shape=outs,
        grid=(bsz, s // tm),
        in_specs=[pl.BlockSpec((1, tm, d), lambda i, j: (i, j, 0)),
                  _resident((1, d)), _resident((d, n)), _resident((1, n))],
        out_specs=[pl.BlockSpec((1, tm, hi - lo), lambda i, j: (i, j, 0)) for lo, hi in splits],
        compiler_params=_params(2),
        name=name,
    )(x, g, w, b)


def _rg_kernel(x_ref, g_ref, cw_ref, cb_ref, wa_ref, ba_ref, wx_ref, bx_ref, lam_ref, o_ref,
               tail_ref, hc_ref, a_ref, u_ref):
    ts = x_ref.shape[1]

    @pl.when(pl.program_id(1) == 0)
    def _():
        tail_ref[...] = jnp.zeros_like(tail_ref)
        hc_ref[...] = jnp.zeros_like(hc_ref)

    x = x_ref[0]
    xc = _causal_conv(x, tail_ref[...], cw_ref, cb_ref)
    tail_ref[...] = x[ts - SUBLANES:]
    xb = xc.astype(BF16)
    r = jax.nn.sigmoid(jnp.dot(xb, wa_ref[...], preferred_element_type=F32) + ba_ref[...])
    i = jax.nn.sigmoid(jnp.dot(xb, wx_ref[...], preferred_element_type=F32) + bx_ref[...])
    lam = lam_ref[...]
    softplus_neg_lam = jnp.maximum(-lam, 0.0) + jnp.log1p(jnp.exp(-jnp.abs(lam)))
    log_a = (-RG_C) * r * softplus_neg_lam
    a = jnp.exp(log_a)
    a_ref[...] = a
    u_ref[...] = jnp.sqrt(1.0 - a * a) * (i * xc)

    row = lax.broadcasted_iota(jnp.int32, (SUBLANES, RG_WIDTH), 0)

    def group(k, hc):
        r0 = pl.multiple_of(k * SUBLANES, SUBLANES)
        a = a_ref[pl.ds(r0, SUBLANES), :]
        u = u_ref[pl.ds(r0, SUBLANES), :]
        for d in (1, 2, 4):
            keep = row >= d
            u = u + a * jnp.where(keep, pltpu.roll(u, d, 0), 0.0)
            a = a * jnp.where(keep, pltpu.roll(a, d, 0), 1.0)
        h = u + a * hc
        u_ref[pl.ds(r0, SUBLANES), :] = h
        return jnp.broadcast_to(h[SUBLANES - 1:SUBLANES, :], (SUBLANES, RG_WIDTH))

    hc_ref[...] = lax.fori_loop(0, ts // SUBLANES, group, hc_ref[...], unroll=4)
    o_ref[0] = (u_ref[...] * jax.nn.gelu(g_ref[0], approximate=True)).astype(o_ref.dtype)


def _rg_lru(z, cw, cb, wa, ba, wx, bx, lam):
    bsz, s, _ = z.shape
    ts = min(TS_RG, s)
    w = RG_WIDTH
    return pl.pallas_call(
        _rg_kernel,
        out_shape=jax.ShapeDtypeStruct((bsz, s, w), BF16),
        grid=(bsz, s // ts),
        in_specs=[pl.BlockSpec((1, ts, w), lambda i, j: (i, j, 0)),
                  pl.BlockSpec((1, ts, w), lambda i, j: (i, j, 1)),
                  _resident(cw.shape), _resident((1, w)), _resident((w, w)), _resident((1, w)),
                  _resident((w, w)), _resident((1, w)), _resident((1, w))],
        out_specs=pl.BlockSpec((1, ts, w), lambda i, j: (i, j, 0)),
        scratch_shapes=[pltpu.VMEM((SUBLANES, w), F32), pltpu.VMEM((SUBLANES, w), F32),
                        pltpu.VMEM((ts, w), F32), pltpu.VMEM((ts, w), F32)],
        compiler_params=_params(2),
        name="rg_lru",
    )(z, z, cw, cb, wa, ba, wx, bx, lam)


def _lane_col(x, h):
    return jnp.broadcast_to(x[:, h:h + 1], x.shape)


def _ml_kernel(q_ref, k_ref, v_ref, o_ref, gt_ref, cw_ref, cb_ref, nw_ref, y_ref,
               qtail_ref, ktail_ref, c_ref, n_ref, m_ref):
    ts = q_ref.shape[1]
    lc = ML_CHUNK
    dh = ML_HEAD_DIM

    @pl.when(pl.program_id(1) == 0)
    def _():
        qtail_ref[...] = jnp.zeros_like(qtail_ref)
        ktail_ref[...] = jnp.zeros_like(ktail_ref)
        c_ref[...] = jnp.zeros_like(c_ref)
        n_ref[...] = jnp.zeros_like(n_ref)
        m_ref[...] = jnp.full_like(m_ref, -jnp.inf)

    row = lax.broadcasted_iota(jnp.int32, (lc, LANES), 0)
    col = lax.broadcasted_iota(jnp.int32, (lc, LANES), 1)
    causal = row >= col

    def chunk(c, carry):
        r0 = pl.multiple_of(c * lc, lc)
        rows = pl.ds(r0, lc)
        q_raw = q_ref[0, rows, :]
        k_raw = k_ref[0, rows, :]
        q = jax.nn.silu(_causal_conv(q_raw, qtail_ref[...], cw_ref.at[:, :ML_WIDTH], cb_ref.at[:, :ML_WIDTH]))
        k = jax.nn.silu(_causal_conv(k_raw, ktail_ref[...], cw_ref.at[:, ML_WIDTH:], cb_ref.at[:, ML_WIDTH:]))
        qtail_ref[...] = q_raw[lc - SUBLANES:]
        ktail_ref[...] = k_raw[lc - SUBLANES:]
        k = k * (dh ** -0.5)

        gates = gt_ref[0, rows, :]
        b = _log_sigmoid(gates)
        d = 1
        while d < lc:
            b = b + jnp.where(row >= d, pltpu.roll(b, d, 0), 0.0)
            d *= 2
        b = pltpu.roll(b, LANES - ML_HEADS, 1)
        r = gates - b
        b_last = b[lc - 1:lc, :]
        g_loc = b_last + r
        m_loc = jnp.max(g_loc, axis=0, keepdims=True)
        w_loc = jnp.exp(g_loc - m_loc)
        m_st = m_ref[0:1, :]
        m_new = jnp.maximum(b_last + m_st, m_loc)
        a_prev = jnp.exp(b_last + m_st - m_new)
        a_loc = jnp.exp(m_loc - m_new)
        rmax = r
        d = 1
        while d < lc:
            rmax = jnp.maximum(rmax, jnp.where(row >= d, pltpu.roll(rmax, d, 0), -jnp.inf))
            d *= 2
        e = -jnp.maximum(m_st, rmax)
        a_inter = jnp.exp(m_st + e)
        exp_neg_mt = jnp.exp(e - b)
        r_t = r.T

        for h in range(ML_HEADS):
            hs = slice(h * dh, (h + 1) * dh)
            qh = q[:, hs]
            kh = k[:, hs]
            vh = v_ref[0, rows, hs]
            qb = qh.astype(BF16)
            kb = kh.astype(BF16)
            vb = vh.astype(BF16)
            c_prev = c_ref[h]
            n_prev = n_ref[h:h + 1, :]
            w_intra = jnp.where(causal, jnp.exp(_lane_col(e, h) + r_t[h:h + 1, :]), 0.0)
            s_qk = lax.dot_general(qb, kb, (((1,), (1,)), ((), ())), preferred_element_type=F32) * w_intra
            a_in = _lane_col(a_inter, h)
            num = (jnp.dot(s_qk.astype(BF16), vb, preferred_element_type=F32)
                   + a_in * jnp.dot(qb, c_prev.astype(BF16), preferred_element_type=F32))
            den = jnp.sum(s_qk + a_in * (qh * n_prev), axis=-1, keepdims=True)
            hm = num / jnp.maximum(jnp.abs(den), exp_neg_mt[:, h:h + 1])
            hm = hm * lax.rsqrt(jnp.mean(hm * hm, axis=-1, keepdims=True) + EPS) * nw_ref[:, hs]
            y = jax.nn.sigmoid(o_ref[0, rows, hs]) * hm
            y_ref[0, rows, hs] = y.astype(y_ref.dtype)

            kw = kh * _lane_col(w_loc, h)
            c_loc = jnp.dot(kw.T.astype(BF16), vb, preferred_element_type=F32)
            n_loc = jnp.sum(kw, axis=0, keepdims=True)
            ap = a_prev[:, h:h + 1]
            al = a_loc[:, h:h + 1]
            c_ref[h] = ap * c_prev + al * c_loc
            n_ref[h:h + 1, :] = ap * n_prev + al * n_loc
        m_ref[...] = jnp.broadcast_to(m_new, m_ref.shape)
        return carry

    lax.fori_loop(0, ts // lc, chunk, 0)


def _mlstm(z, gates, cw, cb, nw):
    bsz, s, _ = z.shape
    ts = min(TS_ML, s)
    w = ML_WIDTH
    col0 = 2 * RG_WIDTH // w
    return pl.pallas_call(
        _ml_kernel,
        out_shape=jax.ShapeDtypeStruct((bsz, s, w), BF16),
        grid=(bsz, s // ts),
        in_specs=[pl.BlockSpec((1, ts, w), lambda i, j: (i, j, col0)),
                  pl.BlockSpec((1, ts, w), lambda i, j: (i, j, col0 + 1)),
                  pl.BlockSpec((1, ts, w), lambda i, j: (i, j, col0 + 2)),
                  pl.BlockSpec((1, ts, w), lambda i, j: (i, j, col0 + 3)),
                  pl.BlockSpec((1, ts, LANES), lambda i, j: (i, j, 0)),
                  _resident(cw.shape), _resident(cb.shape), _resident(nw.shape)],
        out_specs=pl.BlockSpec((1, ts, w), lambda i, j: (i, j, 0)),
        scratch_shapes=[pltpu.VMEM((SUBLANES, w), F32), pltpu.VMEM((SUBLANES, w), F32),
                        pltpu.VMEM((ML_HEADS, ML_HEAD_DIM, ML_HEAD_DIM), F32),
                        pltpu.VMEM((SUBLANES, ML_HEAD_DIM), F32),
                        pltpu.VMEM((SUBLANES, LANES), F32)],
        compiler_params=_params(2),
        name="mlstm",
    )(z, z, z, z, gates, cw, cb, nw)


def _swap_halves(x, period):
    half = period // 2
    lane = lax.broadcasted_iota(jnp.int32, x.shape, 1)
    lo = (lane % period) < half
    return jnp.where(lo, pltpu.roll(x, LANES - half, 1), pltpu.roll(x, half, 1))


def _attn_kernel(q_ref, k_ref, v_ref, pos_ref, freq_ref, qn_ref, kn_ref, seg_ref, sink_ref, o_ref,
                 qs_ref, kf_ref, vf_ref):
    tq = q_ref.shape[1]
    w = WINDOW
    n_sub = tq // w
    n_qslab = AT_HEADS * AT_HEAD_DIM // LANES
    n_kslab = AT_KV_HEADS * AT_HEAD_DIM // LANES
    first = pl.program_id(1) == 0

    @pl.when(first)
    def _():
        kf_ref[:, 0:w, :] = jnp.zeros((n_kslab, w, LANES), kf_ref.dtype)
        vf_ref[:, 0:w, :] = jnp.zeros((n_kslab, w, LANES), vf_ref.dtype)

    lane = lax.broadcasted_iota(jnp.int32, (w, LANES), 1)
    sin_sign = jnp.where((lane % AT_HEAD_DIM) < AT_HEAD_DIM // 2, -1.0, 1.0)
    seg = seg_ref[...]

    def norm_rope(x, gain, cos, sin):
        ms = jnp.dot((x * x).astype(BF16), seg, preferred_element_type=F32) * (1.0 / AT_HEAD_DIM)
        xn = x * lax.rsqrt(ms + EPS) * gain
        return xn * cos + _swap_halves(xn, AT_HEAD_DIM) * sin

    for i in range(n_sub):
        rows = slice(i * w, (i + 1) * w)
        pos = pos_ref[0, 0, i:i + 1, :].astype(F32)
        pos_col = jnp.broadcast_to(pos, (w, LANES)).T
        ang = pos_col * freq_ref[...]
        cos = jnp.cos(ang)
        sin = jnp.sin(ang) * sin_sign
        for sl in range(n_qslab):
            cs = slice(sl * LANES, (sl + 1) * LANES)
            xq = norm_rope(q_ref[0, rows, cs], qn_ref[...], cos, sin) * (AT_HEAD_DIM ** -0.5)
            qs_ref[sl, rows, :] = xq.astype(qs_ref.dtype)
        for sl in range(n_kslab):
            cs = slice(sl * LANES, (sl + 1) * LANES)
            xk = norm_rope(k_ref[0, rows, cs], kn_ref[...], cos, sin)
            kf_ref[sl, w + i * w:w + (i + 1) * w, :] = xk.astype(kf_ref.dtype)
            vf_ref[sl, w + i * w:w + (i + 1) * w, :] = v_ref[0, rows, cs].astype(vf_ref.dtype)

    qi = lax.broadcasted_iota(jnp.int32, (w, 2 * w), 0)
    kj = lax.broadcasted_iota(jnp.int32, (w, 2 * w), 1)
    band = (kj > qi) & (kj <= qi + w)
    klane = lax.broadcasted_iota(jnp.int32, (2 * w, LANES), 1)
    lo_half = klane < AT_HEAD_DIM
    grp = AT_HEADS // AT_KV_HEADS

    for i in range(n_sub):
        rows = slice(i * w, (i + 1) * w)
        keys = slice(i * w, (i + 2) * w)
        if i == 0:
            valid = band & (kj >= jnp.where(first, w, 0))
        else:
            valid = band
        for hk in range(AT_KV_HEADS):
            ksl = kf_ref[hk // 2, keys, :]
            vsl = vf_ref[hk // 2, keys, :]
            zero = jnp.zeros_like(ksl)
            if hk % 2 == 0:
                k_lo = jnp.where(lo_half, ksl, zero)
                v_lo = jnp.where(lo_half, vsl, zero)
                k_hi = pltpu.roll(k_lo.astype(F32), AT_HEAD_DIM, 1).astype(BF16)
                v_hi = pltpu.roll(v_lo.astype(F32), AT_HEAD_DIM, 1).astype(BF16)
            else:
                k_hi = jnp.where(lo_half, zero, ksl)
                v_hi = jnp.where(lo_half, zero, vsl)
                k_lo = pltpu.roll(k_hi.astype(F32), AT_HEAD_DIM, 1).astype(BF16)
                v_lo = pltpu.roll(v_hi.astype(F32), AT_HEAD_DIM, 1).astype(BF16)
            for pair in range(grp // 2):
                sl = (hk * grp) // 2 + pair
                qsl = qs_ref[sl, rows, :]
                out = jnp.zeros((w, LANES), F32)
                for half, (kk, vv) in enumerate(((k_lo, v_lo), (k_hi, v_hi))):
                    head = 2 * sl + half
                    logits = lax.dot_general(qsl, kk, (((1,), (1,)), ((), ())), preferred_element_type=F32)
                    logits = jnp.where(valid, logits, -jnp.inf)
                    sink = sink_ref[0:1, head:head + 1]
                    m = jnp.maximum(jnp.max(logits, axis=-1, keepdims=True), sink)
                    p = jnp.exp(logits - m)
                    denom = jnp.sum(p, axis=-1, keepdims=True) + jnp.exp(sink - m)
                    out = out + jnp.dot(p.astype(BF16), vv, preferred_element_type=F32) / denom
                o_ref[0, rows, sl * LANES:(sl + 1) * LANES] = out.astype(o_ref.dtype)

    kf_ref[:, 0:w, :] = kf_ref[:, tq:tq + w, :]
    vf_ref[:, 0:w, :] = vf_ref[:, tq:tq + w, :]


def _attention(qkv, positions, q_norm, k_norm, sinks):
    bsz, s, _ = qkv.shape
    tq = min(TQ_ATTN, s)
    qw = AT_HEADS * AT_HEAD_DIM
    kw = AT_KV_HEADS * AT_HEAD_DIM
    half = AT_HEAD_DIM // 2
    inv_freq = ROPE_THETA ** (-jnp.arange(half, dtype=F32) * (2.0 / AT_HEAD_DIM))
    freq = jnp.tile(inv_freq, LANES // half)[None, :]
    seg_id = np.arange(LANES) // AT_HEAD_DIM
    seg = jnp.asarray((seg_id[:, None] == seg_id[None, :]).astype(np.float32), dtype=BF16)
    pos = positions.reshape(bsz, s // tq, tq // WINDOW, WINDOW)
    qn = jnp.tile(q_norm.astype(F32), LANES // AT_HEAD_DIM)[None, :]
    kn = jnp.tile(k_norm.astype(F32), LANES // AT_HEAD_DIM)[None, :]
    sk = jnp.zeros((SUBLANES, LANES), F32).at[0, :AT_HEADS].set(sinks.astype(F32))
    return pl.pallas_call(
        _attn_kernel,
        out_shape=jax.ShapeDtypeStruct((bsz, s, qw), BF16),
        grid=(bsz, s // tq),
        in_specs=[pl.BlockSpec((1, tq, qw), lambda i, j: (i, j, 0)),
                  pl.BlockSpec((1, tq, kw), lambda i, j: (i, j, qw // kw)),
                  pl.BlockSpec((1, tq, kw), lambda i, j: (i, j, qw // kw + 1)),
                  pl.BlockSpec((1, 1, tq // WINDOW, WINDOW), lambda i, j: (i, j, 0, 0)),
                  _resident((1, LANES)), _resident((1, LANES)), _resident((1, LANES)),
                  _resident((LANES, LANES)), _resident((SUBLANES, LANES))],
        out_specs=pl.BlockSpec((1, tq, qw), lambda i, j: (i, j, 0)),
        scratch_shapes=[pltpu.VMEM((qw // LANES, tq, LANES), BF16),
                        pltpu.VMEM((kw // LANES, tq + WINDOW, LANES), BF16),
                        pltpu.VMEM((kw // LANES, tq + WINDOW, LANES), BF16)],
        compiler_params=_params(2),
        name="swa_attention",
    )(qkv, qkv, qkv, pos, freq, qn, kn, seg, sk)


def _post_kernel(x_ref, y_ref, p_ref, wo_ref, nf_ref, wu_ref, cw_ref, cb_ref, wd_ref, np_ref, wg_ref,
                 wp_ref, o_ref, gtail_ref, act_ref):
    tm = x_ref.shape[1]
    ff = wd_ref.shape[0]

    @pl.when(pl.program_id(1) == 0)
    def _():
        gtail_ref[...] = jnp.zeros_like(gtail_ref)

    x1 = x_ref[0] + jnp.dot(y_ref[0], wo_ref[...], preferred_element_type=F32)
    h = _rms(x1, nf_ref[...]).astype(BF16)
    for c in range(ff // FF_CHUNK):
        cs = slice(c * FF_CHUNK, (c + 1) * FF_CHUNK)
        g = jnp.dot(h, wu_ref[:, cs], preferred_element_type=F32)
        u = jnp.dot(h, wu_ref[:, ff + c * FF_CHUNK:ff + (c + 1) * FF_CHUNK], preferred_element_type=F32)
        gc = _causal_conv(g, gtail_ref[:, cs], cw_ref.at[:, cs], cb_ref.at[:, cs])
        gtail_ref[:, cs] = g[tm - SUBLANES:]
        act_ref[:, cs] = (jax.nn.gelu(gc, approximate=True) * u).astype(act_ref.dtype)
    x2 = x1 + jnp.dot(act_ref[...], wd_ref[...], preferred_element_type=F32)
    gate = jax.nn.sigmoid(jnp.dot(_rms(x2, np_ref[...]).astype(BF16), wg_ref[...], preferred_element_type=F32))
    pe = jnp.dot(p_ref[0, 0].astype(BF16), wp_ref[...], preferred_element_type=F32)
    o_ref[0] = x2 + gate * pe


def _post_mixer(x, y, p, layer, wo, nf, wu, cw, cb, wd, npl, wg, wp):
    bsz, s, d = x.shape
    tm = min(TM_POST, s)
    ff = wd.shape[0]
    return pl.pallas_call(
        _post_kernel,
        out_shape=jax.ShapeDtypeStruct((bsz, s, d), F32),
        grid=(bsz, s // tm),
        in_specs=[pl.BlockSpec((1, tm, d), lambda i, j: (i, j, 0)),
                  pl.BlockSpec((1, tm, y.shape[-1]), lambda i, j: (i, j, 0)),
                  pl.BlockSpec((1, 1, tm, p.shape[-1]), lambda i, j: (layer, i, j, 0)),
                  _resident(wo.shape), _resident((1, d)), _resident(wu.shape), _resident(cw.shape),
                  _resident((1, ff)), _resident(wd.shape), _resident((1, d)), _resident(wg.shape),
                  _resident(wp.shape)],
        out_specs=pl.BlockSpec((1, tm, d), lambda i, j: (i, j, 0)),
        scratch_shapes=[pltpu.VMEM((SUBLANES, ff), F32), pltpu.VMEM((tm, ff), BF16)],
        compiler_params=_params(2),
        name=f"post_mixer_{layer}",
    )(x, y, p, wo, nf, wu, cw, cb, wd, npl, wg, wp)


def _block_diag(w):
    n, r, _ = w.shape
    eye = jnp.eye(n, dtype=w.dtype)
    return (eye[:, None, :, None] * w[:, :, None, :]).reshape(n * r, n * r)


def kernel(x, p, positions, norm_mix, norm_ffn, norm_ple, hy_w_in, hy_b_in, rg_conv_w, rg_conv_b, rg_w_a, rg_b_a, rg_w_x, rg_b_x, rg_lambda, ml_conv_w, ml_conv_b, ml_norm, hy_w_out, at_w_qkv, at_q_norm, at_k_norm, at_sinks, at_w_out, ff_w_up, ff_conv_w, ff_conv_b, ff_w_down, ple_w_gate, ple_w_proj):
    depth = p.shape[0]
    row = lambda v: v.astype(F32)[None, :]
    n_main = 2 * RG_WIDTH + 4 * ML_WIDTH
    for layer in range(depth):
        if layer % 2 == 0:
            e = layer // 2
            pad = LANES - 2 * ML_HEADS
            w_in = jnp.pad(hy_w_in[e], ((0, 0), (0, pad))).astype(BF16)
            b_in = jnp.pad(hy_b_in[e], (0, pad)).astype(F32)[None, :]
            z, gates = _norm_proj(x, row(norm_mix[layer]), w_in, b_in,
                                  ((0, n_main), (n_main, n_main + LANES)), "in_proj")
            y_rg = _rg_lru(z, rg_conv_w[e], row(rg_conv_b[e]), _block_diag(rg_w_a[e]).astype(BF16), row(rg_b_a[e]),
                           _block_diag(rg_w_x[e]).astype(BF16), row(rg_b_x[e]), row(rg_lambda[e]))
            y_ml = _mlstm(z, gates, ml_conv_w[e], row(ml_conv_b[e]), row(ml_norm[e]))
            y = jnp.concatenate([y_rg, y_ml], axis=-1)
            w_out = hy_w_out[e]
        else:
            o = layer // 2
            n_qkv = at_w_qkv.shape[-1]
            (qkv,) = _norm_proj(x, row(norm_mix[layer]), at_w_qkv[o].astype(BF16), jnp.zeros((1, n_qkv), F32),
                                ((0, n_qkv),), "qkv_proj")
            y = _attention(qkv, positions, at_q_norm[o], at_k_norm[o], at_sinks[o])
            w_out = at_w_out[o]
        x = _post_mixer(x, y, p, layer, w_out.astype(BF16), row(norm_ffn[layer]), ff_w_up[layer].astype(BF16),
                        ff_conv_w[layer], row(ff_conv_b[layer]), ff_w_down[layer].astype(BF16),
                        row(norm_ple[layer]), ple_w_gate[layer].astype(BF16), ple_w_proj[layer].astype(BF16))
    return x
```

```python
import functools
import math

import jax
import jax.numpy as jnp
import numpy as np
from jax import lax
from jax.experimental import pallas as pl
from jax.experimental.pallas import tpu as pltpu

F32 = jnp.float32
BF16 = jnp.bfloat16

D_MODEL = 1024
PLE_DIM = 256
RG_WIDTH = 512
RG_BLOCKS = 8
RG_C = 8.0
ML_HEADS = 4
ML_HEAD_DIM = 128
ML_WIDTH = ML_HEADS * ML_HEAD_DIM
ML_CHUNK = 128
AT_HEADS = 16
AT_KV_HEADS = 4
AT_HEAD_DIM = 64
WINDOW = 128
ROPE_THETA = 10000.0
FF_DIM = 3 * D_MODEL
EPS = 1e-6

LANES = 128
SUBLANES = 8
VMEM_LIMIT_BYTES = 56 * 1024 * 1024

TM_PROJ = 512
TS_RG = 512
TS_ML = 512
TQ_ATTN = 512
TM_POST = 512
FF_CHUNK = 512


def _params(n_axes):
    return pltpu.CompilerParams(dimension_semantics=("arbitrary",) * n_axes,
                                vmem_limit_bytes=VMEM_LIMIT_BYTES)


def _resident(shape):
    n = len(shape)
    return pl.BlockSpec(shape, lambda *_: (0,) * n, pipeline_mode=pl.Buffered(1))


def _rms(x, g):
    return x * lax.rsqrt(jnp.mean(x * x, axis=-1, keepdims=True) + EPS) * g


def _shift_rows(x, tail, d):
    rolled = pltpu.roll(x, d, 0)
    tail_rolled = pltpu.roll(tail, d, 0)
    row = lax.broadcasted_iota(jnp.int32, tail.shape, 0)
    head = jnp.where(row < d, tail_rolled, rolled[:SUBLANES])
    return jnp.concatenate([head, rolled[SUBLANES:]], axis=0)


def _causal_conv(x, tail, w_ref, b_ref):
    k_w = w_ref.shape[0]
    y = x * w_ref[k_w - 1:k_w, :] + b_ref[...]
    for j in range(k_w - 1):
        y = y + _shift_rows(x, tail, k_w - 1 - j) * w_ref[j:j + 1, :]
    return y


def _log_sigmoid(x):
    return jnp.minimum(x, 0.0) - jnp.log1p(jnp.exp(-jnp.abs(x)))


def _norm_proj_kernel(x_ref, g_ref, w_ref, b_ref, *o_refs, splits):
    h = _rms(x_ref[0], g_ref[...]).astype(BF16)
    for o_ref, (lo, hi) in zip(o_refs, splits):
        z = jnp.dot(h, w_ref[:, lo:hi], preferred_element_type=F32) + b_ref[:, lo:hi]
        o_ref[0] = z.astype(o_ref.dtype)


def _norm_proj(x, g, w, b, splits, name):
    bsz, s, d = x.shape
    n = w.shape[1]
    tm = min(TM_PROJ, s)
    outs = [jax.ShapeDtypeStruct((bsz, s, hi - lo), F32) for lo, hi in splits]
    return pl.pallas_call(
        functools.partial(_norm_proj_kernel, splits=splits),
        out_shape=outs,
        grid=(bsz, s // tm),
        in_specs=[pl.BlockSpec((1, tm, d), lambda i, j: (i, j, 0)),
                  _resident((1, d)), _resident((d, n)), _resident((1, n))],
        out_specs=[pl.BlockSpec((1, tm, hi - lo), lambda i, j: (i, j, 0)) for lo, hi in splits],
        compiler_params=_params(2),
        name=name,
    )(x, g, w, b)


def _rg_kernel(x_ref, g_ref, cw_ref, cb_ref, wa_ref, ba_ref, wx_ref, bx_ref, lam_ref, o_ref,
               tail_ref, hc_ref, a_ref, u_ref):
    ts = x_ref.shape[1]

    @pl.when(pl.program_id(1) == 0)
    def _():
        tail_ref[...] = jnp.zeros_like(tail_ref)
        hc_ref[...] = jnp.zeros_like(hc_ref)

    x = x_ref[0]
    xc = _causal_conv(x, tail_ref[...], cw_ref, cb_ref)
    tail_ref[...] = x[ts - SUBLANES:]
    xb = xc.astype(BF16)
    r = jax.nn.sigmoid(jnp.dot(xb, wa_ref[...], preferred_element_type=F32) + ba_ref[...])
    i = jax.nn.sigmoid(jnp.dot(xb, wx_ref[...], preferred_element_type=F32) + bx_ref[...])
    lam = lam_ref[...]
    softplus_neg_lam = jnp.maximum(-lam, 0.0) + jnp.log1p(jnp.exp(-jnp.abs(lam)))
    log_a = (-RG_C) * r * softplus_neg_lam
    a = jnp.exp(log_a)
    a_ref[...] = a
    u_ref[...] = jnp.sqrt(1.0 - a * a) * (i * xc)

    row = lax.broadcasted_iota(jnp.int32, (SUBLANES, RG_WIDTH), 0)

    def group(k, hc):
        r0 = pl.multiple_of(k * SUBLANES, SUBLANES)
        a = a_ref[pl.ds(r0, SUBLANES), :]
        u = u_ref[pl.ds(r0, SUBLANES), :]
        for d in (1, 2, 4):
            keep = row >= d
            u = u + a * jnp.where(keep, pltpu.roll(u, d, 0), 0.0)
            a = a * jnp.where(keep, pltpu.roll(a, d, 0), 1.0)
        h = u + a * hc
        u_ref[pl.ds(r0, SUBLANES), :] = h
        return jnp.broadcast_to(h[SUBLANES - 1:SUBLANES, :], (SUBLANES, RG_WIDTH))

    hc_ref[...] = lax.fori_loop(0, ts // SUBLANES, group, hc_ref[...], unroll=4)
    o_ref[0] = (u_ref[...] * jax.nn.gelu(g_ref[0], approximate=True)).astype(o_ref.dtype)


def _rg_lru(z, cw, cb, wa, ba, wx, bx, lam):
    bsz, s, _ = z.shape
    ts = min(TS_RG, s)
    w = RG_WIDTH
    return pl.pallas_call(
        _rg_kernel,
        out_shape=jax.ShapeDtypeStruct((bsz, s, w), BF16),
        grid=(bsz, s // ts),
        in_specs=[pl.BlockSpec((1, ts, w), lambda i, j: (i, j, 0)),
                  pl.BlockSpec((1, ts, w), lambda i, j: (i, j, 1)),
                  _resident(cw.shape), _resident((1, w)), _resident((w, w)), _resident((1, w)),
                  _resident((w, w)), _resident((1, w)), _resident((1, w))],
        out_specs=pl.BlockSpec((1, ts, w), lambda i, j: (i, j, 0)),
        scratch_shapes=[pltpu.VMEM((SUBLANES, w), F32), pltpu.VMEM((SUBLANES, w), F32),
                        pltpu.VMEM((ts, w), F32), pltpu.VMEM((ts, w), F32)],
        compiler_params=_params(2),
        name="rg_lru",
    )(z, z, cw, cb, wa, ba, wx, bx, lam)


def _lane_col(x, h):
    return jnp.broadcast_to(x[:, h:h + 1], x.shape)


def _ml_kernel(q_ref, k_ref, v_ref, o_ref, gt_ref, cw_ref, cb_ref, nw_ref, y_ref,
               qtail_ref, ktail_ref, c_ref, n_ref, m_ref):
    ts = q_ref.shape[1]
    lc = ML_CHUNK
    dh = ML_HEAD_DIM

    @pl.when(pl.program_id(1) == 0)
    def _():
        qtail_ref[...] = jnp.zeros_like(qtail_ref)
        ktail_ref[...] = jnp.zeros_like(ktail_ref)
        c_ref[...] = jnp.zeros_like(c_ref)
        n_ref[...] = jnp.zeros_like(n_ref)
        m_ref[...] = jnp.full_like(m_ref, -jnp.inf)

    row = lax.broadcasted_iota(jnp.int32, (lc, LANES), 0)
    col = lax.broadcasted_iota(jnp.int32, (lc, LANES), 1)
    causal = row >= col

    def chunk(c, carry):
        r0 = pl.multiple_of(c * lc, lc)
        rows = pl.ds(r0, lc)
        q_raw = q_ref[0, rows, :]
        k_raw = k_ref[0, rows, :]
        q = jax.nn.silu(_causal_conv(q_raw, qtail_ref[...], cw_ref.at[:, :ML_WIDTH], cb_ref.at[:, :ML_WIDTH]))
        k = jax.nn.silu(_causal_conv(k_raw, ktail_ref[...], cw_ref.at[:, ML_WIDTH:], cb_ref.at[:, ML_WIDTH:]))
        qtail_ref[...] = q_raw[lc - SUBLANES:]
        ktail_ref[...] = k_raw[lc - SUBLANES:]
        k = k * (dh ** -0.5)

        gates = gt_ref[0, rows, :]
        b = _log_sigmoid(gates)
        d = 1
        while d < lc:
            b = b + jnp.where(row >= d, pltpu.roll(b, d, 0), 0.0)
            d *= 2
        b = pltpu.roll(b, LANES - ML_HEADS, 1)
        r = gates - b
        b_last = b[lc - 1:lc, :]
        g_loc = b_last + r
        m_loc = jnp.max(g_loc, axis=0, keepdims=True)
        w_loc = jnp.exp(g_loc - m_loc)
        m_st = m_ref[0:1, :]
        m_new = jnp.maximum(b_last + m_st, m_loc)
        a_prev = jnp.exp(b_last + m_st - m_new)
        a_loc = jnp.exp(m_loc - m_new)
        rmax = r
        d = 1
        while d < lc:
            rmax = jnp.maximum(rmax, jnp.where(row >= d, pltpu.roll(rmax, d, 0), -jnp.inf))
            d *= 2
        e = -jnp.maximum(m_st, rmax)
        a_inter = jnp.exp(m_st + e)
        exp_neg_mt = jnp.exp(e - b)
        r_t = r.T

        for h in range(ML_HEADS):
            hs = slice(h * dh, (h + 1) * dh)
            qh = q[:, hs]
            kh = k[:, hs]
            vh = v_ref[0, rows, hs]
            qb = qh.astype(BF16)
            kb = kh.astype(BF16)
            vb = vh.astype(BF16)
            c_prev = c_ref[h]
            n_prev = n_ref[h:h + 1, :]
            w_intra = jnp.where(causal, jnp.exp(_lane_col(e, h) + r_t[h:h + 1, :]), 0.0)
            s_qk = lax.dot_general(qb, kb, (((1,), (1,)), ((), ())), preferred_element_type=F32) * w_intra
            a_in = _lane_col(a_inter, h)
            num = (jnp.dot(s_qk.astype(BF16), vb, preferred_element_type=F32)
                   + a_in * jnp.dot(qb, c_prev.astype(BF16), preferred_element_type=F32))
            den = jnp.sum(s_qk + a_in * (qh * n_prev), axis=-1, keepdims=True)
            hm = num / jnp.maximum(jnp.abs(den), exp_neg_mt[:, h:h + 1])
            hm = hm * lax.rsqrt(jnp.mean(hm * hm, axis=-1, keepdims=True) + EPS) * nw_ref[:, hs]
            y = jax.nn.sigmoid(o_ref[0, rows, hs]) * hm
            y_ref[0, rows, hs] = y.astype(y_ref.dtype)

            kw = kh * _lane_col(w_loc, h)
            c_loc = jnp.dot(kw.T.astype(BF16), vb, preferred_element_type=F32)
            n_loc = jnp.sum(kw, axis=0, keepdims=True)
            ap = a_prev[:, h:h + 1]
            al = a_loc[:, h:h + 1]
            c_ref[h] = ap * c_prev + al * c_loc
            n_ref[h:h + 1, :] = ap * n_prev + al * n_loc
        m_ref[...] = jnp.broadcast_to(m_new, m_ref.shape)
        return carry

    lax.fori_loop(0, ts // lc, chunk, 0)


def _mlstm(z, gates, cw, cb, nw):
    bsz, s, _ = z.shape
    ts = min(TS_ML, s)
    w = ML_WIDTH
    col0 = 2 * RG_WIDTH // w
    return pl.pallas_call(
        _ml_kernel,
        out_shape=jax.ShapeDtypeStruct((bsz, s, w), BF16),
        grid=(bsz, s // ts),
        in_specs=[pl.BlockSpec((1, ts, w), lambda i, j: (i, j, col0)),
                  pl.BlockSpec((1, ts, w), lambda i, j: (i, j, col0 + 1)),
                  pl.BlockSpec((1, ts, w), lambda i, j: (i, j, col0 + 2)),
                  pl.BlockSpec((1, ts, w), lambda i, j: (i, j, col0 + 3)),
                  pl.BlockSpec((1, ts, LANES), lambda i, j: (i, j, 0)),
                  _resident(cw.shape), _resident(cb.shape), _resident(nw.shape)],
        out_specs=pl.BlockSpec((1, ts, w), lambda i, j: (i, j, 0)),
        scratch_shapes=[pltpu.VMEM((SUBLANES, w), F32), pltpu.VMEM((SUBLANES, w), F32),
                        pltpu.VMEM((ML_HEADS, ML_HEAD_DIM, ML_HEAD_DIM), F32),
                        pltpu.VMEM((SUBLANES, ML_HEAD_DIM), F32),
                        pltpu.VMEM((SUBLANES, LANES), F32)],
        compiler_params=_params(2),
        name="mlstm",
    )(z, z, z, z, gates, cw, cb, nw)


def _swap_halves(x, period):
    half = period // 2
    lane = lax.broadcasted_iota(jnp.int32, x.shape, 1)
    lo = (lane % period) < half
    return jnp.where(lo, pltpu.roll(x, LANES - half, 1), pltpu.roll(x, half, 1))


def _attn_kernel(q_ref, k_ref, v_ref, pos_ref, freq_ref, qn_ref, kn_ref, seg_ref, sink_ref, o_ref,
                 qs_ref, kf_ref, vt_ref):
    tq = q_ref.shape[1]
    w = WINDOW
    hd = AT_HEAD_DIM
    n_sub = tq // w
    n_qslab = AT_HEADS * hd // LANES
    n_kslab = AT_KV_HEADS * hd // LANES
    n_pack = LANES // (hd // 2)
    rp = w // n_pack
    first = pl.program_id(1) == 0

    @pl.when(first)
    def _():
        kf_ref[:, :, 0:w, :] = jnp.zeros((AT_KV_HEADS, 2, w, LANES), kf_ref.dtype)
        vt_ref[:, :, 0:w] = jnp.zeros((AT_KV_HEADS, hd, w), vt_ref.dtype)

    lane = lax.broadcasted_iota(jnp.int32, (w, LANES), 1)
    row = lax.broadcasted_iota(jnp.int32, (w, LANES), 0)
    lo_lanes = lane < hd
    sin_sign = jnp.where((lane % hd) < hd // 2, -1.0, 1.0)
    lane_grp = lax.broadcasted_iota(jnp.int32, (rp, LANES), 1) // (hd // 2)
    seg = seg_ref[...]

    def norm_rope(x, gain, cos, sin):
        ms = jnp.dot((x * x).astype(BF16), seg, preferred_element_type=F32) * (1.0 / hd)
        xn = x * lax.rsqrt(ms + EPS) * gain
        return xn * cos + _swap_halves(xn, hd) * sin

    def spread(t):
        parts = []
        for g in range(n_pack):
            z = jnp.where(lane_grp == g, t, 0.0)
            y = z
            for k in range(1, n_pack):
                y = y + pltpu.roll(z, k * (hd // 2), 1)
            parts.append(y)
        return jnp.concatenate(parts, axis=0)

    for i in range(n_sub):
        rows = slice(i * w, (i + 1) * w)
        krows = slice(w + i * w, w + (i + 1) * w)
        pos = pos_ref[0, 0, i:i + 1, :].astype(F32)
        pos_col = jnp.broadcast_to(pos, (w, LANES)).T
        packed = pos_col[0:rp]
        for g in range(1, n_pack):
            packed = jnp.where(lane_grp == g, pos_col[g * rp:(g + 1) * rp], packed)
        ang = packed * freq_ref[...]
        cos = spread(jnp.cos(ang))
        sin = spread(jnp.sin(ang)) * sin_sign
        for sl in range(n_qslab):
            cs = slice(sl * LANES, (sl + 1) * LANES)
            xq = norm_rope(q_ref[0, rows, cs], qn_ref[...], cos, sin) * (hd ** -0.5)
            qs_ref[sl, rows, :] = xq.astype(qs_ref.dtype)
        for sl in range(n_kslab):
            cs = slice(sl * LANES, (sl + 1) * LANES)
            xk = norm_rope(k_ref[0, rows, cs], kn_ref[...], cos, sin)
            xk_sw = pltpu.roll(xk, hd, 1)
            kf_ref[2 * sl, 0, krows, :] = jnp.where(lo_lanes, xk, 0.0).astype(kf_ref.dtype)
            kf_ref[2 * sl, 1, krows, :] = jnp.where(lo_lanes, 0.0, xk_sw).astype(kf_ref.dtype)
            kf_ref[2 * sl + 1, 0, krows, :] = jnp.where(lo_lanes, xk_sw, 0.0).astype(kf_ref.dtype)
            kf_ref[2 * sl + 1, 1, krows, :] = jnp.where(lo_lanes, 0.0, xk).astype(kf_ref.dtype)
            vt = v_ref[0, rows, cs].T.astype(vt_ref.dtype)
            vt_ref[2 * sl, :, krows] = vt[:hd]
            vt_ref[2 * sl + 1, :, krows] = vt[hd:]

    grp = AT_HEADS // AT_KV_HEADS
    row4 = lax.broadcasted_iota(jnp.int32, (w, grp * w), 0)
    lane4 = lax.broadcasted_iota(jnp.int32, (w, grp * w), 1)
    from_prev = row4 > (lane4 % w)
    nt = (((1,), (1,)), ((), ()))
    for i in range(n_sub):
        rows = slice(i * w, (i + 1) * w)
        win = slice(i * w, (i + 2) * w)
        if i == 0:
            no_prev = jnp.where(first, -jnp.inf, 0.0)
        for hk in range(AT_KV_HEADS):
            q2 = jnp.concatenate([qs_ref[2 * hk, rows, :], qs_ref[2 * hk + 1, rows, :]], axis=0)
            s = jnp.concatenate(
                [lax.dot_general(kf_ref[hk, 0, win, :], q2, nt, preferred_element_type=F32),
                 lax.dot_general(kf_ref[hk, 1, win, :], q2, nt, preferred_element_type=F32)], axis=1)
            s_prev = s[:w]
            if i == 0:
                s_prev = s_prev + no_prev
            logits = jnp.where(from_prev, s_prev, s[w:])
            sink = sink_ref[hk:hk + 1, :]
            m = jnp.maximum(jnp.max(logits, axis=0, keepdims=True), sink)
            p = jnp.exp(logits - m)
            denom = jnp.sum(p, axis=0, keepdims=True) + jnp.exp(sink - m)
            pn = p * (1.0 / denom)
            pp = jnp.concatenate([jnp.where(from_prev, pn, 0.0), jnp.where(from_prev, 0.0, pn)],
                                 axis=0).astype(BF16)
            r = jnp.dot(vt_ref[hk, :, win], pp, preferred_element_type=F32)
            for pair in range(grp // 2):
                out_t = jnp.concatenate([r[:, pair * w:(pair + 1) * w],
                                         r[:, (2 + pair) * w:(3 + pair) * w]], axis=0)
                sl = 2 * hk + pair
                o_ref[0, rows, sl * LANES:(sl + 1) * LANES] = out_t.T.astype(o_ref.dtype)

    kf_ref[:, :, 0:w, :] = kf_ref[:, :, tq:tq + w, :]
    vt_ref[:, :, 0:w] = vt_ref[:, :, tq:tq + w]


def _attention(qkv, positions, q_norm, k_norm, sinks):
    bsz, s, _ = qkv.shape
    tq = min(TQ_ATTN, s)
    qw = AT_HEADS * AT_HEAD_DIM
    kw = AT_KV_HEADS * AT_HEAD_DIM
    half = AT_HEAD_DIM // 2
    inv_freq = ROPE_THETA ** (-jnp.arange(half, dtype=F32) * (2.0 / AT_HEAD_DIM))
    freq = jnp.tile(inv_freq, LANES // half)[None, :]
    seg_id = np.arange(LANES) // AT_HEAD_DIM
    seg = jnp.asarray((seg_id[:, None] == seg_id[None, :]).astype(np.float32), dtype=BF16)
    pos = positions.reshape(bsz, s // tq, tq // WINDOW, WINDOW)
    qn = jnp.tile(q_norm.astype(F32), LANES // AT_HEAD_DIM)[None, :]
    kn = jnp.tile(k_norm.astype(F32), LANES // AT_HEAD_DIM)[None, :]
    grp = AT_HEADS // AT_KV_HEADS
    sk = sinks.astype(F32).reshape(AT_KV_HEADS, grp)[:, jnp.array([0, 2, 1, 3])]
    sk = jnp.repeat(sk, WINDOW, axis=1)
    return pl.pallas_call(
        _attn_kernel,
        out_shape=jax.ShapeDtypeStruct((bsz, s, qw), BF16),
        grid=(bsz, s // tq),
        in_specs=[pl.BlockSpec((1, tq, qw), lambda i, j: (i, j, 0)),
                  pl.BlockSpec((1, tq, kw), lambda i, j: (i, j, qw // kw)),
                  pl.BlockSpec((1, tq, kw), lambda i, j: (i, j, qw // kw + 1)),
                  pl.BlockSpec((1, 1, tq // WINDOW, WINDOW), lambda i, j: (i, j, 0, 0)),
                  _resident((1, LANES)), _resident((1, LANES)), _resident((1, LANES)),
                  _resident((LANES, LANES)), _resident((AT_KV_HEADS, grp * WINDOW))],
        out_specs=pl.BlockSpec((1, tq, qw), lambda i, j: (i, j, 0)),
        scratch_shapes=[pltpu.VMEM((qw // LANES, tq, LANES), BF16),
                        pltpu.VMEM((AT_KV_HEADS, 2, tq + WINDOW, LANES), BF16),
                        pltpu.VMEM((AT_KV_HEADS, AT_HEAD_DIM, tq + WINDOW), BF16)],
        compiler_params=_params(2),
        name="swa_attention",
    )(qkv, qkv, qkv, pos, freq, qn, kn, seg, sk)


def _post_kernel(x_ref, y_ref, p_ref, wo_ref, nf_ref, wu_ref, cw_ref, cb_ref, wd_ref, np_ref, wg_ref,
                 wp_ref, o_ref, gtail_ref, act_ref):
    tm = x_ref.shape[1]
    ff = wd_ref.shape[0]

    @pl.when(pl.program_id(1) == 0)
    def _():
        gtail_ref[...] = jnp.zeros_like(gtail_ref)

    x1 = x_ref[0] + jnp.dot(y_ref[0], wo_ref[...], preferred_element_type=F32)
    h = _rms(x1, nf_ref[...]).astype(BF16)
    for c in range(ff // FF_CHUNK):
        cs = slice(c * FF_CHUNK, (c + 1) * FF_CHUNK)
        g = jnp.dot(h, wu_ref[:, cs], preferred_element_type=F32)
        u = jnp.dot(h, wu_ref[:, ff + c * FF_CHUNK:ff + (c + 1) * FF_CHUNK], preferred_element_type=F32)
        gc = _causal_conv(g, gtail_ref[:, cs], cw_ref.at[:, cs], cb_ref.at[:, cs])
        gtail_ref[:, cs] = g[tm - SUBLANES:]
        act_ref[:, cs] = (jax.nn.gelu(gc, approximate=True) * u).astype(act_ref.dtype)
    x2 = x1 + jnp.dot(act_ref[...], wd_ref[...], preferred_element_type=F32)
    gate = jax.nn.sigmoid(jnp.dot(_rms(x2, np_ref[...]).astype(BF16), wg_ref[...], preferred_element_type=F32))
    pe = jnp.dot(p_ref[0, 0].astype(BF16), wp_ref[...], preferred_element_type=F32)
    o_ref[0] = x2 + gate * pe


def _post_mixer(x, y, p, layer, wo, nf, wu, cw, cb, wd, npl, wg, wp):
    bsz, s, d = x.shape
    tm = min(TM_POST, s)
    ff = wd.shape[0]
    return pl.pallas_call(
        _post_kernel,
        out_shape=jax.ShapeDtypeStruct((bsz, s, d), F32),
        grid=(bsz, s // tm),
        in_specs=[pl.BlockSpec((1, tm, d), lambda i, j: (i, j, 0)),
                  pl.BlockSpec((1, tm, y.shape[-1]), lambda i, j: (i, j, 0)),
                  pl.BlockSpec((1, 1, tm, p.shape[-1]), lambda i, j: (layer, i, j, 0)),
                  _resident(wo.shape), _resident((1, d)), _resident(wu.shape), _resident(cw.shape),
                  _resident((1, ff)), _resident(wd.shape), _resident((1, d)), _resident(wg.shape),
                  _resident(wp.shape)],
        out_specs=pl.BlockSpec((1, tm, d), lambda i, j: (i, j, 0)),
        scratch_shapes=[pltpu.VMEM((SUBLANES, ff), F32), pltpu.VMEM((tm, ff), BF16)],
        compiler_params=_params(2),
        name=f"post_mixer_{layer}",
    )(x, y, p, wo, nf, wu, cw, cb, wd, npl, wg, wp)


def _block_diag(w):
    n, r, _ = w.shape
    eye = jnp.eye(n, dtype=w.dtype)
    return (eye[:, None, :, None] * w[:, :, None, :]).reshape(n * r, n * r)


def kernel(x, p, positions, norm_mix, norm_ffn, norm_ple, hy_w_in, hy_b_in, rg_conv_w, rg_conv_b, rg_w_a, rg_b_a, rg_w_x, rg_b_x, rg_lambda, ml_conv_w, ml_conv_b, ml_norm, hy_w_out, at_w_qkv, at_q_norm, at_k_norm, at_sinks, at_w_out, ff_w_up, ff_conv_w, ff_conv_b, ff_w_down, ple_w_gate, ple_w_proj):
    depth = p.shape[0]
    row = lambda v: v.astype(F32)[None, :]
    n_main = 2 * RG_WIDTH + 4 * ML_WIDTH
    for layer in range(depth):
        if layer % 2 == 0:
            e = layer // 2
            pad = LANES - 2 * ML_HEADS
            w_in = jnp.pad(hy_w_in[e], ((0, 0), (0, pad))).astype(BF16)
            b_in = jnp.pad(hy_b_in[e], (0, pad)).astype(F32)[None, :]
            z, gates = _norm_proj(x, row(norm_mix[layer]), w_in, b_in,
                                  ((0, n_main), (n_main, n_main + LANES)), "in_proj")
            y_rg = _rg_lru(z, rg_conv_w[e], row(rg_conv_b[e]), _block_diag(rg_w_a[e]).astype(BF16), row(rg_b_a[e]),
                           _block_diag(rg_w_x[e]).astype(BF16), row(rg_b_x[e]), row(rg_lambda[e]))
            y_ml = _mlstm(z, gates, ml_conv_w[e], row(ml_conv_b[e]), row(ml_norm[e]))
            y = jnp.concatenate([y_rg, y_ml], axis=-1)
            w_out = hy_w_out[e]
        else:
            o = layer // 2
            n_qkv = at_w_qkv.shape[-1]
            (qkv,) = _norm_proj(x, row(norm_mix[layer]), at_w_qkv[o].astype(BF16), jnp.zeros((1, n_qkv), F32),
                                ((0, n_qkv),), "qkv_proj")
            y = _attention(qkv, positions, at_q_norm[o], at_k_norm[o], at_sinks[o])
            w_out = at_w_out[o]
        x = _post_mixer(x, y, p, layer, w_out.astype(BF16), row(norm_ffn[layer]), ff_w_up[layer].astype(BF16),
                        ff_conv_w[layer], row(ff_conv_b[layer]), ff_w_down[layer].astype(BF16),
                        row(norm_ple[layer]), ple_w_gate[layer].astype(BF16), ple_w_proj[layer].astype(BF16))
    return x
```

```python
import jax
import jax.numpy as jnp
import numpy as np
from jax import lax
from jax.experimental import pallas as pl
from jax.experimental.pallas import tpu as pltpu

F32 = jnp.float32
BF16 = jnp.bfloat16

D_MODEL = 1024
PLE_DIM = 256
RG_WIDTH = 512
RG_BLOCKS = 8
RG_C = 8.0
ML_HEADS = 4
ML_HEAD_DIM = 128
ML_WIDTH = ML_HEADS * ML_HEAD_DIM
ML_CHUNK = 128
AT_HEADS = 16
AT_KV_HEADS = 4
AT_HEAD_DIM = 64
WINDOW = 128
ROPE_THETA = 10000.0
FF_DIM = 3 * D_MODEL
EPS = 1e-6

LANES = 128
SUBLANES = 8
VMEM_LIMIT_BYTES = 56 * 1024 * 1024

TS_MIX = 512
TM_POST = 512
FF_CHUNK = 512
PROJ_CHUNK = 512


def _params(n_axes):
    return pltpu.CompilerParams(dimension_semantics=("arbitrary",) * n_axes,
                                vmem_limit_bytes=VMEM_LIMIT_BYTES)


def _resident(shape, index=None):
    n = len(shape)
    idx = (0,) * n if index is None else tuple(index)
    return pl.BlockSpec(shape, lambda *_: idx, pipeline_mode=pl.Buffered(1))


def _layer_block(arr, layer):
    shape = (None,) + arr.shape[1:]
    return _resident(shape, (layer,) + (0,) * (arr.ndim - 1))


def _rms(x, g):
    return x * lax.rsqrt(jnp.mean(x * x, axis=-1, keepdims=True) + EPS) * g


def _shift_rows(x, tail, d):
    rolled = pltpu.roll(x, d, 0)
    tail_rolled = pltpu.roll(tail, d, 0)
    row = lax.broadcasted_iota(jnp.int32, tail.shape, 0)
    head = jnp.where(row < d, tail_rolled, rolled[:SUBLANES])
    return jnp.concatenate([head, rolled[SUBLANES:]], axis=0)


def _causal_conv(x, tail, w_ref, b_ref):
    k_w = w_ref.shape[0]
    y = x * w_ref[k_w - 1:k_w, :] + b_ref[...]
    for j in range(k_w - 1):
        y = y + _shift_rows(x, tail, k_w - 1 - j) * w_ref[j:j + 1, :]
    return y


def _log_sigmoid(x):
    return jnp.minimum(x, 0.0) - jnp.log1p(jnp.exp(-jnp.abs(x)))


def _rg_gates(z_ref, cw_ref, cb_ref, wa_ref, ba_ref, wx_ref, bx_ref, lam_ref, tail_ref, a_ref, u_ref):
    ts = z_ref.shape[0]
    x = z_ref[:, 0:RG_WIDTH]
    xc = _causal_conv(x, tail_ref[...], cw_ref, cb_ref)
    tail_ref[...] = x[ts - SUBLANES:]
    xb = xc.astype(BF16)
    r = jax.nn.sigmoid(jnp.dot(xb, wa_ref[...], preferred_element_type=F32) + ba_ref[...])
    i = jax.nn.sigmoid(jnp.dot(xb, wx_ref[...], preferred_element_type=F32) + bx_ref[...])
    lam = lam_ref[...]
    softplus_neg_lam = jnp.maximum(-lam, 0.0) + jnp.log1p(jnp.exp(-jnp.abs(lam)))
    log_a = (-RG_C) * r * softplus_neg_lam
    a = jnp.exp(log_a)
    a_ref[...] = a
    u_ref[...] = jnp.sqrt(1.0 - a * a) * (i * xc)


def _rg_scan(z_ref, y_ref, hc_ref, a_ref, u_ref):
    ts = z_ref.shape[0]
    row = lax.broadcasted_iota(jnp.int32, (SUBLANES, RG_WIDTH), 0)
    hc = hc_ref[...]
    for k in range(ts // SUBLANES):
        rows = slice(k * SUBLANES, (k + 1) * SUBLANES)
        a = a_ref[rows, :]
        u = u_ref[rows, :]
        for d in (1, 2, 4):
            keep = row >= d
            u = u + a * jnp.where(keep, pltpu.roll(u, d, 0), 0.0)
            a = a * jnp.where(keep, pltpu.roll(a, d, 0), 1.0)
        h = u + a * hc
        u_ref[rows, :] = h
        hc = jnp.broadcast_to(h[SUBLANES - 1:SUBLANES, :], (SUBLANES, RG_WIDTH))
    hc_ref[...] = hc
    gate = jax.nn.gelu(z_ref[:, RG_WIDTH:2 * RG_WIDTH], approximate=True)
    y_ref[:, 0:RG_WIDTH] = (u_ref[...] * gate).astype(y_ref.dtype)


def _lane_col(x, h):
    return jnp.broadcast_to(x[:, h:h + 1], x.shape)


def _ml_chunk(rows, z_ref, gt_ref, cw_ref, cb_ref, nw_ref, y_ref, qtail_ref, ktail_ref, c_ref, n_ref, m_ref):
    lc = ML_CHUNK
    dh = ML_HEAD_DIM
    q0 = 2 * RG_WIDTH
    k0, v0, o0 = q0 + ML_WIDTH, q0 + 2 * ML_WIDTH, q0 + 3 * ML_WIDTH
    row = lax.broadcasted_iota(jnp.int32, (lc, LANES), 0)
    col = lax.broadcasted_iota(jnp.int32, (lc, LANES), 1)
    causal = row >= col

    q_raw = z_ref[rows, q0:k0]
    k_raw = z_ref[rows, k0:v0]
    q = jax.nn.silu(_causal_conv(q_raw, qtail_ref[...], cw_ref.at[:, :ML_WIDTH], cb_ref.at[:, :ML_WIDTH]))
    k = jax.nn.silu(_causal_conv(k_raw, ktail_ref[...], cw_ref.at[:, ML_WIDTH:], cb_ref.at[:, ML_WIDTH:]))
    qtail_ref[...] = q_raw[lc - SUBLANES:]
    ktail_ref[...] = k_raw[lc - SUBLANES:]
    k = k * (dh ** -0.5)

    gates = gt_ref[rows, :]
    b = _log_sigmoid(gates)
    d = 1
    while d < lc:
        b = b + jnp.where(row >= d, pltpu.roll(b, d, 0), 0.0)
        d *= 2
    b = pltpu.roll(b, LANES - ML_HEADS, 1)
    r = gates - b
    b_last = b[lc - 1:lc, :]
    g_loc = b_last + r
    m_loc = jnp.max(g_loc, axis=0, keepdims=True)
    w_loc = jnp.exp(g_loc - m_loc)
    m_st = m_ref[0:1, :]
    m_new = jnp.maximum(b_last + m_st, m_loc)
    a_prev = jnp.exp(b_last + m_st - m_new)
    a_loc = jnp.exp(m_loc - m_new)
    rmax = r
    d = 1
    while d < lc:
        rmax = jnp.maximum(rmax, jnp.where(row >= d, pltpu.roll(rmax, d, 0), -jnp.inf))
        d *= 2
    e = -jnp.maximum(m_st, rmax)
    a_inter = jnp.exp(m_st + e)
    exp_neg_mt = jnp.exp(e - b)
    r_t = r.T

    for h in range(ML_HEADS):
        hs = slice(h * dh, (h + 1) * dh)
        qh = q[:, hs]
        kh = k[:, hs]
        vh = z_ref[rows, v0 + h * dh:v0 + (h + 1) * dh]
        qb = qh.astype(BF16)
        kb = kh.astype(BF16)
        vb = vh.astype(BF16)
        c_prev = c_ref[h]
        n_prev = n_ref[h:h + 1, :]
        w_intra = jnp.where(causal, jnp.exp(_lane_col(e, h) + r_t[h:h + 1, :]), 0.0)
        s_qk = lax.dot_general(qb, kb, (((1,), (1,)), ((), ())), preferred_element_type=F32) * w_intra
        a_in = _lane_col(a_inter, h)
        num = (jnp.dot(s_qk.astype(BF16), vb, preferred_element_type=F32)
               + a_in * jnp.dot(qb, c_prev.astype(BF16), preferred_element_type=F32))
        den = jnp.sum(s_qk + a_in * (qh * n_prev), axis=-1, keepdims=True)
        hm = num / jnp.maximum(jnp.abs(den), exp_neg_mt[:, h:h + 1])
        hm = hm * lax.rsqrt(jnp.mean(hm * hm, axis=-1, keepdims=True) + EPS) * nw_ref[:, hs]
        y = jax.nn.sigmoid(z_ref[rows, o0 + h * dh:o0 + (h + 1) * dh]) * hm
        y_ref[rows, RG_WIDTH + h * dh:RG_WIDTH + (h + 1) * dh] = y.astype(y_ref.dtype)

        kw = kh * _lane_col(w_loc, h)
        c_loc = jnp.dot(kw.T.astype(BF16), vb, preferred_element_type=F32)
        n_loc = jnp.sum(kw, axis=0, keepdims=True)
        ap = a_prev[:, h:h + 1]
        al = a_loc[:, h:h + 1]
        c_ref[h] = ap * c_prev + al * c_loc
        n_ref[h:h + 1, :] = ap * n_prev + al * n_loc
    m_ref[...] = jnp.broadcast_to(m_new, m_ref.shape)


def _hybrid_kernel(x_ref, nm_ref, win_ref, bin_ref, wgt_ref, bgt_ref,
                   rcw_ref, rcb_ref, wa_ref, ba_ref, wx_ref, bx_ref, lam_ref,
                   mcw_ref, mcb_ref, mnw_ref, wo_ref, o_ref,
                   z_ref, gt_ref, y_ref, rtail_ref, hc_ref, a_ref, u_ref,
                   qtail_ref, ktail_ref, c_ref, n_ref, m_ref):
    @pl.when(pl.program_id(1) == 0)
    def _():
        for ref in (rtail_ref, hc_ref, qtail_ref, ktail_ref, c_ref, n_ref):
            ref[...] = jnp.zeros_like(ref)
        m_ref[...] = jnp.full_like(m_ref, -jnp.inf)

    h = _rms(x_ref[0], nm_ref[...]).astype(BF16)
    n_main = win_ref.shape[1]
    n_rg = 2 * RG_WIDTH

    def project(lo, hi):
        for c in range(lo // PROJ_CHUNK, hi // PROJ_CHUNK):
            cs = slice(c * PROJ_CHUNK, (c + 1) * PROJ_CHUNK)
            z_ref[:, cs] = jnp.dot(h, win_ref[:, cs], preferred_element_type=F32) + bin_ref[:, cs]

    project(0, n_rg)
    _rg_gates(z_ref, rcw_ref, rcb_ref, wa_ref, ba_ref, wx_ref, bx_ref, lam_ref, rtail_ref, a_ref, u_ref)
    project(n_rg, n_main)
    gt_ref[...] = jnp.dot(h, wgt_ref[...], preferred_element_type=F32) + bgt_ref[...]
    _rg_scan(z_ref, y_ref, hc_ref, a_ref, u_ref)
    for c in range(x_ref.shape[1] // ML_CHUNK):
        rows = slice(c * ML_CHUNK, (c + 1) * ML_CHUNK)
        _ml_chunk(rows, z_ref, gt_ref, mcw_ref, mcb_ref, mnw_ref, y_ref, qtail_ref, ktail_ref, c_ref, n_ref, m_ref)
        o_ref[0, rows, :] = x_ref[0, rows, :] + jnp.dot(y_ref[rows, :], wo_ref[...], preferred_element_type=F32)


def _hybrid_mixer(x, nm, w_in, b_in, w_gt, b_gt, rcw, rcb, wa, ba, wx, bx, lam, mcw, mcb, mnw, wo):
    bsz, s, d = x.shape
    ts = min(TS_MIX, s)
    n_main = w_in.shape[1]
    rw, mw = RG_WIDTH, ML_WIDTH
    consts = (nm, w_in, b_in, w_gt, b_gt, rcw, rcb, wa, ba, wx, bx, lam, mcw, mcb, mnw, wo)
    return pl.pallas_call(
        _hybrid_kernel,
        out_shape=jax.ShapeDtypeStruct((bsz, s, d), F32),
        grid=(bsz, s // ts),
        in_specs=[pl.BlockSpec((1, ts, d), lambda i, j: (i, j, 0))] + [_resident(c.shape) for c in consts],
        out_specs=pl.BlockSpec((1, ts, d), lambda i, j: (i, j, 0)),
        scratch_shapes=[pltpu.VMEM((ts, n_main), F32), pltpu.VMEM((ts, LANES), F32),
                        pltpu.VMEM((ts, rw + mw), BF16),
                        pltpu.VMEM((SUBLANES, rw), F32), pltpu.VMEM((SUBLANES, rw), F32),
                        pltpu.VMEM((ts, rw), F32), pltpu.VMEM((ts, rw), F32),
                        pltpu.VMEM((SUBLANES, mw), F32), pltpu.VMEM((SUBLANES, mw), F32),
                        pltpu.VMEM((ML_HEADS, ML_HEAD_DIM, ML_HEAD_DIM), F32),
                        pltpu.VMEM((SUBLANES, ML_HEAD_DIM), F32),
                        pltpu.VMEM((SUBLANES, LANES), F32)],
        compiler_params=_params(2),
        name="hybrid_mixer",
    )(x, *consts)


def _attn_kernel(x_ref, nm_ref, wqkv_ref, pos_ref, freq_ref, qn_ref, kn_ref, seg_ref, sink_ref, wo_ref,
                 o_ref, qs_ref, kf_ref, vt_ref, att_ref):
    tq = x_ref.shape[1]
    w = WINDOW
    hd = AT_HEAD_DIM
    hh = hd // 2
    n_sub = tq // w
    qw = AT_HEADS * hd
    kw = AT_KV_HEADS * hd
    n_qslab = qw // LANES
    n_kslab = kw // LANES
    n_pack = LANES // hh
    rp = w // n_pack
    first = pl.program_id(1) == 0

    @pl.when(first)
    def _():
        kf_ref[:, :, 0:w, :] = jnp.zeros((AT_KV_HEADS, 2, w, LANES), kf_ref.dtype)
        vt_ref[:, :, 0:w] = jnp.zeros((AT_KV_HEADS, hd, w), vt_ref.dtype)

    lane = lax.broadcasted_iota(jnp.int32, (w, LANES), 1)
    slot_a = (lane // hh) % 2 == 0
    sin_sign = jnp.where(lane < hd, -1.0, 1.0)
    lane_grp = lax.broadcasted_iota(jnp.int32, (rp, LANES), 1) // hh
    seg = seg_ref[...]

    qkv_all = jnp.dot(_rms(x_ref[0], nm_ref[...]).astype(BF16), wqkv_ref[...], preferred_element_type=F32)

    def norm_rope(t, gain, cos, sin):
        ms = jnp.dot((t * t).astype(BF16), seg, preferred_element_type=F32) * (1.0 / hd)
        tn = t * lax.rsqrt(ms + EPS) * gain
        return tn * cos + pltpu.roll(tn, hd, 1) * sin

    def spread(t):
        parts = []
        for g in range(n_pack):
            z = jnp.where(lane_grp == g, t, 0.0)
            y = z
            for k in range(1, n_pack):
                y = y + pltpu.roll(z, k * hh, 1)
            parts.append(y)
        return jnp.concatenate(parts, axis=0)

    for i in range(n_sub):
        rows = slice(i * w, (i + 1) * w)
        krows = slice(w + i * w, w + (i + 1) * w)
        qkv = qkv_all[rows]
        pos = pos_ref[0, 0, i:i + 1, :].astype(F32)
        pos_col = jnp.broadcast_to(pos, (w, LANES)).T
        packed = pos_col[0:rp]
        for g in range(1, n_pack):
            packed = jnp.where(lane_grp == g, pos_col[g * rp:(g + 1) * rp], packed)
        ang = packed * freq_ref[...]
        cos = spread(jnp.cos(ang))
        sin = spread(jnp.sin(ang)) * sin_sign
        for sl in range(n_qslab):
            xq = norm_rope(qkv[:, sl * LANES:(sl + 1) * LANES], qn_ref[...], cos, sin) * (hd ** -0.5)
            qs_ref[sl, rows, :] = xq.astype(qs_ref.dtype)
        for sl in range(n_kslab):
            xk = norm_rope(qkv[:, qw + sl * LANES:qw + (sl + 1) * LANES], kn_ref[...], cos, sin)
            kf_ref[2 * sl, 0, krows, :] = jnp.where(slot_a, xk, 0.0).astype(kf_ref.dtype)
            kf_ref[2 * sl, 1, krows, :] = jnp.where(slot_a, 0.0, pltpu.roll(xk, hh, 1)).astype(kf_ref.dtype)
            kf_ref[2 * sl + 1, 0, krows, :] = jnp.where(slot_a, pltpu.roll(xk, LANES - hh, 1), 0.0).astype(kf_ref.dtype)
            kf_ref[2 * sl + 1, 1, krows, :] = jnp.where(slot_a, 0.0, xk).astype(kf_ref.dtype)
            v0 = qw + kw + sl * LANES
            vt = qkv[:, v0:v0 + LANES].T.astype(vt_ref.dtype)
            vt_ref[2 * sl, :, krows] = vt[:hd]
            vt_ref[2 * sl + 1, :, krows] = vt[hd:]

    grp = AT_HEADS // AT_KV_HEADS
    row4 = lax.broadcasted_iota(jnp.int32, (w, grp * w), 0)
    lane4 = lax.broadcasted_iota(jnp.int32, (w, grp * w), 1)
    from_prev = row4 > (lane4 % w)
    nt = (((1,), (1,)), ((), ()))
    for i in range(n_sub):
        rows = slice(i * w, (i + 1) * w)
        win = slice(i * w, (i + 2) * w)
        if i == 0:
            no_prev = jnp.where(first, -jnp.inf, 0.0)
        for hk in range(AT_KV_HEADS):
            q2 = jnp.concatenate([qs_ref[2 * hk, rows, :], qs_ref[2 * hk + 1, rows, :]], axis=0)
            s = jnp.concatenate(
                [lax.dot_general(kf_ref[hk, 0, win, :], q2, nt, preferred_element_type=F32),
                 lax.dot_general(kf_ref[hk, 1, win, :], q2, nt, preferred_element_type=F32)], axis=1)
            s_prev = s[:w]
            if i == 0:
                s_prev = s_prev + no_prev
            logits = jnp.where(from_prev, s_prev, s[w:])
            sink = sink_ref[hk:hk + 1, :]
            m = jnp.maximum(jnp.max(logits, axis=0, keepdims=True), sink)
            p = jnp.exp(logits - m)
            denom = jnp.sum(p, axis=0, keepdims=True) + jnp.exp(sink - m)
            pn = p * (1.0 / denom)
            pp = jnp.concatenate([jnp.where(from_prev, pn, 0.0), jnp.where(from_prev, 0.0, pn)],
                                 axis=0).astype(BF16)
            r = jnp.dot(vt_ref[hk, :, win], pp, preferred_element_type=F32)
            for pair in range(grp // 2):
                out_t = jnp.concatenate([r[:, pair * w:(pair + 1) * w],
                                         r[:, (2 + pair) * w:(3 + pair) * w]], axis=0)
                sl = 2 * hk + pair
                att_ref[rows, sl * LANES:(sl + 1) * LANES] = out_t.T.astype(att_ref.dtype)

    kf_ref[:, :, 0:w, :] = kf_ref[:, :, tq:tq + w, :]
    vt_ref[:, :, 0:w] = vt_ref[:, :, tq:tq + w]
    o_ref[0] = x_ref[0] + jnp.dot(att_ref[...], wo_ref[...], preferred_element_type=F32)


def _slab_lane_dims():
    hh = AT_HEAD_DIM // 2
    lane = np.arange(LANES)
    return (lane // hh) % 2, lane % hh + hh * (lane // AT_HEAD_DIM)


def _attn_mixer(x, nm, w_qkv, positions, q_norm, k_norm, sinks, wo):
    bsz, s, d = x.shape
    tq = min(TS_MIX, s)
    hd = AT_HEAD_DIM
    qw, kw = AT_HEADS * hd, AT_KV_HEADS * hd
    grp = AT_HEADS // AT_KV_HEADS
    inv_freq = ROPE_THETA ** (-jnp.arange(hd // 2, dtype=F32) * (2.0 / hd))
    freq = jnp.tile(inv_freq, LANES // (hd // 2))[None, :]
    head_in_slab, dim = _slab_lane_dims()
    seg = jnp.asarray((head_in_slab[:, None] == head_in_slab[None, :]).astype(np.float32), dtype=BF16)
    n_slab = (qw + kw) // LANES
    cols = np.concatenate([sl * LANES + head_in_slab * hd + dim for sl in range(n_slab)]
                          + [np.arange(qw + kw, qw + 2 * kw)])
    w_qkv = w_qkv[:, cols]
    qn = q_norm.astype(F32)[dim][None, :]
    kn = k_norm.astype(F32)[dim][None, :]
    pos = positions.reshape(bsz, s // tq, tq // WINDOW, WINDOW)
    sk = sinks.astype(F32).reshape(AT_KV_HEADS, grp)[:, np.array([0, 2, 1, 3])]
    sk = jnp.repeat(sk, WINDOW, axis=1)
    consts_a = (nm, w_qkv)
    consts_b = (freq, qn, kn, seg, sk, wo)
    return pl.pallas_call(
        _attn_kernel,
        out_shape=jax.ShapeDtypeStruct((bsz, s, d), F32),
        grid=(bsz, s // tq),
        in_specs=([pl.BlockSpec((1, tq, d), lambda i, j: (i, j, 0))] + [_resident(c.shape) for c in consts_a]
                  + [pl.BlockSpec((1, 1, tq // WINDOW, WINDOW), lambda i, j: (i, j, 0, 0))]
                  + [_resident(c.shape) for c in consts_b]),
        out_specs=pl.BlockSpec((1, tq, d), lambda i, j: (i, j, 0)),
        scratch_shapes=[pltpu.VMEM((qw // LANES, tq, LANES), BF16),
                        pltpu.VMEM((AT_KV_HEADS, 2, tq + WINDOW, LANES), BF16),
                        pltpu.VMEM((AT_KV_HEADS, hd, tq + WINDOW), BF16),
                        pltpu.VMEM((tq, qw), BF16)],
        compiler_params=_params(2),
        name="attn_mixer",
    )(x, *consts_a, pos, *consts_b)


def _post_kernel(x_ref, p_ref, nf_ref, wu_ref, cw_ref, cb_ref, wd_ref, np_ref, wg_ref, wp_ref, o_ref,
                 gtail_ref, act_ref):
    tm = x_ref.shape[1]
    ff = wd_ref.shape[0]

    @pl.when(pl.program_id(1) == 0)
    def _():
        gtail_ref[...] = jnp.zeros_like(gtail_ref)

    x1 = x_ref[0]
    h = _rms(x1, nf_ref[...]).astype(BF16)
    for c in range(ff // FF_CHUNK):
        cs = slice(c * FF_CHUNK, (c + 1) * FF_CHUNK)
        g = jnp.dot(h, wu_ref[:, cs], preferred_element_type=F32)
        u = jnp.dot(h, wu_ref[:, ff + c * FF_CHUNK:ff + (c + 1) * FF_CHUNK], preferred_element_type=F32)
        gc = _causal_conv(g, gtail_ref[:, cs], cw_ref.at[:, cs], cb_ref.at[:, cs])
        gtail_ref[:, cs] = g[tm - SUBLANES:]
        act_ref[:, cs] = (jax.nn.gelu(gc, approximate=True) * u).astype(act_ref.dtype)
    x2 = x1 + jnp.dot(act_ref[...], wd_ref[...], preferred_element_type=F32)
    gate = jax.nn.sigmoid(jnp.dot(_rms(x2, np_ref[...]).astype(BF16), wg_ref[...], preferred_element_type=F32))
    pe = jnp.dot(p_ref[0, 0].astype(BF16), wp_ref[...], preferred_element_type=F32)
    o_ref[0] = x2 + gate * pe


def _post_mixer(x, p, layer, nf, wu, cw, cb, wd, npl, wg, wp):
    bsz, s, d = x.shape
    tm = min(TM_POST, s)
    ff = wd.shape[1]
    stacked = (nf, wu, cw, cb, wd, npl, wg, wp)
    return pl.pallas_call(
        _post_kernel,
        out_shape=jax.ShapeDtypeStruct((bsz, s, d), F32),
        grid=(bsz, s // tm),
        in_specs=[pl.BlockSpec((1, tm, d), lambda i, j: (i, j, 0)),
                  pl.BlockSpec((1, 1, tm, p.shape[-1]), lambda i, j: (layer, i, j, 0))]
                 + [_layer_block(a, layer) for a in stacked],
        out_specs=pl.BlockSpec((1, tm, d), lambda i, j: (i, j, 0)),
        scratch_shapes=[pltpu.VMEM((SUBLANES, ff), F32), pltpu.VMEM((tm, ff), BF16)],
        compiler_params=_params(2),
        name=f"post_mixer_{layer}",
    )(x, p, *stacked)


def _block_diag(w):
    n, r, _ = w.shape
    eye = jnp.eye(n, dtype=w.dtype)
    return (eye[:, None, :, None] * w[:, :, None, :]).reshape(n * r, n * r)


def kernel(x, p, positions, norm_mix, norm_ffn, norm_ple, hy_w_in, hy_b_in, rg_conv_w, rg_conv_b, rg_w_a, rg_b_a, rg_w_x, rg_b_x, rg_lambda, ml_conv_w, ml_conv_b, ml_norm, hy_w_out, at_w_qkv, at_q_norm, at_k_norm, at_sinks, at_w_out, ff_w_up, ff_conv_w, ff_conv_b, ff_w_down, ple_w_gate, ple_w_proj):
    depth = p.shape[0]
    row = lambda v: v.astype(F32)[None, :]
    n_main = 2 * RG_WIDTH + 4 * ML_WIDTH
    post = (norm_ffn.astype(F32)[:, None, :], ff_w_up.astype(BF16), ff_conv_w.astype(F32),
            ff_conv_b.astype(F32)[:, None, :], ff_w_down.astype(BF16), norm_ple.astype(F32)[:, None, :],
            ple_w_gate.astype(BF16), ple_w_proj.astype(BF16))
    for layer in range(depth):
        if layer % 2 == 0:
            e = layer // 2
            pad = LANES - 2 * ML_HEADS
            w_gt = jnp.pad(hy_w_in[e, :, n_main:], ((0, 0), (0, pad))).astype(BF16)
            b_gt = jnp.pad(hy_b_in[e, n_main:], (0, pad)).astype(F32)[None, :]
            x = _hybrid_mixer(x, row(norm_mix[layer]), hy_w_in[e, :, :n_main].astype(BF16),
                              row(hy_b_in[e, :n_main]), w_gt, b_gt,
                              rg_conv_w[e], row(rg_conv_b[e]), _block_diag(rg_w_a[e]).astype(BF16), row(rg_b_a[e]),
                              _block_diag(rg_w_x[e]).astype(BF16), row(rg_b_x[e]), row(rg_lambda[e]),
                              ml_conv_w[e], row(ml_conv_b[e]), row(ml_norm[e]), hy_w_out[e].astype(BF16))
        else:
            o = layer // 2
            x = _attn_mixer(x, row(norm_mix[layer]), at_w_qkv[o].astype(BF16), positions, at_q_norm[o],
                            at_k_norm[o], at_sinks[o], at_w_out[o].astype(BF16))
        x = _post_mixer(x, p, layer, *post)
    return x
```

```python
import jax
import jax.numpy as jnp
import numpy as np
from jax import lax
from jax.experimental import pallas as pl
from jax.experimental.pallas import tpu as pltpu

F32 = jnp.float32
BF16 = jnp.bfloat16

D_MODEL = 1024
PLE_DIM = 256
RG_WIDTH = 512
RG_BLOCKS = 8
RG_C = 8.0
ML_HEADS = 4
ML_HEAD_DIM = 128
ML_WIDTH = ML_HEADS * ML_HEAD_DIM
ML_CHUNK = 128
AT_HEADS = 16
AT_KV_HEADS = 4
AT_HEAD_DIM = 64
WINDOW = 128
ROPE_THETA = 10000.0
FF_DIM = 3 * D_MODEL
EPS = 1e-6

LANES = 128
SUBLANES = 8
VMEM_LIMIT_BYTES = 56 * 1024 * 1024

TS_MIX = 512
TM_POST = 512
FF_CHUNK = 512
PROJ_CHUNK = 512


def _params(n_axes):
    return pltpu.CompilerParams(dimension_semantics=("arbitrary",) * n_axes,
                                vmem_limit_bytes=VMEM_LIMIT_BYTES)


def _resident(shape, index=None):
    n = len(shape)
    idx = (0,) * n if index is None else tuple(index)
    return pl.BlockSpec(shape, lambda *_: idx, pipeline_mode=pl.Buffered(1))


def _layer_block(arr, layer):
    shape = (None,) + arr.shape[1:]
    return _resident(shape, (layer,) + (0,) * (arr.ndim - 1))


def _rms(x, g):
    return x * lax.rsqrt(jnp.mean(x * x, axis=-1, keepdims=True) + EPS) * g


def _shift_rows(x, tail, d):
    rolled = pltpu.roll(x, d, 0)
    tail_rolled = pltpu.roll(tail, d, 0)
    row = lax.broadcasted_iota(jnp.int32, tail.shape, 0)
    head = jnp.where(row < d, tail_rolled, rolled[:SUBLANES])
    return jnp.concatenate([head, rolled[SUBLANES:]], axis=0)


def _causal_conv(x, tail, w_ref, b_ref):
    k_w = w_ref.shape[0]
    y = x * w_ref[k_w - 1:k_w, :] + b_ref[...]
    for j in range(k_w - 1):
        y = y + _shift_rows(x, tail, k_w - 1 - j) * w_ref[j:j + 1, :]
    return y


def _log_sigmoid(x):
    return jnp.minimum(x, 0.0) - jnp.log1p(jnp.exp(-jnp.abs(x)))


def _rg_gates(z_ref, cw_ref, cb_ref, wa_ref, ba_ref, wx_ref, bx_ref, lam_ref, tail_ref, a_ref, u_ref):
    ts = z_ref.shape[0]
    x = z_ref[:, 0:RG_WIDTH]
    xc = _causal_conv(x, tail_ref[...], cw_ref, cb_ref)
    tail_ref[...] = x[ts - SUBLANES:]
    xb = xc.astype(BF16)
    r = jax.nn.sigmoid(jnp.dot(xb, wa_ref[...], preferred_element_type=F32) + ba_ref[...])
    i = jax.nn.sigmoid(jnp.dot(xb, wx_ref[...], preferred_element_type=F32) + bx_ref[...])
    lam = lam_ref[...]
    softplus_neg_lam = jnp.maximum(-lam, 0.0) + jnp.log1p(jnp.exp(-jnp.abs(lam)))
    log_a = (-RG_C) * r * softplus_neg_lam
    a = jnp.exp(log_a)
    a_ref[...] = a
    u_ref[...] = jnp.sqrt(1.0 - a * a) * (i * xc)


def _rg_scan(z_ref, y_ref, hc_ref, a_ref, u_ref):
    ts = z_ref.shape[0]
    row = lax.broadcasted_iota(jnp.int32, (SUBLANES, RG_WIDTH), 0)
    hc = hc_ref[...]
    for k in range(ts // SUBLANES):
        rows = slice(k * SUBLANES, (k + 1) * SUBLANES)
        a = a_ref[rows, :]
        u = u_ref[rows, :]
        for d in (1, 2, 4):
            keep = row >= d
            u = u + a * jnp.where(keep, pltpu.roll(u, d, 0), 0.0)
            a = a * jnp.where(keep, pltpu.roll(a, d, 0), 1.0)
        h = u + a * hc
        u_ref[rows, :] = h
        hc = jnp.broadcast_to(h[SUBLANES - 1:SUBLANES, :], (SUBLANES, RG_WIDTH))
    hc_ref[...] = hc
    gate = jax.nn.gelu(z_ref[:, RG_WIDTH:2 * RG_WIDTH], approximate=True)
    y_ref[:, 0:RG_WIDTH] = (u_ref[...] * gate).astype(y_ref.dtype)


def _lane_col(x, h):
    return jnp.broadcast_to(x[:, h:h + 1], x.shape)


def _ml_chunk(rows, z_ref, gt_ref, cw_ref, cb_ref, nw_ref, y_ref, qtail_ref, ktail_ref, c_ref, n_ref, m_ref):
    lc = ML_CHUNK
    dh = ML_HEAD_DIM
    q0 = 2 * RG_WIDTH
    k0, v0, o0 = q0 + ML_WIDTH, q0 + 2 * ML_WIDTH, q0 + 3 * ML_WIDTH
    row = lax.broadcasted_iota(jnp.int32, (lc, LANES), 0)
    col = lax.broadcasted_iota(jnp.int32, (lc, LANES), 1)
    causal = row >= col

    q_raw = z_ref[rows, q0:k0]
    k_raw = z_ref[rows, k0:v0]
    q = jax.nn.silu(_causal_conv(q_raw, qtail_ref[...], cw_ref.at[:, :ML_WIDTH], cb_ref.at[:, :ML_WIDTH]))
    k = jax.nn.silu(_causal_conv(k_raw, ktail_ref[...], cw_ref.at[:, ML_WIDTH:], cb_ref.at[:, ML_WIDTH:]))
    qtail_ref[...] = q_raw[lc - SUBLANES:]
    ktail_ref[...] = k_raw[lc - SUBLANES:]
    k = k * (dh ** -0.5)

    gates = gt_ref[rows, :]
    b = _log_sigmoid(gates)
    d = 1
    while d < lc:
        b = b + jnp.where(row >= d, pltpu.roll(b, d, 0), 0.0)
        d *= 2
    b = pltpu.roll(b, LANES - ML_HEADS, 1)
    r = gates - b
    b_last = b[lc - 1:lc, :]
    g_loc = b_last + r
    m_loc = jnp.max(g_loc, axis=0, keepdims=True)
    w_loc = jnp.exp(g_loc - m_loc)
    m_st = m_ref[0:1, :]
    m_new = jnp.maximum(b_last + m_st, m_loc)
    a_prev = jnp.exp(b_last + m_st - m_new)
    a_loc = jnp.exp(m_loc - m_new)
    rmax = r
    d = 1
    while d < lc:
        rmax = jnp.maximum(rmax, jnp.where(row >= d, pltpu.roll(rmax, d, 0), -jnp.inf))
        d *= 2
    e = -jnp.maximum(m_st, rmax)
    a_inter = jnp.exp(m_st + e)
    exp_neg_mt = jnp.exp(e - b)
    r_t = r.T

    for h in range(ML_HEADS):
        hs = slice(h * dh, (h + 1) * dh)
        qh = q[:, hs]
        kh = k[:, hs]
        vh = z_ref[rows, v0 + h * dh:v0 + (h + 1) * dh]
        qb = qh.astype(BF16)
        kb = kh.astype(BF16)
        vb = vh.astype(BF16)
        c_prev = c_ref[h]
        n_prev = n_ref[h:h + 1, :]
        w_intra = jnp.where(causal, jnp.exp(_lane_col(e, h) + r_t[h:h + 1, :]), 0.0)
        s_qk = lax.dot_general(qb, kb, (((1,), (1,)), ((), ())), preferred_element_type=F32) * w_intra
        a_in = _lane_col(a_inter, h)
        num = (jnp.dot(s_qk.astype(BF16), vb, preferred_element_type=F32)
               + a_in * jnp.dot(qb, c_prev.astype(BF16), preferred_element_type=F32))
        den = jnp.sum(s_qk + a_in * (qh * n_prev), axis=-1, keepdims=True)
        hm = num / jnp.maximum(jnp.abs(den), exp_neg_mt[:, h:h + 1])
        hm = hm * lax.rsqrt(jnp.mean(hm * hm, axis=-1, keepdims=True) + EPS) * nw_ref[:, hs]
        y = jax.nn.sigmoid(z_ref[rows, o0 + h * dh:o0 + (h + 1) * dh]) * hm
        y_ref[rows, RG_WIDTH + h * dh:RG_WIDTH + (h + 1) * dh] = y.astype(y_ref.dtype)

        kw = kh * _lane_col(w_loc, h)
        c_loc = jnp.dot(kw.T.astype(BF16), vb, preferred_element_type=F32)
        n_loc = jnp.sum(kw, axis=0, keepdims=True)
        ap = a_prev[:, h:h + 1]
        al = a_loc[:, h:h + 1]
        c_ref[h] = ap * c_prev + al * c_loc
        n_ref[h:h + 1, :] = ap * n_prev + al * n_loc
    m_ref[...] = jnp.broadcast_to(m_new, m_ref.shape)


def _hybrid_kernel(x_ref, nm_ref, win_ref, bin_ref, wgt_ref, bgt_ref,
                   rcw_ref, rcb_ref, wa_ref, ba_ref, wx_ref, bx_ref, lam_ref,
                   mcw_ref, mcb_ref, mnw_ref, wo_ref, o_ref,
                   z_ref, gt_ref, y_ref, rtail_ref, hc_ref, a_ref, u_ref,
                   qtail_ref, ktail_ref, c_ref, n_ref, m_ref):
    @pl.when(pl.program_id(1) == 0)
    def _():
        for ref in (rtail_ref, hc_ref, qtail_ref, ktail_ref, c_ref, n_ref):
            ref[...] = jnp.zeros_like(ref)
        m_ref[...] = jnp.full_like(m_ref, -jnp.inf)

    h = _rms(x_ref[0], nm_ref[...]).astype(BF16)
    n_main = win_ref.shape[1]
    n_rg = 2 * RG_WIDTH

    def project(lo, hi):
        for c in range(lo // PROJ_CHUNK, hi // PROJ_CHUNK):
            cs = slice(c * PROJ_CHUNK, (c + 1) * PROJ_CHUNK)
            z_ref[:, cs] = jnp.dot(h, win_ref[:, cs], preferred_element_type=F32) + bin_ref[:, cs]

    project(0, n_rg)
    _rg_gates(z_ref, rcw_ref, rcb_ref, wa_ref, ba_ref, wx_ref, bx_ref, lam_ref, rtail_ref, a_ref, u_ref)
    project(n_rg, n_main)
    gt_ref[...] = jnp.dot(h, wgt_ref[...], preferred_element_type=F32) + bgt_ref[...]
    _rg_scan(z_ref, y_ref, hc_ref, a_ref, u_ref)
    for c in range(x_ref.shape[1] // ML_CHUNK):
        rows = slice(c * ML_CHUNK, (c + 1) * ML_CHUNK)
        _ml_chunk(rows, z_ref, gt_ref, mcw_ref, mcb_ref, mnw_ref, y_ref, qtail_ref, ktail_ref, c_ref, n_ref, m_ref)
        o_ref[0, rows, :] = x_ref[0, rows, :] + jnp.dot(y_ref[rows, :], wo_ref[...], preferred_element_type=F32)


def _hybrid_mixer(x, nm, w_in, b_in, w_gt, b_gt, rcw, rcb, wa, ba, wx, bx, lam, mcw, mcb, mnw, wo):
    bsz, s, d = x.shape
    ts = min(TS_MIX, s)
    n_main = w_in.shape[1]
    rw, mw = RG_WIDTH, ML_WIDTH
    consts = (nm, w_in, b_in, w_gt, b_gt, rcw, rcb, wa, ba, wx, bx, lam, mcw, mcb, mnw, wo)
    return pl.pallas_call(
        _hybrid_kernel,
        out_shape=jax.ShapeDtypeStruct((bsz, s, d), F32),
        grid=(bsz, s // ts),
        in_specs=[pl.BlockSpec((1, ts, d), lambda i, j: (i, j, 0))] + [_resident(c.shape) for c in consts],
        out_specs=pl.BlockSpec((1, ts, d), lambda i, j: (i, j, 0)),
        scratch_shapes=[pltpu.VMEM((ts, n_main), F32), pltpu.VMEM((ts, LANES), F32),
                        pltpu.VMEM((ts, rw + mw), BF16),
                        pltpu.VMEM((SUBLANES, rw), F32), pltpu.VMEM((SUBLANES, rw), F32),
                        pltpu.VMEM((ts, rw), F32), pltpu.VMEM((ts, rw), F32),
                        pltpu.VMEM((SUBLANES, mw), F32), pltpu.VMEM((SUBLANES, mw), F32),
                        pltpu.VMEM((ML_HEADS, ML_HEAD_DIM, ML_HEAD_DIM), F32),
                        pltpu.VMEM((SUBLANES, ML_HEAD_DIM), F32),
                        pltpu.VMEM((SUBLANES, LANES), F32)],
        compiler_params=_params(2),
        name="hybrid_mixer",
    )(x, *consts)


def _attn_kernel(x_ref, nm_ref, wqkv_ref, pos_ref, freq_ref, qn_ref, kn_ref, seg_ref, sink_ref, wo_ref,
                 o_ref, qs_ref, kf_ref, vt_ref, att_ref):
    tq = x_ref.shape[1]
    w = WINDOW
    hd = AT_HEAD_DIM
    hh = hd // 2
    n_sub = tq // w
    qw = AT_HEADS * hd
    kw = AT_KV_HEADS * hd
    n_qslab = qw // LANES
    n_kslab = kw // LANES
    n_pack = LANES // hh
    rp = w // n_pack
    first = pl.program_id(1) == 0

    @pl.when(first)
    def _():
        kf_ref[:, :, 0:w, :] = jnp.zeros((AT_KV_HEADS, 2, w, LANES), kf_ref.dtype)
        vt_ref[:, :, 0:w] = jnp.zeros((AT_KV_HEADS, hd, w), vt_ref.dtype)

    lane = lax.broadcasted_iota(jnp.int32, (w, LANES), 1)
    slot_a = (lane // hh) % 2 == 0
    sin_sign = jnp.where(lane < hd, -1.0, 1.0)
    lane_grp = lax.broadcasted_iota(jnp.int32, (rp, LANES), 1) // hh
    seg = seg_ref[...]

    qkv_all = jnp.dot(_rms(x_ref[0], nm_ref[...]).astype(BF16), wqkv_ref[...], preferred_element_type=F32)

    def norm_rope(t, gain, cos, sin):
        ms = jnp.dot((t * t).astype(BF16), seg, preferred_element_type=F32) * (1.0 / hd)
        tn = t * lax.rsqrt(ms + EPS) * gain
        return tn * cos + pltpu.roll(tn, hd, 1) * sin

    def spread(t):
        parts = []
        for g in range(n_pack):
            z = jnp.where(lane_grp == g, t, 0.0)
            y = z
            for k in range(1, n_pack):
                y = y + pltpu.roll(z, k * hh, 1)
            parts.append(y)
        return jnp.concatenate(parts, axis=0)

    for i in range(n_sub):
        rows = slice(i * w, (i + 1) * w)
        krows = slice(w + i * w, w + (i + 1) * w)
        qkv = qkv_all[rows]
        pos = pos_ref[0, 0, i:i + 1, :].astype(F32)
        pos_col = jnp.broadcast_to(pos, (w, LANES)).T
        packed = pos_col[0:rp]
        for g in range(1, n_pack):
            packed = jnp.where(lane_grp == g, pos_col[g * rp:(g + 1) * rp], packed)
        ang = packed * freq_ref[...]
        cos = spread(jnp.cos(ang))
        sin = spread(jnp.sin(ang)) * sin_sign
        for sl in range(n_qslab):
            xq = norm_rope(qkv[:, sl * LANES:(sl + 1) * LANES], qn_ref[...], cos, sin) * (hd ** -0.5)
            qs_ref[sl, rows, :] = xq.astype(qs_ref.dtype)
        for sl in range(n_kslab):
            xk = norm_rope(qkv[:, qw + sl * LANES:qw + (sl + 1) * LANES], kn_ref[...], cos, sin)
            kf_ref[2 * sl, 0, krows, :] = jnp.where(slot_a, xk, 0.0).astype(kf_ref.dtype)
            kf_ref[2 * sl, 1, krows, :] = jnp.where(slot_a, 0.0, pltpu.roll(xk, hh, 1)).astype(kf_ref.dtype)
            kf_ref[2 * sl + 1, 0, krows, :] = jnp.where(slot_a, pltpu.roll(xk, LANES - hh, 1), 0.0).astype(kf_ref.dtype)
            kf_ref[2 * sl + 1, 1, krows, :] = jnp.where(slot_a, 0.0, xk).astype(kf_ref.dtype)
            v0 = qw + kw + sl * LANES
            vt = qkv[:, v0:v0 + LANES].T.astype(vt_ref.dtype)
            vt_ref[2 * sl, :, krows] = vt[:hd]
            vt_ref[2 * sl + 1, :, krows] = vt[hd:]

    grp = AT_HEADS // AT_KV_HEADS
    row_a = lax.broadcasted_iota(jnp.int32, (w, AT_HEADS * w), 0)
    lane_a = lax.broadcasted_iota(jnp.int32, (w, AT_HEADS * w), 1)
    from_prev = row_a > (lane_a % w)
    nt = (((1,), (1,)), ((), ()))
    for i in range(n_sub):
        rows = slice(i * w, (i + 1) * w)
        win = slice(i * w, (i + 2) * w)
        parts = []
        for hk in range(AT_KV_HEADS):
            q2 = jnp.concatenate([qs_ref[2 * hk, rows, :], qs_ref[2 * hk + 1, rows, :]], axis=0)
            parts.append(lax.dot_general(kf_ref[hk, 0, win, :], q2, nt, preferred_element_type=F32))
            parts.append(lax.dot_general(kf_ref[hk, 1, win, :], q2, nt, preferred_element_type=F32))
        s = jnp.concatenate(parts, axis=1)
        s_prev = s[:w]
        if i == 0:
            s_prev = s_prev + jnp.where(first, -jnp.inf, 0.0)
        logits = jnp.where(from_prev, s_prev, s[w:])
        sink = sink_ref[...]
        m = jnp.maximum(jnp.max(logits, axis=0, keepdims=True), sink)
        p = jnp.exp(logits - m)
        denom = jnp.sum(p, axis=0, keepdims=True) + jnp.exp(sink - m)
        pn = p * (1.0 / denom)
        pp = jnp.concatenate([jnp.where(from_prev, pn, 0.0), jnp.where(from_prev, 0.0, pn)],
                             axis=0).astype(BF16)
        for hk in range(AT_KV_HEADS):
            r = jnp.dot(vt_ref[hk, :, win], pp[:, hk * grp * w:(hk + 1) * grp * w],
                        preferred_element_type=F32)
            for pair in range(grp // 2):
                out_t = jnp.concatenate([r[:, pair * w:(pair + 1) * w],
                                         r[:, (2 + pair) * w:(3 + pair) * w]], axis=0)
                sl = 2 * hk + pair
                att_ref[rows, sl * LANES:(sl + 1) * LANES] = out_t.T.astype(att_ref.dtype)

    kf_ref[:, :, 0:w, :] = kf_ref[:, :, tq:tq + w, :]
    vt_ref[:, :, 0:w] = vt_ref[:, :, tq:tq + w]
    o_ref[0] = x_ref[0] + jnp.dot(att_ref[...], wo_ref[...], preferred_element_type=F32)


def _slab_lane_dims():
    hh = AT_HEAD_DIM // 2
    lane = np.arange(LANES)
    return (lane // hh) % 2, lane % hh + hh * (lane // AT_HEAD_DIM)


def _attn_mixer(x, nm, w_qkv, positions, q_norm, k_norm, sinks, wo):
    bsz, s, d = x.shape
    tq = min(TS_MIX, s)
    hd = AT_HEAD_DIM
    qw, kw = AT_HEADS * hd, AT_KV_HEADS * hd
    grp = AT_HEADS // AT_KV_HEADS
    inv_freq = ROPE_THETA ** (-jnp.arange(hd // 2, dtype=F32) * (2.0 / hd))
    freq = jnp.tile(inv_freq, LANES // (hd // 2))[None, :]
    head_in_slab, dim = _slab_lane_dims()
    seg = jnp.asarray((head_in_slab[:, None] == head_in_slab[None, :]).astype(np.float32), dtype=BF16)
    n_slab = (qw + kw) // LANES
    cols = np.concatenate([sl * LANES + head_in_slab * hd + dim for sl in range(n_slab)]
                          + [np.arange(qw + kw, qw + 2 * kw)])
    w_qkv = w_qkv[:, cols]
    qn = q_norm.astype(F32)[dim][None, :]
    kn = k_norm.astype(F32)[dim][None, :]
    pos = positions.reshape(bsz, s // tq, tq // WINDOW, WINDOW)
    sk = sinks.astype(F32).reshape(AT_KV_HEADS, grp)[:, np.array([0, 2, 1, 3])]
    sk = jnp.repeat(sk, WINDOW, axis=1).reshape(1, AT_HEADS * WINDOW)
    consts_a = (nm, w_qkv)
    consts_b = (freq, qn, kn, seg, sk, wo)
    return pl.pallas_call(
        _attn_kernel,
        out_shape=jax.ShapeDtypeStruct((bsz, s, d), F32),
        grid=(bsz, s // tq),
        in_specs=([pl.BlockSpec((1, tq, d), lambda i, j: (i, j, 0))] + [_resident(c.shape) for c in consts_a]
                  + [pl.BlockSpec((1, 1, tq // WINDOW, WINDOW), lambda i, j: (i, j, 0, 0))]
                  + [_resident(c.shape) for c in consts_b]),
        out_specs=pl.BlockSpec((1, tq, d), lambda i, j: (i, j, 0)),
        scratch_shapes=[pltpu.VMEM((qw // LANES, tq, LANES), BF16),
                        pltpu.VMEM((AT_KV_HEADS, 2, tq + WINDOW, LANES), BF16),
                        pltpu.VMEM((AT_KV_HEADS, hd, tq + WINDOW), BF16),
                        pltpu.VMEM((tq, qw), BF16)],
        compiler_params=_params(2),
        name="attn_mixer",
    )(x, *consts_a, pos, *consts_b)


def _post_kernel(x_ref, p_ref, nf_ref, wu_ref, cw_ref, cb_ref, wd_ref, np_ref, wg_ref, wp_ref, o_ref,
                 gtail_ref, act_ref):
    tm = x_ref.shape[1]
    ff = wd_ref.shape[0]

    @pl.when(pl.program_id(1) == 0)
    def _():
        gtail_ref[...] = jnp.zeros_like(gtail_ref)

    x1 = x_ref[0]
    h = _rms(x1, nf_ref[...]).astype(BF16)
    for c in range(ff // FF_CHUNK):
        cs = slice(c * FF_CHUNK, (c + 1) * FF_CHUNK)
        g = jnp.dot(h, wu_ref[:, cs], preferred_element_type=F32)
        u = jnp.dot(h, wu_ref[:, ff + c * FF_CHUNK:ff + (c + 1) * FF_CHUNK], preferred_element_type=F32)
        gc = _causal_conv(g, gtail_ref[:, cs], cw_ref.at[:, cs], cb_ref.at[:, cs])
        gtail_ref[:, cs] = g[tm - SUBLANES:]
        act_ref[:, cs] = (jax.nn.gelu(gc, approximate=True) * u).astype(act_ref.dtype)
    x2 = x1 + jnp.dot(act_ref[...], wd_ref[...], preferred_element_type=F32)
    gate = jax.nn.sigmoid(jnp.dot(_rms(x2, np_ref[...]).astype(BF16), wg_ref[...], preferred_element_type=F32))
    pe = jnp.dot(p_ref[0, 0].astype(BF16), wp_ref[...], preferred_element_type=F32)
    o_ref[0] = x2 + gate * pe


def _post_mixer(x, p, layer, nf, wu, cw, cb, wd, npl, wg, wp):
    bsz, s, d = x.shape
    tm = min(TM_POST, s)
    ff = wd.shape[1]
    stacked = (nf, wu, cw, cb, wd, npl, wg, wp)
    return pl.pallas_call(
        _post_kernel,
        out_shape=jax.ShapeDtypeStruct((bsz, s, d), F32),
        grid=(bsz, s // tm),
        in_specs=[pl.BlockSpec((1, tm, d), lambda i, j: (i, j, 0)),
                  pl.BlockSpec((1, 1, tm, p.shape[-1]), lambda i, j: (layer, i, j, 0))]
                 + [_layer_block(a, layer) for a in stacked],
        out_specs=pl.BlockSpec((1, tm, d), lambda i, j: (i, j, 0)),
        scratch_shapes=[pltpu.VMEM((SUBLANES, ff), F32), pltpu.VMEM((tm, ff), BF16)],
        compiler_params=_params(2),
        name=f"post_mixer_{layer}",
    )(x, p, *stacked)


def _block_diag(w):
    n, r, _ = w.shape
    eye = jnp.eye(n, dtype=w.dtype)
    return (eye[:, None, :, None] * w[:, :, None, :]).reshape(n * r, n * r)


def kernel(x, p, positions, norm_mix, norm_ffn, norm_ple, hy_w_in, hy_b_in, rg_conv_w, rg_conv_b, rg_w_a, rg_b_a, rg_w_x, rg_b_x, rg_lambda, ml_conv_w, ml_conv_b, ml_norm, hy_w_out, at_w_qkv, at_q_norm, at_k_norm, at_sinks, at_w_out, ff_w_up, ff_conv_w, ff_conv_b, ff_w_down, ple_w_gate, ple_w_proj):
    depth = p.shape[0]
    row = lambda v: v.astype(F32)[None, :]
    n_main = 2 * RG_WIDTH + 4 * ML_WIDTH
    post = (norm_ffn.astype(F32)[:, None, :], ff_w_up.astype(BF16), ff_conv_w.astype(F32),
            ff_conv_b.astype(F32)[:, None, :], ff_w_down.astype(BF16), norm_ple.astype(F32)[:, None, :],
            ple_w_gate.astype(BF16), ple_w_proj.astype(BF16))
    for layer in range(depth):
        if layer % 2 == 0:
            e = layer // 2
            pad = LANES - 2 * ML_HEADS
            w_gt = jnp.pad(hy_w_in[e, :, n_main:], ((0, 0), (0, pad))).astype(BF16)
            b_gt = jnp.pad(hy_b_in[e, n_main:], (0, pad)).astype(F32)[None, :]
            x = _hybrid_mixer(x, row(norm_mix[layer]), hy_w_in[e, :, :n_main].astype(BF16),
                              row(hy_b_in[e, :n_main]), w_gt, b_gt,
                              rg_conv_w[e], row(rg_conv_b[e]), _block_diag(rg_w_a[e]).astype(BF16), row(rg_b_a[e]),
                              _block_diag(rg_w_x[e]).astype(BF16), row(rg_b_x[e]), row(rg_lambda[e]),
                              ml_conv_w[e], row(ml_conv_b[e]), row(ml_norm[e]), hy_w_out[e].astype(BF16))
        else:
            o = layer // 2
            x = _attn_mixer(x, row(norm_mix[layer]), at_w_qkv[o].astype(BF16), positions, at_q_norm[o],
                            at_k_norm[o], at_sinks[o], at_w_out[o].astype(BF16))
        x = _post_mixer(x, p, layer, *post)
    return x
```

```python
import jax
import jax.numpy as jnp
import numpy as np
from jax import lax
from jax.experimental import pallas as pl
from jax.experimental.pallas import tpu as pltpu

F32 = jnp.float32
BF16 = jnp.bfloat16

D_MODEL = 1024
PLE_DIM = 256
RG_WIDTH = 512
RG_BLOCKS = 8
RG_C = 8.0
ML_HEADS = 4
ML_HEAD_DIM = 128
ML_WIDTH = ML_HEADS * ML_HEAD_DIM
ML_CHUNK = 128
AT_HEADS = 16
AT_KV_HEADS = 4
AT_HEAD_DIM = 64
WINDOW = 128
ROPE_THETA = 10000.0
FF_DIM = 3 * D_MODEL
EPS = 1e-6

LANES = 128
SUBLANES = 8
VMEM_LIMIT_BYTES = 56 * 1024 * 1024

TS_MIX = 512
TM_POST = 512
FF_CHUNK = 512
PROJ_CHUNK = 512


def _params(n_axes):
    return pltpu.CompilerParams(dimension_semantics=("arbitrary",) * n_axes,
                                vmem_limit_bytes=VMEM_LIMIT_BYTES)


def _resident(shape, index=None):
    n = len(shape)
    idx = (0,) * n if index is None else tuple(index)
    return pl.BlockSpec(shape, lambda *_: idx, pipeline_mode=pl.Buffered(1))


def _layer_block(arr, layer):
    shape = (None,) + arr.shape[1:]
    return _resident(shape, (layer,) + (0,) * (arr.ndim - 1))


def _rms(x, g):
    return x * lax.rsqrt(jnp.mean(x * x, axis=-1, keepdims=True) + EPS) * g


def _shift_rows(x, tail, d):
    rolled = pltpu.roll(x, d, 0)
    tail_rolled = pltpu.roll(tail, d, 0)
    row = lax.broadcasted_iota(jnp.int32, tail.shape, 0)
    head = jnp.where(row < d, tail_rolled, rolled[:SUBLANES])
    return jnp.concatenate([head, rolled[SUBLANES:]], axis=0)


def _causal_conv(x, tail, w_ref, b_ref):
    k_w = w_ref.shape[0]
    y = x * w_ref[k_w - 1:k_w, :] + b_ref[...]
    for j in range(k_w - 1):
        y = y + _shift_rows(x, tail, k_w - 1 - j) * w_ref[j:j + 1, :]
    return y


def _log_sigmoid(x):
    return jnp.minimum(x, 0.0) - jnp.log1p(jnp.exp(-jnp.abs(x)))


def _rg_gates(z_ref, cw_ref, cb_ref, wa_ref, ba_ref, wx_ref, bx_ref, lam_ref, tail_ref, a_ref, u_ref):
    ts = z_ref.shape[0]
    x = z_ref[:, 0:RG_WIDTH]
    xc = _causal_conv(x, tail_ref[...], cw_ref, cb_ref)
    tail_ref[...] = x[ts - SUBLANES:]
    xb = xc.astype(BF16)
    r = jax.nn.sigmoid(jnp.dot(xb, wa_ref[...], preferred_element_type=F32) + ba_ref[...])
    i = jax.nn.sigmoid(jnp.dot(xb, wx_ref[...], preferred_element_type=F32) + bx_ref[...])
    lam = lam_ref[...]
    softplus_neg_lam = jnp.maximum(-lam, 0.0) + jnp.log1p(jnp.exp(-jnp.abs(lam)))
    log_a = (-RG_C) * r * softplus_neg_lam
    a = jnp.exp(log_a)
    a_ref[...] = a
    v = 1.0 - a * a
    u_ref[...] = jnp.where(v > 0.0, v * lax.rsqrt(v), 0.0) * (i * xc)


def _rg_scan(z_ref, y_ref, hc_ref, a_ref, u_ref):
    ts = z_ref.shape[0]
    row = lax.broadcasted_iota(jnp.int32, (SUBLANES, RG_WIDTH), 0)
    hc = hc_ref[...]
    for k in range(ts // SUBLANES):
        rows = slice(k * SUBLANES, (k + 1) * SUBLANES)
        a = a_ref[rows, :]
        u = u_ref[rows, :]
        for d in (1, 2, 4):
            keep = row >= d
            u = u + a * jnp.where(keep, pltpu.roll(u, d, 0), 0.0)
            a = a * jnp.where(keep, pltpu.roll(a, d, 0), 1.0)
        h = u + a * hc
        u_ref[rows, :] = h
        hc = jnp.broadcast_to(h[SUBLANES - 1:SUBLANES, :], (SUBLANES, RG_WIDTH))
    hc_ref[...] = hc
    gate = jax.nn.gelu(z_ref[:, RG_WIDTH:2 * RG_WIDTH], approximate=True)
    y_ref[:, 0:RG_WIDTH] = (u_ref[...] * gate).astype(y_ref.dtype)


def _lane_col(x, h):
    return jnp.broadcast_to(x[:, h:h + 1], x.shape)


def _ml_chunk(rows, z_ref, gt_ref, cw_ref, cb_ref, nw_ref, y_ref, qtail_ref, ktail_ref, c_ref, n_ref, m_ref):
    lc = ML_CHUNK
    dh = ML_HEAD_DIM
    q0 = 2 * RG_WIDTH
    k0, v0, o0 = q0 + ML_WIDTH, q0 + 2 * ML_WIDTH, q0 + 3 * ML_WIDTH
    row = lax.broadcasted_iota(jnp.int32, (lc, LANES), 0)

    q_raw = z_ref[rows, q0:k0]
    k_raw = z_ref[rows, k0:v0]
    q = jax.nn.silu(_causal_conv(q_raw, qtail_ref[...], cw_ref.at[:, :ML_WIDTH], cb_ref.at[:, :ML_WIDTH]))
    k = jax.nn.silu(_causal_conv(k_raw, ktail_ref[...], cw_ref.at[:, ML_WIDTH:], cb_ref.at[:, ML_WIDTH:]))
    qtail_ref[...] = q_raw[lc - SUBLANES:]
    ktail_ref[...] = k_raw[lc - SUBLANES:]
    k = k * (dh ** -0.5)

    gates = gt_ref[rows, :]
    b = _log_sigmoid(gates)
    d = 1
    while d < lc:
        b = b + jnp.where(row >= d, pltpu.roll(b, d, 0), 0.0)
        d *= 2
    b = pltpu.roll(b, LANES - ML_HEADS, 1)
    r = gates - b
    b_last = b[lc - 1:lc, :]
    g_loc = b_last + r
    m_loc = jnp.max(g_loc, axis=0, keepdims=True)
    w_loc = jnp.exp(g_loc - m_loc)
    m_st = m_ref[0:1, :]
    m_new = jnp.maximum(b_last + m_st, m_loc)
    a_prev = jnp.exp(b_last + m_st - m_new)
    a_loc = jnp.exp(m_loc - m_new)
    rmax = r
    d = 1
    while d < lc:
        rmax = jnp.maximum(rmax, jnp.where(row >= d, pltpu.roll(rmax, d, 0), -jnp.inf))
        d *= 2
    e = -jnp.maximum(m_st, rmax)
    a_inter = jnp.exp(m_st + e)
    exp_neg_mt = jnp.exp(e - b)
    r_t = r.T

    heads = [slice(h * dh, (h + 1) * dh) for h in range(ML_HEADS)]
    wide = lambda f: jnp.concatenate([f(h) for h in range(ML_HEADS)], axis=1)
    nt = (((1,), (1,)), ((), ()))
    row_w = lax.broadcasted_iota(jnp.int32, (lc, ML_WIDTH), 0)
    col_w = lax.broadcasted_iota(jnp.int32, (lc, ML_WIDTH), 1)
    causal = row_w >= (col_w % dh)
    qb = q.astype(BF16)
    kb = k.astype(BF16)
    vb = z_ref[rows, v0:o0].astype(BF16)
    c_prev = [c_ref[h] for h in range(ML_HEADS)]
    n_prev = wide(lambda h: n_ref[h:h + 1, :])
    s = wide(lambda h: lax.dot_general(qb[:, heads[h]], kb[:, heads[h]], nt, preferred_element_type=F32))
    w_intra = jnp.where(causal, jnp.exp(wide(lambda h: _lane_col(e, h)) + wide(lambda h: r_t[h:h + 1, :])), 0.0)
    s_qk = s * w_intra
    sb = s_qk.astype(BF16)
    a_in = wide(lambda h: _lane_col(a_inter, h))
    num = (wide(lambda h: jnp.dot(sb[:, heads[h]], vb[:, heads[h]], preferred_element_type=F32))
           + a_in * wide(lambda h: jnp.dot(qb[:, heads[h]], c_prev[h].astype(BF16), preferred_element_type=F32)))
    t = s_qk + a_in * (q * n_prev)
    den = wide(lambda h: jnp.broadcast_to(jnp.sum(t[:, heads[h]], axis=-1, keepdims=True), (lc, dh)))
    hm = num / jnp.maximum(jnp.abs(den), wide(lambda h: _lane_col(exp_neg_mt, h)))
    ms = wide(lambda h: jnp.broadcast_to(jnp.mean(hm[:, heads[h]] * hm[:, heads[h]], axis=-1, keepdims=True), (lc, dh)))
    hm = hm * lax.rsqrt(ms + EPS) * nw_ref[...]
    y = jax.nn.sigmoid(z_ref[rows, o0:o0 + ML_WIDTH]) * hm
    y_ref[rows, RG_WIDTH:RG_WIDTH + ML_WIDTH] = y.astype(y_ref.dtype)

    kw = k * wide(lambda h: _lane_col(w_loc, h))
    for h in range(ML_HEADS):
        c_loc = jnp.dot(kw[:, heads[h]].T.astype(BF16), vb[:, heads[h]], preferred_element_type=F32)
        n_loc = jnp.sum(kw[:, heads[h]], axis=0, keepdims=True)
        ap = a_prev[:, h:h + 1]
        al = a_loc[:, h:h + 1]
        c_ref[h] = ap * c_prev[h] + al * c_loc
        n_ref[h:h + 1, :] = ap * n_ref[h:h + 1, :] + al * n_loc
    m_ref[...] = jnp.broadcast_to(m_new, m_ref.shape)


def _hybrid_kernel(x_ref, xn_ref, nm_ref, win_ref, bin_ref, wgt_ref, bgt_ref,
                   rcw_ref, rcb_ref, wa_ref, ba_ref, wx_ref, bx_ref, lam_ref,
                   mcw_ref, mcb_ref, mnw_ref, wo_ref, o_ref,
                   za_ref, zb_ref, gta_ref, gtb_ref, y_ref, rtail_ref, hc_ref, a_ref, u_ref,
                   qtail_ref, ktail_ref, c_ref, n_ref, m_ref):
    step = pl.program_id(0) * pl.num_programs(1) + pl.program_id(1)

    @pl.when(pl.program_id(1) == 0)
    def _():
        for ref in (rtail_ref, hc_ref, qtail_ref, ktail_ref, c_ref, n_ref):
            ref[...] = jnp.zeros_like(ref)
        m_ref[...] = jnp.full_like(m_ref, -jnp.inf)

    n_main = win_ref.shape[1]
    n_piece = n_main // PROJ_CHUNK + 1

    def project(src_ref, z_ref, gt_ref):
        h = _rms(src_ref[0], nm_ref[...]).astype(BF16)

        def piece(c):
            if c < n_main // PROJ_CHUNK:
                cs = slice(c * PROJ_CHUNK, (c + 1) * PROJ_CHUNK)
                z_ref[:, cs] = jnp.dot(h, win_ref[:, cs], preferred_element_type=F32) + bin_ref[:, cs]
            else:
                gt_ref[...] = jnp.dot(h, wgt_ref[...], preferred_element_type=F32) + bgt_ref[...]
        return piece

    @pl.when(step == 0)
    def _():
        piece = project(x_ref, za_ref, gta_ref)
        for c in range(n_piece):
            piece(c)

    def mix(z_ref, gt_ref, zn_ref, gtn_ref):
        piece = project(xn_ref, zn_ref, gtn_ref)
        n_chunk = x_ref.shape[1] // ML_CHUNK
        _rg_gates(z_ref, rcw_ref, rcb_ref, wa_ref, ba_ref, wx_ref, bx_ref, lam_ref, rtail_ref, a_ref, u_ref)
        piece(0)
        _rg_scan(z_ref, y_ref, hc_ref, a_ref, u_ref)
        nxt = 1
        for c in range(n_chunk):
            upto = 1 + ((n_piece - 1) * (c + 1)) // n_chunk
            while nxt < upto:
                piece(nxt)
                nxt += 1
            rows = slice(c * ML_CHUNK, (c + 1) * ML_CHUNK)
            _ml_chunk(rows, z_ref, gt_ref, mcw_ref, mcb_ref, mnw_ref, y_ref, qtail_ref, ktail_ref, c_ref, n_ref, m_ref)
            o_ref[0, rows, :] = x_ref[0, rows, :] + jnp.dot(y_ref[rows, :], wo_ref[...], preferred_element_type=F32)

    @pl.when(step % 2 == 0)
    def _():
        mix(za_ref, gta_ref, zb_ref, gtb_ref)

    @pl.when(step % 2 == 1)
    def _():
        mix(zb_ref, gtb_ref, za_ref, gta_ref)


def _next_block_map(bsz, n_blk):
    def index(i, j):
        f = jnp.minimum(i * n_blk + j + 1, bsz * n_blk - 1)
        return f // n_blk, f % n_blk, 0
    return index


def _hybrid_mixer(x, nm, w_in, b_in, w_gt, b_gt, rcw, rcb, wa, ba, wx, bx, lam, mcw, mcb, mnw, wo):
    bsz, s, d = x.shape
    ts = min(TS_MIX, s)
    n_main = w_in.shape[1]
    rw, mw = RG_WIDTH, ML_WIDTH
    consts = (nm, w_in, b_in, w_gt, b_gt, rcw, rcb, wa, ba, wx, bx, lam, mcw, mcb, mnw, wo)
    return pl.pallas_call(
        _hybrid_kernel,
        out_shape=jax.ShapeDtypeStruct((bsz, s, d), F32),
        grid=(bsz, s // ts),
        in_specs=[pl.BlockSpec((1, ts, d), lambda i, j: (i, j, 0)),
                  pl.BlockSpec((1, ts, d), _next_block_map(bsz, s // ts))] + [_resident(c.shape) for c in consts],
        out_specs=pl.BlockSpec((1, ts, d), lambda i, j: (i, j, 0)),
        scratch_shapes=[pltpu.VMEM((ts, n_main), F32), pltpu.VMEM((ts, n_main), F32),
                        pltpu.VMEM((ts, LANES), F32), pltpu.VMEM((ts, LANES), F32),
                        pltpu.VMEM((ts, rw + mw), BF16),
                        pltpu.VMEM((SUBLANES, rw), F32), pltpu.VMEM((SUBLANES, rw), F32),
                        pltpu.VMEM((ts, rw), F32), pltpu.VMEM((ts, rw), F32),
                        pltpu.VMEM((SUBLANES, mw), F32), pltpu.VMEM((SUBLANES, mw), F32),
                        pltpu.VMEM((ML_HEADS, ML_HEAD_DIM, ML_HEAD_DIM), F32),
                        pltpu.VMEM((SUBLANES, ML_HEAD_DIM), F32),
                        pltpu.VMEM((SUBLANES, LANES), F32)],
        compiler_params=_params(2),
        name="hybrid_mixer",
    )(x, x, *consts)


def _attn_kernel(x_ref, xn_ref, nm_ref, wqkv_ref, pos_ref, freq_ref, qn_ref, kn_ref, seg_ref, sink_ref, wo_ref,
                 o_ref, qkva_ref, qkvb_ref, qs_ref, kf_ref, vt_ref, att_ref):
    tq = x_ref.shape[1]
    w = WINDOW
    hd = AT_HEAD_DIM
    hh = hd // 2
    n_sub = tq // w
    qw = AT_HEADS * hd
    kw = AT_KV_HEADS * hd
    n_qslab = qw // LANES
    n_kslab = kw // LANES
    n_pack = LANES // hh
    rp = w // n_pack
    first = pl.program_id(1) == 0
    step = pl.program_id(0) * pl.num_programs(1) + pl.program_id(1)

    @pl.when(first)
    def _():
        kf_ref[:, :, 0:w, :] = jnp.zeros((AT_KV_HEADS, 2, w, LANES), kf_ref.dtype)
        vt_ref[:, :, 0:w] = jnp.zeros((AT_KV_HEADS, hd, w), vt_ref.dtype)

    lane = lax.broadcasted_iota(jnp.int32, (w, LANES), 1)
    slot_a = (lane // hh) % 2 == 0
    sin_sign = jnp.where(lane < hd, -1.0, 1.0)
    lane_grp = lax.broadcasted_iota(jnp.int32, (rp, LANES), 1) // hh
    seg = seg_ref[...]
    n_piece = (qw + 2 * kw) // PROJ_CHUNK

    def project(src_ref, dst_ref):
        h = _rms(src_ref[0], nm_ref[...]).astype(BF16)

        def piece(c):
            cs = slice(c * PROJ_CHUNK, (c + 1) * PROJ_CHUNK)
            dst_ref[:, cs] = jnp.dot(h, wqkv_ref[:, cs], preferred_element_type=F32)
        return piece

    @pl.when(step == 0)
    def _():
        piece = project(x_ref, qkva_ref)
        for c in range(n_piece):
            piece(c)

    def norm_rope(t, gain, cos, sin):
        ms = jnp.dot((t * t).astype(BF16), seg, preferred_element_type=F32) * (1.0 / hd)
        tn = t * lax.rsqrt(ms + EPS) * gain
        return tn * cos + pltpu.roll(tn, hd, 1) * sin

    def spread(t):
        parts = []
        for g in range(n_pack):
            z = jnp.where(lane_grp == g, t, 0.0)
            y = z
            for k in range(1, n_pack):
                y = y + pltpu.roll(z, k * hh, 1)
            parts.append(y)
        return jnp.concatenate(parts, axis=0)

    def prepare(i, qkv_ref):
        rows = slice(i * w, (i + 1) * w)
        krows = slice(w + i * w, w + (i + 1) * w)
        pos = pos_ref[0, 0, i:i + 1, :].astype(F32)
        pos_col = jnp.broadcast_to(pos, (w, LANES)).T
        packed = pos_col[0:rp]
        for g in range(1, n_pack):
            packed = jnp.where(lane_grp == g, pos_col[g * rp:(g + 1) * rp], packed)
        ang = packed * freq_ref[...]
        cos = spread(jnp.cos(ang))
        sin = spread(jnp.sin(ang)) * sin_sign
        for sl in range(n_qslab):
            xq = norm_rope(qkv_ref[rows, sl * LANES:(sl + 1) * LANES], qn_ref[...], cos, sin) * (hd ** -0.5)
            qs_ref[sl, rows, :] = xq.astype(qs_ref.dtype)
        for sl in range(n_kslab):
            xk = norm_rope(qkv_ref[rows, qw + sl * LANES:qw + (sl + 1) * LANES], kn_ref[...], cos, sin)
            kf_ref[2 * sl, 0, krows, :] = jnp.where(slot_a, xk, 0.0).astype(kf_ref.dtype)
            kf_ref[2 * sl, 1, krows, :] = jnp.where(slot_a, 0.0, pltpu.roll(xk, hh, 1)).astype(kf_ref.dtype)
            kf_ref[2 * sl + 1, 0, krows, :] = jnp.where(slot_a, pltpu.roll(xk, LANES - hh, 1), 0.0).astype(kf_ref.dtype)
            kf_ref[2 * sl + 1, 1, krows, :] = jnp.where(slot_a, 0.0, xk).astype(kf_ref.dtype)
            v0 = qw + kw + sl * LANES
            vt = qkv_ref[rows, v0:v0 + LANES].T.astype(vt_ref.dtype)
            vt_ref[2 * sl, :, krows] = vt[:hd]
            vt_ref[2 * sl + 1, :, krows] = vt[hd:]

    grp = AT_HEADS // AT_KV_HEADS
    row_a = lax.broadcasted_iota(jnp.int32, (w, AT_HEADS * w), 0)
    lane_a = lax.broadcasted_iota(jnp.int32, (w, AT_HEADS * w), 1)
    from_prev = row_a > (lane_a % w)
    nt = (((1,), (1,)), ((), ()))

    def attend(i):
        rows = slice(i * w, (i + 1) * w)
        win = slice(i * w, (i + 2) * w)
        parts = []
        for hk in range(AT_KV_HEADS):
            q2 = jnp.concatenate([qs_ref[2 * hk, rows, :], qs_ref[2 * hk + 1, rows, :]], axis=0)
            parts.append(lax.dot_general(kf_ref[hk, 0, win, :], q2, nt, preferred_element_type=F32))
            parts.append(lax.dot_general(kf_ref[hk, 1, win, :], q2, nt, preferred_element_type=F32))
        s = jnp.concatenate(parts, axis=1)
        s_prev = s[:w]
        if i == 0:
            s_prev = s_prev + jnp.where(first, -jnp.inf, 0.0)
        logits = jnp.where(from_prev, s_prev, s[w:])
        sink = sink_ref[...]
        m = jnp.maximum(jnp.max(logits, axis=0, keepdims=True), sink)
        p = jnp.exp(logits - m)
        denom = jnp.sum(p, axis=0, keepdims=True) + jnp.exp(sink - m)
        pn = p * (1.0 / denom)
        pp = jnp.concatenate([jnp.where(from_prev, pn, 0.0), jnp.where(from_prev, 0.0, pn)],
                             axis=0).astype(BF16)
        for hk in range(AT_KV_HEADS):
            r = jnp.dot(vt_ref[hk, :, win], pp[:, hk * grp * w:(hk + 1) * grp * w],
                        preferred_element_type=F32)
            for pair in range(grp // 2):
                out_t = jnp.concatenate([r[:, pair * w:(pair + 1) * w],
                                         r[:, (2 + pair) * w:(3 + pair) * w]], axis=0)
                sl = 2 * hk + pair
                att_ref[rows, sl * LANES:(sl + 1) * LANES] = out_t.T.astype(att_ref.dtype)

    def mix(qkv_ref, qkvn_ref):
        piece = project(xn_ref, qkvn_ref)
        piece(0)
        for i in range(n_sub):
            prepare(i, qkv_ref)
        nxt = 1
        for i in range(n_sub):
            attend(i)
            if i % 2 == 1:
                rows = slice((i - 1) * w, (i + 1) * w)
                o_ref[0, rows, :] = x_ref[0, rows, :] + jnp.dot(att_ref[rows, :], wo_ref[...],
                                                                preferred_element_type=F32)
            upto = 1 + ((n_piece - 1) * (i + 1)) // n_sub
            while nxt < upto:
                piece(nxt)
                nxt += 1
        kf_ref[:, :, 0:w, :] = kf_ref[:, :, tq:tq + w, :]
        vt_ref[:, :, 0:w] = vt_ref[:, :, tq:tq + w]

    @pl.when(step % 2 == 0)
    def _():
        mix(qkva_ref, qkvb_ref)

    @pl.when(step % 2 == 1)
    def _():
        mix(qkvb_ref, qkva_ref)


def _slab_lane_dims():
    hh = AT_HEAD_DIM // 2
    lane = np.arange(LANES)
    return (lane // hh) % 2, lane % hh + hh * (lane // AT_HEAD_DIM)


def _attn_mixer(x, nm, w_qkv, positions, q_norm, k_norm, sinks, wo):
    bsz, s, d = x.shape
    tq = min(TS_MIX, s)
    hd = AT_HEAD_DIM
    qw, kw = AT_HEADS * hd, AT_KV_HEADS * hd
    grp = AT_HEADS // AT_KV_HEADS
    inv_freq = ROPE_THETA ** (-jnp.arange(hd // 2, dtype=F32) * (2.0 / hd))
    freq = jnp.tile(inv_freq, LANES // (hd // 2))[None, :]
    head_in_slab, dim = _slab_lane_dims()
    seg = jnp.asarray((head_in_slab[:, None] == head_in_slab[None, :]).astype(np.float32), dtype=BF16)
    n_slab = (qw + kw) // LANES
    cols = np.concatenate([sl * LANES + head_in_slab * hd + dim for sl in range(n_slab)]
                          + [np.arange(qw + kw, qw + 2 * kw)])
    w_qkv = w_qkv[:, cols]
    qn = q_norm.astype(F32)[dim][None, :]
    kn = k_norm.astype(F32)[dim][None, :]
    pos = positions.reshape(bsz, s // tq, tq // WINDOW, WINDOW)
    sk = sinks.astype(F32).reshape(AT_KV_HEADS, grp)[:, np.array([0, 2, 1, 3])]
    sk = jnp.repeat(sk, WINDOW, axis=1).reshape(1, AT_HEADS * WINDOW)
    consts_a = (nm, w_qkv)
    consts_b = (freq, qn, kn, seg, sk, wo)
    return pl.pallas_call(
        _attn_kernel,
        out_shape=jax.ShapeDtypeStruct((bsz, s, d), F32),
        grid=(bsz, s // tq),
        in_specs=([pl.BlockSpec((1, tq, d), lambda i, j: (i, j, 0)),
                   pl.BlockSpec((1, tq, d), _next_block_map(bsz, s // tq))] + [_resident(c.shape) for c in consts_a]
                  + [pl.BlockSpec((1, 1, tq // WINDOW, WINDOW), lambda i, j: (i, j, 0, 0))]
                  + [_resident(c.shape) for c in consts_b]),
        out_specs=pl.BlockSpec((1, tq, d), lambda i, j: (i, j, 0)),
        scratch_shapes=[pltpu.VMEM((tq, qw + 2 * kw), F32), pltpu.VMEM((tq, qw + 2 * kw), F32),
                        pltpu.VMEM((qw // LANES, tq, LANES), BF16),
                        pltpu.VMEM((AT_KV_HEADS, 2, tq + WINDOW, LANES), BF16),
                        pltpu.VMEM((AT_KV_HEADS, hd, tq + WINDOW), BF16),
                        pltpu.VMEM((tq, qw), BF16)],
        compiler_params=_params(2),
        name="attn_mixer",
    )(x, x, *consts_a, pos, *consts_b)


def _post_kernel(x_ref, p_ref, nf_ref, wu_ref, cw_ref, cb_ref, wd_ref, np_ref, wg_ref, wp_ref, o_ref,
                 gtail_ref, act_ref):
    tm = x_ref.shape[1]
    ff = wd_ref.shape[0]

    @pl.when(pl.program_id(1) == 0)
    def _():
        gtail_ref[...] = jnp.zeros_like(gtail_ref)

    x1 = x_ref[0]
    h = _rms(x1, nf_ref[...]).astype(BF16)
    for c in range(ff // FF_CHUNK):
        cs = slice(c * FF_CHUNK, (c + 1) * FF_CHUNK)
        g = jnp.dot(h, wu_ref[:, cs], preferred_element_type=F32)
        u = jnp.dot(h, wu_ref[:, ff + c * FF_CHUNK:ff + (c + 1) * FF_CHUNK], preferred_element_type=F32)
        gc = _causal_conv(g, gtail_ref[:, cs], cw_ref.at[:, cs], cb_ref.at[:, cs])
        gtail_ref[:, cs] = g[tm - SUBLANES:]
        act_ref[:, cs] = (jax.nn.gelu(gc, approximate=True) * u).astype(act_ref.dtype)
    x2 = x1 + jnp.dot(act_ref[...], wd_ref[...], preferred_element_type=F32)
    gate = jax.nn.sigmoid(jnp.dot(_rms(x2, np_ref[...]).astype(BF16), wg_ref[...], preferred_element_type=F32))
    pe = jnp.dot(p_ref[0, 0].astype(BF16), wp_ref[...], preferred_element_type=F32)
    o_ref[0] = x2 + gate * pe


def _post_mixer(x, p, layer, nf, wu, cw, cb, wd, npl, wg, wp):
    bsz, s, d = x.shape
    tm = min(TM_POST, s)
    ff = wd.shape[1]
    stacked = (nf, wu, cw, cb, wd, npl, wg, wp)
    return pl.pallas_call(
        _post_kernel,
        out_shape=jax.ShapeDtypeStruct((bsz, s, d), F32),
        grid=(bsz, s // tm),
        in_specs=[pl.BlockSpec((1, tm, d), lambda i, j: (i, j, 0)),
                  pl.BlockSpec((1, 1, tm, p.shape[-1]), lambda i, j: (layer, i, j, 0))]
                 + [_layer_block(a, layer) for a in stacked],
        out_specs=pl.BlockSpec((1, tm, d), lambda i, j: (i, j, 0)),
        scratch_shapes=[pltpu.VMEM((SUBLANES, ff), F32), pltpu.VMEM((tm, ff), BF16)],
        compiler_params=_params(2),
        name=f"post_mixer_{layer}",
    )(x, p, *stacked)


def _block_diag(w):
    n, r, _ = w.shape
    eye = jnp.eye(n, dtype=w.dtype)
    return (eye[:, None, :, None] * w[:, :, None, :]).reshape(n * r, n * r)


def kernel(x, p, positions, norm_mix, norm_ffn, norm_ple, hy_w_in, hy_b_in, rg_conv_w, rg_conv_b, rg_w_a, rg_b_a, rg_w_x, rg_b_x, rg_lambda, ml_conv_w, ml_conv_b, ml_norm, hy_w_out, at_w_qkv, at_q_norm, at_k_norm, at_sinks, at_w_out, ff_w_up, ff_conv_w, ff_conv_b, ff_w_down, ple_w_gate, ple_w_proj):
    depth = p.shape[0]
    row = lambda v: v.astype(F32)[None, :]
    n_main = 2 * RG_WIDTH + 4 * ML_WIDTH
    post = (norm_ffn.astype(F32)[:, None, :], ff_w_up.astype(BF16), ff_conv_w.astype(F32),
            ff_conv_b.astype(F32)[:, None, :], ff_w_down.astype(BF16), norm_ple.astype(F32)[:, None, :],
            ple_w_gate.astype(BF16), ple_w_proj.astype(BF16))
    for layer in range(depth):
        if layer % 2 == 0:
            e = layer // 2
            pad = LANES - 2 * ML_HEADS
            w_gt = jnp.pad(hy_w_in[e, :, n_main:], ((0, 0), (0, pad))).astype(BF16)
            b_gt = jnp.pad(hy_b_in[e, n_main:], (0, pad)).astype(F32)[None, :]
            x = _hybrid_mixer(x, row(norm_mix[layer]), hy_w_in[e, :, :n_main].astype(BF16),
                              row(hy_b_in[e, :n_main]), w_gt, b_gt,
                              rg_conv_w[e], row(rg_conv_b[e]), _block_diag(rg_w_a[e]).astype(BF16), row(rg_b_a[e]),
                              _block_diag(rg_w_x[e]).astype(BF16), row(rg_b_x[e]), row(rg_lambda[e]),
                              ml_conv_w[e], row(ml_conv_b[e]), row(ml_norm[e]), hy_w_out[e].astype(BF16))
        else:
            o = layer // 2
            x = _attn_mixer(x, row(norm_mix[layer]), at_w_qkv[o].astype(BF16), positions, at_q_norm[o],
                            at_k_norm[o], at_sinks[o], at_w_out[o].astype(BF16))
        x = _post_mixer(x, p, layer, *post)
    return x
```

```python
import jax
import jax.numpy as jnp
import numpy as np
from jax import lax
from jax.experimental import pallas as pl
from jax.experimental.pallas import tpu as pltpu

F32 = jnp.float32
BF16 = jnp.bfloat16

D_MODEL = 1024
PLE_DIM = 256
RG_WIDTH = 512
RG_BLOCKS = 8
RG_C = 8.0
ML_HEADS = 4
ML_HEAD_DIM = 128
ML_WIDTH = ML_HEADS * ML_HEAD_DIM
ML_CHUNK = 128
AT_HEADS = 16
AT_KV_HEADS = 4
AT_HEAD_DIM = 64
WINDOW = 128
ROPE_THETA = 10000.0
FF_DIM = 3 * D_MODEL
EPS = 1e-6
LOG2_E = 1.4426950408889634

LANES = 128
SUBLANES = 8
VMEM_LIMIT_BYTES = 56 * 1024 * 1024

TS_MIX = 512
TM_POST = 512
FF_CHUNK = 3072
PROJ_CHUNK = 512


def _params(n_axes):
    return pltpu.CompilerParams(dimension_semantics=("arbitrary",) * n_axes,
                                vmem_limit_bytes=VMEM_LIMIT_BYTES)


def _resident(shape, index=None):
    n = len(shape)
    idx = (0,) * n if index is None else tuple(index)
    return pl.BlockSpec(shape, lambda *_: idx, pipeline_mode=pl.Buffered(1))


def _layer_block(arr, layer):
    shape = (None,) + arr.shape[1:]
    return _resident(shape, (layer,) + (0,) * (arr.ndim - 1))


def _rms(x, g):
    return x * lax.rsqrt(jnp.mean(x * x, axis=-1, keepdims=True) + EPS) * g


def _shift_rows(x, tail, d):
    rolled = pltpu.roll(x, d, 0)
    tail_rolled = pltpu.roll(tail, d, 0)
    row = lax.broadcasted_iota(jnp.int32, tail.shape, 0)
    head = jnp.where(row < d, tail_rolled, rolled[:SUBLANES])
    return jnp.concatenate([head, rolled[SUBLANES:]], axis=0)


def _causal_conv(x, tail, w_ref, b_ref):
    k_w = w_ref.shape[0]
    y = x * w_ref[k_w - 1:k_w, :] + b_ref[...]
    for j in range(k_w - 1):
        y = y + _shift_rows(x, tail, k_w - 1 - j) * w_ref[j:j + 1, :]
    return y


def _log_sigmoid(x):
    return jnp.minimum(x, 0.0) - jnp.log1p(jnp.exp(-jnp.abs(x)))


def _rg_gates(z_ref, cw_ref, cb_ref, wa_ref, ba_ref, wx_ref, bx_ref, lam_ref, tail_ref, a_ref, u_ref):
    ts = z_ref.shape[0]
    x = z_ref[:, 0:RG_WIDTH]
    xc = _causal_conv(x, tail_ref[...], cw_ref, cb_ref)
    tail_ref[...] = x[ts - SUBLANES:]
    xb = xc.astype(BF16)
    r = jax.nn.sigmoid(jnp.dot(xb, wa_ref[...], preferred_element_type=F32) + ba_ref[...])
    i = jax.nn.sigmoid(jnp.dot(xb, wx_ref[...], preferred_element_type=F32) + bx_ref[...])
    lam = lam_ref[...]
    softplus_neg_lam = jnp.maximum(-lam, 0.0) + jnp.log1p(jnp.exp(-jnp.abs(lam)))
    log_a = (-RG_C) * r * softplus_neg_lam
    a = jnp.exp(log_a)
    a_ref[...] = a
    v = 1.0 - a * a
    u_ref[...] = jnp.where(v > 0.0, v * lax.rsqrt(v), 0.0) * (i * xc)


def _rg_scan(z_ref, y_ref, hc_ref, a_ref, u_ref):
    ts = z_ref.shape[0]
    row = lax.broadcasted_iota(jnp.int32, (SUBLANES, RG_WIDTH), 0)
    hc = hc_ref[...]
    for k in range(ts // SUBLANES):
        rows = slice(k * SUBLANES, (k + 1) * SUBLANES)
        a = a_ref[rows, :]
        u = u_ref[rows, :]
        for d in (1, 2, 4):
            keep = row >= d
            u = u + a * jnp.where(keep, pltpu.roll(u, d, 0), 0.0)
            a = a * jnp.where(keep, pltpu.roll(a, d, 0), 1.0)
        h = u + a * hc
        u_ref[rows, :] = h
        hc = jnp.broadcast_to(h[SUBLANES - 1:SUBLANES, :], (SUBLANES, RG_WIDTH))
    hc_ref[...] = hc
    gate = jax.nn.gelu(z_ref[:, RG_WIDTH:2 * RG_WIDTH], approximate=True)
    y_ref[:, 0:RG_WIDTH] = (u_ref[...] * gate).astype(y_ref.dtype)


def _lane_col(x, h):
    return jnp.broadcast_to(x[:, h:h + 1], x.shape)


def _ml_chunk(rows, z_ref, gt_ref, cw_ref, cb_ref, nw_ref, y_ref, qtail_ref, ktail_ref, c_ref, n_ref, m_ref):
    lc = ML_CHUNK
    dh = ML_HEAD_DIM
    q0 = 2 * RG_WIDTH
    k0, v0, o0 = q0 + ML_WIDTH, q0 + 2 * ML_WIDTH, q0 + 3 * ML_WIDTH
    row = lax.broadcasted_iota(jnp.int32, (lc, LANES), 0)

    q_raw = z_ref[rows, q0:k0]
    k_raw = z_ref[rows, k0:v0]
    q = jax.nn.silu(_causal_conv(q_raw, qtail_ref[...], cw_ref.at[:, :ML_WIDTH], cb_ref.at[:, :ML_WIDTH]))
    k = jax.nn.silu(_causal_conv(k_raw, ktail_ref[...], cw_ref.at[:, ML_WIDTH:], cb_ref.at[:, ML_WIDTH:]))
    qtail_ref[...] = q_raw[lc - SUBLANES:]
    ktail_ref[...] = k_raw[lc - SUBLANES:]
    k = k * (dh ** -0.5)

    gates = gt_ref[rows, :]
    b = _log_sigmoid(gates)
    d = 1
    while d < lc:
        b = b + jnp.where(row >= d, pltpu.roll(b, d, 0), 0.0)
        d *= 2
    b = pltpu.roll(b, LANES - ML_HEADS, 1)
    r = gates - b
    b_last = b[lc - 1:lc, :]
    g_loc = b_last + r
    m_loc = jnp.max(g_loc, axis=0, keepdims=True)
    w_loc = jnp.exp(g_loc - m_loc)
    m_st = m_ref[0:1, :]
    m_new = jnp.maximum(b_last + m_st, m_loc)
    a_prev = jnp.exp(b_last + m_st - m_new)
    a_loc = jnp.exp(m_loc - m_new)
    rmax = r
    d = 1
    while d < lc:
        rmax = jnp.maximum(rmax, jnp.where(row >= d, pltpu.roll(rmax, d, 0), -jnp.inf))
        d *= 2
    e = -jnp.maximum(m_st, rmax)
    a_inter = jnp.exp(m_st + e)
    exp_neg_mt = jnp.exp(e - b)
    r_t = r.T

    heads = [slice(h * dh, (h + 1) * dh) for h in range(ML_HEADS)]
    wide = lambda f: jnp.concatenate([f(h) for h in range(ML_HEADS)], axis=1)
    nt = (((1,), (1,)), ((), ()))
    row_w = lax.broadcasted_iota(jnp.int32, (lc, ML_WIDTH), 0)
    col_w = lax.broadcasted_iota(jnp.int32, (lc, ML_WIDTH), 1)
    causal = row_w >= (col_w % dh)
    qb = q.astype(BF16)
    kb = k.astype(BF16)
    vb = z_ref[rows, v0:o0].astype(BF16)
    c_prev = [c_ref[h] for h in range(ML_HEADS)]
    n_prev = wide(lambda h: n_ref[h:h + 1, :])
    s = wide(lambda h: lax.dot_general(qb[:, heads[h]], kb[:, heads[h]], nt, preferred_element_type=F32))
    w_intra = jnp.where(causal, jnp.exp(wide(lambda h: _lane_col(e, h)) + wide(lambda h: r_t[h:h + 1, :])), 0.0)
    s_qk = s * w_intra
    sb = s_qk.astype(BF16)
    a_in = wide(lambda h: _lane_col(a_inter, h))
    num = (wide(lambda h: jnp.dot(sb[:, heads[h]], vb[:, heads[h]], preferred_element_type=F32))
           + a_in * wide(lambda h: jnp.dot(qb[:, heads[h]], c_prev[h].astype(BF16), preferred_element_type=F32)))
    t = s_qk + a_in * (q * n_prev)
    den = wide(lambda h: jnp.broadcast_to(jnp.sum(t[:, heads[h]], axis=-1, keepdims=True), (lc, dh)))
    hm = num / jnp.maximum(jnp.abs(den), wide(lambda h: _lane_col(exp_neg_mt, h)))
    ms = wide(lambda h: jnp.broadcast_to(jnp.mean(hm[:, heads[h]] * hm[:, heads[h]], axis=-1, keepdims=True), (lc, dh)))
    hm = hm * lax.rsqrt(ms + EPS) * nw_ref[...]
    y = jax.nn.sigmoid(z_ref[rows, o0:o0 + ML_WIDTH]) * hm
    y_ref[rows, RG_WIDTH:RG_WIDTH + ML_WIDTH] = y.astype(y_ref.dtype)

    kw = k * wide(lambda h: _lane_col(w_loc, h))
    for h in range(ML_HEADS):
        c_loc = jnp.dot(kw[:, heads[h]].T.astype(BF16), vb[:, heads[h]], preferred_element_type=F32)
        n_loc = jnp.sum(kw[:, heads[h]], axis=0, keepdims=True)
        ap = a_prev[:, h:h + 1]
        al = a_loc[:, h:h + 1]
        c_ref[h] = ap * c_prev[h] + al * c_loc
        n_ref[h:h + 1, :] = ap * n_ref[h:h + 1, :] + al * n_loc
    m_ref[...] = jnp.broadcast_to(m_new, m_ref.shape)


def _hybrid_kernel(x_ref, xn_ref, nm_ref, win_ref, bin_ref, wgt_ref, bgt_ref,
                   rcw_ref, rcb_ref, wa_ref, ba_ref, wx_ref, bx_ref, lam_ref,
                   mcw_ref, mcb_ref, mnw_ref, wo_ref, o_ref,
                   za_ref, zb_ref, gta_ref, gtb_ref, y_ref, rtail_ref, hc_ref, a_ref, u_ref,
                   qtail_ref, ktail_ref, c_ref, n_ref, m_ref):
    step = pl.program_id(0) * pl.num_programs(1) + pl.program_id(1)

    @pl.when(pl.program_id(1) == 0)
    def _():
        for ref in (rtail_ref, hc_ref, qtail_ref, ktail_ref, c_ref, n_ref):
            ref[...] = jnp.zeros_like(ref)
        m_ref[...] = jnp.full_like(m_ref, -jnp.inf)

    n_main = win_ref.shape[1]
    n_piece = n_main // PROJ_CHUNK + 1

    def project(src_ref, z_ref, gt_ref):
        h = _rms(src_ref[0], nm_ref[...]).astype(BF16)

        def piece(c):
            if c < n_main // PROJ_CHUNK:
                cs = slice(c * PROJ_CHUNK, (c + 1) * PROJ_CHUNK)
                z_ref[:, cs] = jnp.dot(h, win_ref[:, cs], preferred_element_type=F32) + bin_ref[:, cs]
            else:
                gt_ref[...] = jnp.dot(h, wgt_ref[...], preferred_element_type=F32) + bgt_ref[...]
        return piece

    @pl.when(step == 0)
    def _():
        piece = project(x_ref, za_ref, gta_ref)
        for c in range(n_piece):
            piece(c)

    def mix(z_ref, gt_ref, zn_ref, gtn_ref):
        piece = project(xn_ref, zn_ref, gtn_ref)
        n_chunk = x_ref.shape[1] // ML_CHUNK
        _rg_gates(z_ref, rcw_ref, rcb_ref, wa_ref, ba_ref, wx_ref, bx_ref, lam_ref, rtail_ref, a_ref, u_ref)
        piece(0)
        _rg_scan(z_ref, y_ref, hc_ref, a_ref, u_ref)
        nxt = 1
        for c in range(n_chunk):
            upto = 1 + ((n_piece - 1) * (c + 1)) // n_chunk
            while nxt < upto:
                piece(nxt)
                nxt += 1
            rows = slice(c * ML_CHUNK, (c + 1) * ML_CHUNK)
            _ml_chunk(rows, z_ref, gt_ref, mcw_ref, mcb_ref, mnw_ref, y_ref, qtail_ref, ktail_ref, c_ref, n_ref, m_ref)
            o_ref[0, rows, :] = x_ref[0, rows, :] + jnp.dot(y_ref[rows, :], wo_ref[...], preferred_element_type=F32)

    @pl.when(step % 2 == 0)
    def _():
        mix(za_ref, gta_ref, zb_ref, gtb_ref)

    @pl.when(step % 2 == 1)
    def _():
        mix(zb_ref, gtb_ref, za_ref, gta_ref)


def _next_block_map(bsz, n_blk):
    def index(i, j):
        f = jnp.minimum(i * n_blk + j + 1, bsz * n_blk - 1)
        return f // n_blk, f % n_blk, 0
    return index


def _hybrid_mixer(x, nm, w_in, b_in, w_gt, b_gt, rcw, rcb, wa, ba, wx, bx, lam, mcw, mcb, mnw, wo):
    bsz, s, d = x.shape
    ts = min(TS_MIX, s)
    n_main = w_in.shape[1]
    rw, mw = RG_WIDTH, ML_WIDTH
    consts = (nm, w_in, b_in, w_gt, b_gt, rcw, rcb, wa, ba, wx, bx, lam, mcw, mcb, mnw, wo)
    return pl.pallas_call(
        _hybrid_kernel,
        out_shape=jax.ShapeDtypeStruct((bsz, s, d), F32),
        grid=(bsz, s // ts),
        in_specs=[pl.BlockSpec((1, ts, d), lambda i, j: (i, j, 0)),
                  pl.BlockSpec((1, ts, d), _next_block_map(bsz, s // ts))] + [_resident(c.shape) for c in consts],
        out_specs=pl.BlockSpec((1, ts, d), lambda i, j: (i, j, 0)),
        scratch_shapes=[pltpu.VMEM((ts, n_main), F32), pltpu.VMEM((ts, n_main), F32),
                        pltpu.VMEM((ts, LANES), F32), pltpu.VMEM((ts, LANES), F32),
                        pltpu.VMEM((ts, rw + mw), BF16),
                        pltpu.VMEM((SUBLANES, rw), F32), pltpu.VMEM((SUBLANES, rw), F32),
                        pltpu.VMEM((ts, rw), F32), pltpu.VMEM((ts, rw), F32),
                        pltpu.VMEM((SUBLANES, mw), F32), pltpu.VMEM((SUBLANES, mw), F32),
                        pltpu.VMEM((ML_HEADS, ML_HEAD_DIM, ML_HEAD_DIM), F32),
                        pltpu.VMEM((SUBLANES, ML_HEAD_DIM), F32),
                        pltpu.VMEM((SUBLANES, LANES), F32)],
        compiler_params=_params(2),
        name="hybrid_mixer",
    )(x, x, *consts)


def _attn_kernel(x_ref, xn_ref, nm_ref, wqkv_ref, pos_ref, freq_ref, qn_ref, kn_ref, seg_ref, sink_ref, wo_ref,
                 o_ref, qkva_ref, qkvb_ref, qs_ref, kf_ref, vt_ref, att_ref):
    tq = x_ref.shape[1]
    w = WINDOW
    hd = AT_HEAD_DIM
    hh = hd // 2
    n_sub = tq // w
    qw = AT_HEADS * hd
    kw = AT_KV_HEADS * hd
    n_qslab = qw // LANES
    n_kslab = kw // LANES
    n_pack = LANES // hh
    rp = w // n_pack
    first = pl.program_id(1) == 0
    step = pl.program_id(0) * pl.num_programs(1) + pl.program_id(1)

    @pl.when(first)
    def _():
        kf_ref[:, :, 0:w, :] = jnp.zeros((AT_KV_HEADS, 2, w, LANES), kf_ref.dtype)
        vt_ref[:, :, 0:w] = jnp.zeros((AT_KV_HEADS, hd, w), vt_ref.dtype)

    lane = lax.broadcasted_iota(jnp.int32, (w, LANES), 1)
    slot_a = (lane // hh) % 2 == 0
    sin_sign = jnp.where(lane < hd, -1.0, 1.0)
    lane_grp = lax.broadcasted_iota(jnp.int32, (rp, LANES), 1) // hh
    seg = seg_ref[...]
    n_piece = (qw + 2 * kw) // PROJ_CHUNK

    def project(src_ref, dst_ref):
        h = _rms(src_ref[0], nm_ref[...]).astype(BF16)

        def piece(c):
            cs = slice(c * PROJ_CHUNK, (c + 1) * PROJ_CHUNK)
            dst_ref[:, cs] = jnp.dot(h, wqkv_ref[:, cs], preferred_element_type=F32)
        return piece

    @pl.when(step == 0)
    def _():
        piece = project(x_ref, qkva_ref)
        for c in range(n_piece):
            piece(c)

    def norm_rope(t, gain, cos, sin):
        ms = jnp.dot((t * t).astype(BF16), seg, preferred_element_type=F32) * (1.0 / hd)
        tn = t * lax.rsqrt(ms + EPS) * gain
        return tn * cos + pltpu.roll(tn, hd, 1) * sin

    def spread(t):
        parts = []
        for g in range(n_pack):
            z = jnp.where(lane_grp == g, t, 0.0)
            y = z
            for k in range(1, n_pack):
                y = y + pltpu.roll(z, k * hh, 1)
            parts.append(y)
        return jnp.concatenate(parts, axis=0)

    def prepare(i, qkv_ref):
        rows = slice(i * w, (i + 1) * w)
        krows = slice(w + i * w, w + (i + 1) * w)
        pos = pos_ref[0, 0, i:i + 1, :].astype(F32)
        pos_col = jnp.broadcast_to(pos, (w, LANES)).T
        packed = pos_col[0:rp]
        for g in range(1, n_pack):
            packed = jnp.where(lane_grp == g, pos_col[g * rp:(g + 1) * rp], packed)
        ang = packed * freq_ref[...]
        cos = spread(jnp.cos(ang))
        sin = spread(jnp.sin(ang)) * sin_sign
        for sl in range(n_qslab):
            xq = norm_rope(qkv_ref[rows, sl * LANES:(sl + 1) * LANES], qn_ref[...], cos, sin) * (LOG2_E * hd ** -0.5)
            qs_ref[sl, rows, :] = xq.astype(qs_ref.dtype)
        for sl in range(n_kslab):
            xk = norm_rope(qkv_ref[rows, qw + sl * LANES:qw + (sl + 1) * LANES], kn_ref[...], cos, sin)
            kf_ref[2 * sl, 0, krows, :] = jnp.where(slot_a, xk, 0.0).astype(kf_ref.dtype)
            kf_ref[2 * sl, 1, krows, :] = jnp.where(slot_a, 0.0, pltpu.roll(xk, hh, 1)).astype(kf_ref.dtype)
            kf_ref[2 * sl + 1, 0, krows, :] = jnp.where(slot_a, pltpu.roll(xk, LANES - hh, 1), 0.0).astype(kf_ref.dtype)
            kf_ref[2 * sl + 1, 1, krows, :] = jnp.where(slot_a, 0.0, xk).astype(kf_ref.dtype)
            v0 = qw + kw + sl * LANES
            vt = qkv_ref[rows, v0:v0 + LANES].T.astype(vt_ref.dtype)
            vt_ref[2 * sl, :, krows] = vt[:hd]
            vt_ref[2 * sl + 1, :, krows] = vt[hd:]

    grp = AT_HEADS // AT_KV_HEADS
    row_a = lax.broadcasted_iota(jnp.int32, (w, AT_HEADS * w), 0)
    lane_a = lax.broadcasted_iota(jnp.int32, (w, AT_HEADS * w), 1)
    from_prev = row_a > (lane_a % w)
    nt = (((1,), (1,)), ((), ()))

    def attend(i):
        rows = slice(i * w, (i + 1) * w)
        win = slice(i * w, (i + 2) * w)
        parts = []
        for hk in range(AT_KV_HEADS):
            q2 = jnp.concatenate([qs_ref[2 * hk, rows, :], qs_ref[2 * hk + 1, rows, :]], axis=0)
            parts.append(lax.dot_general(kf_ref[hk, 0, win, :], q2, nt, preferred_element_type=F32))
            parts.append(lax.dot_general(kf_ref[hk, 1, win, :], q2, nt, preferred_element_type=F32))
        s = jnp.concatenate(parts, axis=1)
        s_prev = s[:w]
        if i == 0:
            s_prev = s_prev + jnp.where(first, -jnp.inf, 0.0)
        logits = jnp.where(from_prev, s_prev, s[w:])
        sink = sink_ref[...]
        m = jnp.maximum(jnp.max(logits, axis=0, keepdims=True), sink)
        p = jnp.exp2(logits - m)
        inv = 1.0 / (jnp.sum(p, axis=0, keepdims=True) + jnp.exp2(sink - m))
        pp = jnp.concatenate([jnp.where(from_prev, p, 0.0), jnp.where(from_prev, 0.0, p)],
                             axis=0).astype(BF16)
        for hk in range(AT_KV_HEADS):
            cols = slice(hk * grp * w, (hk + 1) * grp * w)
            r = jnp.dot(vt_ref[hk, :, win], pp[:, cols], preferred_element_type=F32) * inv[:, cols]
            for pair in range(grp // 2):
                out_t = jnp.concatenate([r[:, pair * w:(pair + 1) * w],
                                         r[:, (2 + pair) * w:(3 + pair) * w]], axis=0)
                sl = 2 * hk + pair
                att_ref[rows, sl * LANES:(sl + 1) * LANES] = out_t.T.astype(att_ref.dtype)

    def mix(qkv_ref, qkvn_ref):
        piece = project(xn_ref, qkvn_ref)
        piece(0)
        for i in range(n_sub):
            prepare(i, qkv_ref)
        nxt = 1
        for i in range(n_sub):
            attend(i)
            if i % 2 == 1:
                rows = slice((i - 1) * w, (i + 1) * w)
                o_ref[0, rows, :] = x_ref[0, rows, :] + jnp.dot(att_ref[rows, :], wo_ref[...],
                                                                preferred_element_type=F32)
            upto = 1 + ((n_piece - 1) * (i + 1)) // n_sub
            while nxt < upto:
                piece(nxt)
                nxt += 1
        kf_ref[:, :, 0:w, :] = kf_ref[:, :, tq:tq + w, :]
        vt_ref[:, :, 0:w] = vt_ref[:, :, tq:tq + w]

    @pl.when(step % 2 == 0)
    def _():
        mix(qkva_ref, qkvb_ref)

    @pl.when(step % 2 == 1)
    def _():
        mix(qkvb_ref, qkva_ref)


def _slab_lane_dims():
    hh = AT_HEAD_DIM // 2
    lane = np.arange(LANES)
    return (lane // hh) % 2, lane % hh + hh * (lane // AT_HEAD_DIM)


def _attn_mixer(x, nm, w_qkv, positions, q_norm, k_norm, sinks, wo):
    bsz, s, d = x.shape
    tq = min(TS_MIX, s)
    hd = AT_HEAD_DIM
    qw, kw = AT_HEADS * hd, AT_KV_HEADS * hd
    grp = AT_HEADS // AT_KV_HEADS
    inv_freq = ROPE_THETA ** (-jnp.arange(hd // 2, dtype=F32) * (2.0 / hd))
    freq = jnp.tile(inv_freq, LANES // (hd // 2))[None, :]
    head_in_slab, dim = _slab_lane_dims()
    seg = jnp.asarray((head_in_slab[:, None] == head_in_slab[None, :]).astype(np.float32), dtype=BF16)
    n_slab = (qw + kw) // LANES
    cols = np.concatenate([sl * LANES + head_in_slab * hd + dim for sl in range(n_slab)]
                          + [np.arange(qw + kw, qw + 2 * kw)])
    w_qkv = w_qkv[:, cols]
    qn = q_norm.astype(F32)[dim][None, :]
    kn = k_norm.astype(F32)[dim][None, :]
    pos = positions.reshape(bsz, s // tq, tq // WINDOW, WINDOW)
    sk = (sinks.astype(F32) * LOG2_E).reshape(AT_KV_HEADS, grp)[:, np.array([0, 2, 1, 3])]
    sk = jnp.repeat(sk, WINDOW, axis=1).reshape(1, AT_HEADS * WINDOW)
    consts_a = (nm, w_qkv)
    consts_b = (freq, qn, kn, seg, sk, wo)
    return pl.pallas_call(
        _attn_kernel,
        out_shape=jax.ShapeDtypeStruct((bsz, s, d), F32),
        grid=(bsz, s // tq),
        in_specs=([pl.BlockSpec((1, tq, d), lambda i, j: (i, j, 0)),
                   pl.BlockSpec((1, tq, d), _next_block_map(bsz, s // tq))] + [_resident(c.shape) for c in consts_a]
                  + [pl.BlockSpec((1, 1, tq // WINDOW, WINDOW), lambda i, j: (i, j, 0, 0))]
                  + [_resident(c.shape) for c in consts_b]),
        out_specs=pl.BlockSpec((1, tq, d), lambda i, j: (i, j, 0)),
        scratch_shapes=[pltpu.VMEM((tq, qw + 2 * kw), F32), pltpu.VMEM((tq, qw + 2 * kw), F32),
                        pltpu.VMEM((qw // LANES, tq, LANES), BF16),
                        pltpu.VMEM((AT_KV_HEADS, 2, tq + WINDOW, LANES), BF16),
                        pltpu.VMEM((AT_KV_HEADS, hd, tq + WINDOW), BF16),
                        pltpu.VMEM((tq, qw), BF16)],
        compiler_params=_params(2),
        name="attn_mixer",
    )(x, x, *consts_a, pos, *consts_b)


def _post_kernel(x_ref, p_ref, nf_ref, wu_ref, cw_ref, cb_ref, wd_ref, np_ref, wg_ref, wp_ref, o_ref,
                 gtail_ref, act_ref):
    tm = x_ref.shape[1]
    ff = wd_ref.shape[0]

    @pl.when(pl.program_id(1) == 0)
    def _():
        gtail_ref[...] = jnp.zeros_like(gtail_ref)

    x1 = x_ref[0]
    h = _rms(x1, nf_ref[...]).astype(BF16)
    for c in range(ff // FF_CHUNK):
        cs = slice(c * FF_CHUNK, (c + 1) * FF_CHUNK)
        g = jnp.dot(h, wu_ref[:, cs], preferred_element_type=F32)
        u = jnp.dot(h, wu_ref[:, ff + c * FF_CHUNK:ff + (c + 1) * FF_CHUNK], preferred_element_type=F32)
        gc = _causal_conv(g, gtail_ref[:, cs], cw_ref.at[:, cs], cb_ref.at[:, cs])
        gtail_ref[:, cs] = g[tm - SUBLANES:]
        act_ref[:, cs] = (jax.nn.gelu(gc, approximate=True) * u).astype(act_ref.dtype)
    x2 = x1 + jnp.dot(act_ref[...], wd_ref[...], preferred_element_type=F32)
    gate = jax.nn.sigmoid(jnp.dot(_rms(x2, np_ref[...]).astype(BF16), wg_ref[...], preferred_element_type=F32))
    pe = jnp.dot(p_ref[0, 0].astype(BF16), wp_ref[...], preferred_element_type=F32)
    o_ref[0] = x2 + gate * pe


def _post_mixer(x, p, layer, nf, wu, cw, cb, wd, npl, wg, wp):
    bsz, s, d = x.shape
    tm = min(TM_POST, s)
    ff = wd.shape[1]
    stacked = (nf, wu, cw, cb, wd, npl, wg, wp)
    return pl.pallas_call(
        _post_kernel,
        out_shape=jax.ShapeDtypeStruct((bsz, s, d), F32),
        grid=(bsz, s // tm),
        in_specs=[pl.BlockSpec((1, tm, d), lambda i, j: (i, j, 0)),
                  pl.BlockSpec((1, 1, tm, p.shape[-1]), lambda i, j: (layer, i, j, 0))]
                 + [_layer_block(a, layer) for a in stacked],
        out_specs=pl.BlockSpec((1, tm, d), lambda i, j: (i, j, 0)),
        scratch_shapes=[pltpu.VMEM((SUBLANES, ff), F32), pltpu.VMEM((tm, ff), BF16)],
        compiler_params=_params(2),
        name=f"post_mixer_{layer}",
    )(x, p, *stacked)


def _block_diag(w):
    n, r, _ = w.shape
    eye = jnp.eye(n, dtype=w.dtype)
    return (eye[:, None, :, None] * w[:, :, None, :]).reshape(n * r, n * r)


def kernel(x, p, positions, norm_mix, norm_ffn, norm_ple, hy_w_in, hy_b_in, rg_conv_w, rg_conv_b, rg_w_a, rg_b_a, rg_w_x, rg_b_x, rg_lambda, ml_conv_w, ml_conv_b, ml_norm, hy_w_out, at_w_qkv, at_q_norm, at_k_norm, at_sinks, at_w_out, ff_w_up, ff_conv_w, ff_conv_b, ff_w_down, ple_w_gate, ple_w_proj):
    depth = p.shape[0]
    row = lambda v: v.astype(F32)[None, :]
    n_main = 2 * RG_WIDTH + 4 * ML_WIDTH
    post = (norm_ffn.astype(F32)[:, None, :], ff_w_up.astype(BF16), ff_conv_w.astype(F32),
            ff_conv_b.astype(F32)[:, None, :], ff_w_down.astype(BF16), norm_ple.astype(F32)[:, None, :],
            ple_w_gate.astype(BF16), ple_w_proj.astype(BF16))
    for layer in range(depth):
        if layer % 2 == 0:
            e = layer // 2
            pad = LANES - 2 * ML_HEADS
            w_gt = jnp.pad(hy_w_in[e, :, n_main:], ((0, 0), (0, pad))).astype(BF16)
            b_gt = jnp.pad(hy_b_in[e, n_main:], (0, pad)).astype(F32)[None, :]
            x = _hybrid_mixer(x, row(norm_mix[layer]), hy_w_in[e, :, :n_main].astype(BF16),
                              row(hy_b_in[e, :n_main]), w_gt, b_gt,
                              rg_conv_w[e], row(rg_conv_b[e]), _block_diag(rg_w_a[e]).astype(BF16), row(rg_b_a[e]),
                              _block_diag(rg_w_x[e]).astype(BF16), row(rg_b_x[e]), row(rg_lambda[e]),
                              ml_conv_w[e], row(ml_conv_b[e]), row(ml_norm[e]), hy_w_out[e].astype(BF16))
        else:
            o = layer // 2
            x = _attn_mixer(x, row(norm_mix[layer]), at_w_qkv[o].astype(BF16), positions, at_q_norm[o],
                            at_k_norm[o], at_sinks[o], at_w_out[o].astype(BF16))
        x = _post_mixer(x, p, layer, *post)
    return x
```

```python
import jax
import jax.numpy as jnp
import numpy as np
from jax import lax
from jax.experimental import pallas as pl
from jax.experimental.pallas import tpu as pltpu

F32 = jnp.float32
BF16 = jnp.bfloat16

D_MODEL = 1024
PLE_DIM = 256
RG_WIDTH = 512
RG_BLOCKS = 8
RG_C = 8.0
ML_HEADS = 4
ML_HEAD_DIM = 128
ML_WIDTH = ML_HEADS * ML_HEAD_DIM
ML_CHUNK = 128
AT_HEADS = 16
AT_KV_HEADS = 4
AT_HEAD_DIM = 64
WINDOW = 128
ROPE_THETA = 10000.0
FF_DIM = 3 * D_MODEL
EPS = 1e-6
LOG2_E = 1.4426950408889634

LANES = 128
SUBLANES = 8
VMEM_LIMIT_BYTES = 56 * 1024 * 1024

TS_MIX = 512
TM_POST = 512
FF_CHUNK = 3072
PROJ_CHUNK = 512
RG_SLAB0 = 0
Q_SLAB0 = RG_WIDTH // LANES
K_SLAB0 = Q_SLAB0 + ML_WIDTH // LANES
N_CONV_SLAB = K_SLAB0 + ML_WIDTH // LANES


def _params(n_axes):
    return pltpu.CompilerParams(dimension_semantics=("arbitrary",) * n_axes,
                                vmem_limit_bytes=VMEM_LIMIT_BYTES)


def _resident(shape, index=None):
    n = len(shape)
    idx = (0,) * n if index is None else tuple(index)
    return pl.BlockSpec(shape, lambda *_: idx, pipeline_mode=pl.Buffered(1))


def _layer_block(arr, layer):
    shape = (None,) + arr.shape[1:]
    return _resident(shape, (layer,) + (0,) * (arr.ndim - 1))


def _rms(x, g):
    return x * lax.rsqrt(jnp.mean(x * x, axis=-1, keepdims=True) + EPS) * g


def _shift_rows(x, tail, d):
    rolled = pltpu.roll(x, d, 0)
    tail_rolled = pltpu.roll(tail, d, 0)
    row = lax.broadcasted_iota(jnp.int32, tail.shape, 0)
    head = jnp.where(row < d, tail_rolled, rolled[:SUBLANES])
    return jnp.concatenate([head, rolled[SUBLANES:]], axis=0)


def _causal_conv(x, tail, w_ref, b_ref):
    k_w = w_ref.shape[0]
    y = x * w_ref[k_w - 1:k_w, :] + b_ref[...]
    for j in range(k_w - 1):
        y = y + _shift_rows(x, tail, k_w - 1 - j) * w_ref[j:j + 1, :]
    return y


def _slab_conv(zs_ref, slab0, n_slab, r0, n_rows, w_ref, b_ref):
    k_w = w_ref.shape[0]
    base = SUBLANES + r0
    outs = []
    for s in range(n_slab):
        cs = slice(s * LANES, (s + 1) * LANES)
        acc = zs_ref[slab0 + s, base:base + n_rows, :] * w_ref[k_w - 1:k_w, cs] + b_ref[:, cs]
        for j in range(k_w - 1):
            d = k_w - 1 - j
            acc = acc + zs_ref[slab0 + s, base - d:base - d + n_rows, :] * w_ref[j:j + 1, cs]
        outs.append(acc)
    return jnp.concatenate(outs, axis=1)


def _log_sigmoid(x):
    return jnp.minimum(x, 0.0) - jnp.log1p(jnp.exp(-jnp.abs(x)))


def _rg_gates(zs_ref, cw_ref, cb_ref, wa_ref, ba_ref, wx_ref, bx_ref, lam_ref, a_ref, u_ref):
    ts = a_ref.shape[0]
    xc = _slab_conv(zs_ref, RG_SLAB0, RG_WIDTH // LANES, 0, ts, cw_ref, cb_ref)
    xb = xc.astype(BF16)
    r = jax.nn.sigmoid(jnp.dot(xb, wa_ref[...], preferred_element_type=F32) + ba_ref[...])
    i = jax.nn.sigmoid(jnp.dot(xb, wx_ref[...], preferred_element_type=F32) + bx_ref[...])
    lam = lam_ref[...]
    softplus_neg_lam = jnp.maximum(-lam, 0.0) + jnp.log1p(jnp.exp(-jnp.abs(lam)))
    log_a = (-RG_C) * r * softplus_neg_lam
    a = jnp.exp(log_a)
    a_ref[...] = a
    v = 1.0 - a * a
    u_ref[...] = jnp.where(v > 0.0, v * lax.rsqrt(v), 0.0) * (i * xc)


def _rg_scan(z_ref, y_ref, hc_ref, a_ref, u_ref):
    ts = z_ref.shape[0]
    row = lax.broadcasted_iota(jnp.int32, (SUBLANES, RG_WIDTH), 0)
    hc = hc_ref[...]
    for k in range(ts // SUBLANES):
        rows = slice(k * SUBLANES, (k + 1) * SUBLANES)
        a = a_ref[rows, :]
        u = u_ref[rows, :]
        for d in (1, 2, 4):
            keep = row >= d
            u = u + a * jnp.where(keep, pltpu.roll(u, d, 0), 0.0)
            a = a * jnp.where(keep, pltpu.roll(a, d, 0), 1.0)
        h = u + a * hc
        u_ref[rows, :] = h
        hc = jnp.broadcast_to(h[SUBLANES - 1:SUBLANES, :], (SUBLANES, RG_WIDTH))
    hc_ref[...] = hc
    gate = jax.nn.gelu(z_ref[:, RG_WIDTH:2 * RG_WIDTH], approximate=True)
    y_ref[:, 0:RG_WIDTH] = (u_ref[...] * gate).astype(y_ref.dtype)


def _lane_col(x, h):
    return jnp.broadcast_to(x[:, h:h + 1], x.shape)


def _ml_chunk(rows, z_ref, zs_ref, gt_ref, cw_ref, cb_ref, nw_ref, y_ref, c_ref, n_ref, m_ref):
    lc = ML_CHUNK
    dh = ML_HEAD_DIM
    q0 = 2 * RG_WIDTH
    k0, v0, o0 = q0 + ML_WIDTH, q0 + 2 * ML_WIDTH, q0 + 3 * ML_WIDTH
    row = lax.broadcasted_iota(jnp.int32, (lc, LANES), 0)

    n_sl = ML_WIDTH // LANES
    q = jax.nn.silu(_slab_conv(zs_ref, Q_SLAB0, n_sl, rows.start, lc, cw_ref.at[:, :ML_WIDTH], cb_ref.at[:, :ML_WIDTH]))
    k = jax.nn.silu(_slab_conv(zs_ref, K_SLAB0, n_sl, rows.start, lc, cw_ref.at[:, ML_WIDTH:], cb_ref.at[:, ML_WIDTH:]))
    k = k * (dh ** -0.5)

    gates = gt_ref[rows, :]
    b = _log_sigmoid(gates)
    d = 1
    while d < lc:
        b = b + jnp.where(row >= d, pltpu.roll(b, d, 0), 0.0)
        d *= 2
    b = pltpu.roll(b, LANES - ML_HEADS, 1)
    r = gates - b
    b_last = b[lc - 1:lc, :]
    g_loc = b_last + r
    m_loc = jnp.max(g_loc, axis=0, keepdims=True)
    w_loc = jnp.exp(g_loc - m_loc)
    m_st = m_ref[0:1, :]
    m_new = jnp.maximum(b_last + m_st, m_loc)
    a_prev = jnp.exp(b_last + m_st - m_new)
    a_loc = jnp.exp(m_loc - m_new)
    rmax = r
    d = 1
    while d < lc:
        rmax = jnp.maximum(rmax, jnp.where(row >= d, pltpu.roll(rmax, d, 0), -jnp.inf))
        d *= 2
    e = -jnp.maximum(m_st, rmax)
    a_inter = jnp.exp(m_st + e)
    exp_neg_mt = jnp.exp(e - b)
    r_t = r.T

    heads = [slice(h * dh, (h + 1) * dh) for h in range(ML_HEADS)]
    wide = lambda f: jnp.concatenate([f(h) for h in range(ML_HEADS)], axis=1)
    nt = (((1,), (1,)), ((), ()))
    row_w = lax.broadcasted_iota(jnp.int32, (lc, ML_WIDTH), 0)
    col_w = lax.broadcasted_iota(jnp.int32, (lc, ML_WIDTH), 1)
    causal = row_w >= (col_w % dh)
    qb = q.astype(BF16)
    kb = k.astype(BF16)
    vb = z_ref[rows, v0:o0].astype(BF16)
    c_prev = [c_ref[h] for h in range(ML_HEADS)]
    n_prev = wide(lambda h: n_ref[h:h + 1, :])
    s = wide(lambda h: lax.dot_general(qb[:, heads[h]], kb[:, heads[h]], nt, preferred_element_type=F32))
    w_intra = jnp.where(causal, jnp.exp(wide(lambda h: _lane_col(e, h)) + wide(lambda h: r_t[h:h + 1, :])), 0.0)
    s_qk = s * w_intra
    sb = s_qk.astype(BF16)
    a_in = wide(lambda h: _lane_col(a_inter, h))
    num = (wide(lambda h: jnp.dot(sb[:, heads[h]], vb[:, heads[h]], preferred_element_type=F32))
           + a_in * wide(lambda h: jnp.dot(qb[:, heads[h]], c_prev[h].astype(BF16), preferred_element_type=F32)))
    t = s_qk + a_in * (q * n_prev)
    den = wide(lambda h: jnp.broadcast_to(jnp.sum(t[:, heads[h]], axis=-1, keepdims=True), (lc, dh)))
    hm = num / jnp.maximum(jnp.abs(den), wide(lambda h: _lane_col(exp_neg_mt, h)))
    ms = wide(lambda h: jnp.broadcast_to(jnp.mean(hm[:, heads[h]] * hm[:, heads[h]], axis=-1, keepdims=True), (lc, dh)))
    hm = hm * lax.rsqrt(ms + EPS) * nw_ref[...]
    y = jax.nn.sigmoid(z_ref[rows, o0:o0 + ML_WIDTH]) * hm
    y_ref[rows, RG_WIDTH:RG_WIDTH + ML_WIDTH] = y.astype(y_ref.dtype)

    kw = k * wide(lambda h: _lane_col(w_loc, h))
    for h in range(ML_HEADS):
        c_loc = jnp.dot(kw[:, heads[h]].T.astype(BF16), vb[:, heads[h]], preferred_element_type=F32)
        n_loc = jnp.sum(kw[:, heads[h]], axis=0, keepdims=True)
        ap = a_prev[:, h:h + 1]
        al = a_loc[:, h:h + 1]
        c_ref[h] = ap * c_prev[h] + al * c_loc
        n_ref[h:h + 1, :] = ap * n_ref[h:h + 1, :] + al * n_loc
    m_ref[...] = jnp.broadcast_to(m_new, m_ref.shape)


def _hybrid_kernel(x_ref, xn_ref, nm_ref, win_ref, bin_ref,
                   rcw_ref, rcb_ref, wa_ref, ba_ref, wx_ref, bx_ref, lam_ref,
                   mcw_ref, mcb_ref, mnw_ref, wo_ref, o_ref,
                   za_ref, zb_ref, zsa_ref, zsb_ref, gta_ref, gtb_ref, y_ref, tail_ref, hc_ref, a_ref, u_ref,
                   c_ref, n_ref, m_ref):
    step = pl.program_id(0) * pl.num_programs(1) + pl.program_id(1)

    @pl.when(pl.program_id(1) == 0)
    def _():
        for ref in (tail_ref, hc_ref, c_ref, n_ref):
            ref[...] = jnp.zeros_like(ref)
        m_ref[...] = jnp.full_like(m_ref, -jnp.inf)

    n_main = 2 * RG_WIDTH + 4 * ML_WIDTH
    n_piece = n_main // PROJ_CHUNK + 1
    ts = x_ref.shape[1]
    conv_slab = {0: RG_SLAB0, 2 * RG_WIDTH // PROJ_CHUNK: Q_SLAB0, (2 * RG_WIDTH + ML_WIDTH) // PROJ_CHUNK: K_SLAB0}

    def project(src_ref, z_ref, zs_ref, gt_ref):
        h = _rms(src_ref[0], nm_ref[...]).astype(BF16)

        def piece(c):
            if c < n_main // PROJ_CHUNK:
                cs = slice(c * PROJ_CHUNK, (c + 1) * PROJ_CHUNK)
                res = jnp.dot(h, win_ref[:, cs], preferred_element_type=F32) + bin_ref[:, cs]
                if c in conv_slab:
                    for sl in range(PROJ_CHUNK // LANES):
                        zs_ref[conv_slab[c] + sl, SUBLANES:SUBLANES + ts, :] = res[:, sl * LANES:(sl + 1) * LANES]
                else:
                    z_ref[:, cs] = res
            else:
                gs = slice(n_main, n_main + LANES)
                gt_ref[...] = jnp.dot(h, win_ref[:, gs], preferred_element_type=F32) + bin_ref[:, gs]
        return piece

    @pl.when(step == 0)
    def _():
        piece = project(x_ref, za_ref, zsa_ref, gta_ref)
        for c in range(n_piece):
            piece(c)

    def mix(z_ref, zs_ref, gt_ref, zn_ref, zsn_ref, gtn_ref):
        piece = project(xn_ref, zn_ref, zsn_ref, gtn_ref)
        n_chunk = ts // ML_CHUNK
        zs_ref[:, 0:SUBLANES, :] = tail_ref[...]
        piece(0)
        _rg_gates(zs_ref, rcw_ref, rcb_ref, wa_ref, ba_ref, wx_ref, bx_ref, lam_ref, a_ref, u_ref)
        piece(1)
        piece(2)
        _rg_scan(z_ref, y_ref, hc_ref, a_ref, u_ref)
        nxt = 3
        for c in range(n_chunk):
            if nxt < n_piece:
                piece(nxt)
                nxt += 1
            rows = slice(c * ML_CHUNK, (c + 1) * ML_CHUNK)
            _ml_chunk(rows, z_ref, zs_ref, gt_ref, mcw_ref, mcb_ref, mnw_ref, y_ref, c_ref, n_ref, m_ref)
            o_ref[0, rows, :] = x_ref[0, rows, :] + jnp.dot(y_ref[rows, :], wo_ref[...], preferred_element_type=F32)
        for c in range(nxt, n_piece):
            piece(c)
        tail_ref[...] = zs_ref[:, ts:ts + SUBLANES, :]

    @pl.when(step % 2 == 0)
    def _():
        mix(za_ref, zsa_ref, gta_ref, zb_ref, zsb_ref, gtb_ref)

    @pl.when(step % 2 == 1)
    def _():
        mix(zb_ref, zsb_ref, gtb_ref, za_ref, zsa_ref, gta_ref)


def _next_block_map(bsz, n_blk):
    def index(i, j):
        f = jnp.minimum(i * n_blk + j + 1, bsz * n_blk - 1)
        return f // n_blk, f % n_blk, 0
    return index


def _hybrid_mixer(x, nm, w_in, b_in, rcw, rcb, wa, ba, wx, bx, lam, mcw, mcb, mnw, wo):
    bsz, s, d = x.shape
    ts = min(TS_MIX, s)
    n_main = w_in.shape[1] - LANES
    rw, mw = RG_WIDTH, ML_WIDTH
    consts = (nm, w_in, b_in, rcw, rcb, wa, ba, wx, bx, lam, mcw, mcb, mnw, wo)
    return pl.pallas_call(
        _hybrid_kernel,
        out_shape=jax.ShapeDtypeStruct((bsz, s, d), F32),
        grid=(bsz, s // ts),
        in_specs=[pl.BlockSpec((1, ts, d), lambda i, j: (i, j, 0)),
                  pl.BlockSpec((1, ts, d), _next_block_map(bsz, s // ts))] + [_resident(c.shape) for c in consts],
        out_specs=pl.BlockSpec((1, ts, d), lambda i, j: (i, j, 0)),
        scratch_shapes=[pltpu.VMEM((ts, n_main), F32), pltpu.VMEM((ts, n_main), F32),
                        pltpu.VMEM((N_CONV_SLAB, SUBLANES + ts, LANES), F32),
                        pltpu.VMEM((N_CONV_SLAB, SUBLANES + ts, LANES), F32),
                        pltpu.VMEM((ts, LANES), F32), pltpu.VMEM((ts, LANES), F32),
                        pltpu.VMEM((ts, rw + mw), BF16),
                        pltpu.VMEM((N_CONV_SLAB, SUBLANES, LANES), F32), pltpu.VMEM((SUBLANES, rw), F32),
                        pltpu.VMEM((ts, rw), F32), pltpu.VMEM((ts, rw), F32),
                        pltpu.VMEM((ML_HEADS, ML_HEAD_DIM, ML_HEAD_DIM), F32),
                        pltpu.VMEM((SUBLANES, ML_HEAD_DIM), F32),
                        pltpu.VMEM((SUBLANES, LANES), F32)],
        compiler_params=_params(2),
        name="hybrid_mixer",
    )(x, x, *consts)


def _attn_kernel(x_ref, xn_ref, nm_ref, wqkv_ref, pos_ref, freq_ref, qn_ref, kn_ref, seg_ref, sink_ref, wo_ref,
                 o_ref, qkva_ref, qkvb_ref, qs_ref, kf_ref, vt_ref, att_ref):
    tq = x_ref.shape[1]
    w = WINDOW
    hd = AT_HEAD_DIM
    hh = hd // 2
    n_sub = tq // w
    qw = AT_HEADS * hd
    kw = AT_KV_HEADS * hd
    n_qslab = qw // LANES
    n_kslab = kw // LANES
    n_pack = LANES // hh
    rp = w // n_pack
    first = pl.program_id(1) == 0
    step = pl.program_id(0) * pl.num_programs(1) + pl.program_id(1)

    @pl.when(first)
    def _():
        kf_ref[:, :, 0:w, :] = jnp.zeros((AT_KV_HEADS, 2, w, LANES), kf_ref.dtype)
        vt_ref[:, :, 0:w] = jnp.zeros((AT_KV_HEADS, hd, w), vt_ref.dtype)

    lane = lax.broadcasted_iota(jnp.int32, (w, LANES), 1)
    slot_a = (lane // hh) % 2 == 0
    sin_sign = jnp.where(lane < hd, -1.0, 1.0)
    lane_grp = lax.broadcasted_iota(jnp.int32, (rp, LANES), 1) // hh
    seg = seg_ref[...]
    n_piece = (qw + 2 * kw) // PROJ_CHUNK

    def project(src_ref, dst_ref):
        h = _rms(src_ref[0], nm_ref[...]).astype(BF16)

        def piece(c):
            cs = slice(c * PROJ_CHUNK, (c + 1) * PROJ_CHUNK)
            dst_ref[:, cs] = jnp.dot(h, wqkv_ref[:, cs], preferred_element_type=F32)
        return piece

    @pl.when(step == 0)
    def _():
        piece = project(x_ref, qkva_ref)
        for c in range(n_piece):
            piece(c)

    def norm_rope(t, gain, cos, sin):
        ms = jnp.dot((t * t).astype(BF16), seg, preferred_element_type=F32) * (1.0 / hd)
        tn = t * lax.rsqrt(ms + EPS) * gain
        return tn * cos + pltpu.roll(tn, hd, 1) * sin

    def spread(t):
        parts = []
        for g in range(n_pack):
            z = jnp.where(lane_grp == g, t, 0.0)
            y = z
            for k in range(1, n_pack):
                y = y + pltpu.roll(z, k * hh, 1)
            parts.append(y)
        return jnp.concatenate(parts, axis=0)

    def prepare(i, qkv_ref):
        rows = slice(i * w, (i + 1) * w)
        krows = slice(w + i * w, w + (i + 1) * w)
        pos = pos_ref[0, 0, i:i + 1, :].astype(F32)
        pos_col = jnp.broadcast_to(pos, (w, LANES)).T
        packed = pos_col[0:rp]
        for g in range(1, n_pack):
            packed = jnp.where(lane_grp == g, pos_col[g * rp:(g + 1) * rp], packed)
        ang = packed * freq_ref[...]
        cos = spread(jnp.cos(ang))
        sin = spread(jnp.sin(ang)) * sin_sign
        for sl in range(n_qslab):
            xq = norm_rope(qkv_ref[rows, sl * LANES:(sl + 1) * LANES], qn_ref[...], cos, sin) * (LOG2_E * hd ** -0.5)
            qs_ref[sl, rows, :] = xq.astype(qs_ref.dtype)
        for sl in range(n_kslab):
            xk = norm_rope(qkv_ref[rows, qw + sl * LANES:qw + (sl + 1) * LANES], kn_ref[...], cos, sin)
            kf_ref[2 * sl, 0, krows, :] = jnp.where(slot_a, xk, 0.0).astype(kf_ref.dtype)
            kf_ref[2 * sl, 1, krows, :] = jnp.where(slot_a, 0.0, pltpu.roll(xk, hh, 1)).astype(kf_ref.dtype)
            kf_ref[2 * sl + 1, 0, krows, :] = jnp.where(slot_a, pltpu.roll(xk, LANES - hh, 1), 0.0).astype(kf_ref.dtype)
            kf_ref[2 * sl + 1, 1, krows, :] = jnp.where(slot_a, 0.0, xk).astype(kf_ref.dtype)
            v0 = qw + kw + sl * LANES
            vt = qkv_ref[rows, v0:v0 + LANES].T.astype(vt_ref.dtype)
            vt_ref[2 * sl, :, krows] = vt[:hd]
            vt_ref[2 * sl + 1, :, krows] = vt[hd:]

    grp = AT_HEADS // AT_KV_HEADS
    row_a = lax.broadcasted_iota(jnp.int32, (w, AT_HEADS * w), 0)
    lane_a = lax.broadcasted_iota(jnp.int32, (w, AT_HEADS * w), 1)
    from_prev = row_a > (lane_a % w)
    nt = (((1,), (1,)), ((), ()))

    def attend(i):
        rows = slice(i * w, (i + 1) * w)
        win = slice(i * w, (i + 2) * w)
        parts = []
        for hk in range(AT_KV_HEADS):
            q2 = jnp.concatenate([qs_ref[2 * hk, rows, :], qs_ref[2 * hk + 1, rows, :]], axis=0)
            parts.append(lax.dot_general(kf_ref[hk, 0, win, :], q2, nt, preferred_element_type=F32))
            parts.append(lax.dot_general(kf_ref[hk, 1, win, :], q2, nt, preferred_element_type=F32))
        s = jnp.concatenate(parts, axis=1)
        s_prev = s[:w]
        if i == 0:
            s_prev = s_prev + jnp.where(first, -jnp.inf, 0.0)
        logits = jnp.where(from_prev, s_prev, s[w:])
        sink = sink_ref[...]
        m = jnp.maximum(jnp.max(logits, axis=0, keepdims=True), sink)
        p = jnp.exp2(logits - m)
        inv = 1.0 / (jnp.sum(p, axis=0, keepdims=True) + jnp.exp2(sink - m))
        pp = jnp.concatenate([jnp.where(from_prev, p, 0.0), jnp.where(from_prev, 0.0, p)],
                             axis=0).astype(BF16)
        for hk in range(AT_KV_HEADS):
            cols = slice(hk * grp * w, (hk + 1) * grp * w)
            r = jnp.dot(vt_ref[hk, :, win], pp[:, cols], preferred_element_type=F32) * inv[:, cols]
            for pair in range(grp // 2):
                out_t = jnp.concatenate([r[:, pair * w:(pair + 1) * w],
                                         r[:, (2 + pair) * w:(3 + pair) * w]], axis=0)
                sl = 2 * hk + pair
                att_ref[rows, sl * LANES:(sl + 1) * LANES] = out_t.T.astype(att_ref.dtype)

    def mix(qkv_ref, qkvn_ref):
        piece = project(xn_ref, qkvn_ref)
        piece(0)
        for i in range(n_sub):
            prepare(i, qkv_ref)
        nxt = 1
        for i in range(n_sub):
            attend(i)
            if i % 2 == 1:
                rows = slice((i - 1) * w, (i + 1) * w)
                o_ref[0, rows, :] = x_ref[0, rows, :] + jnp.dot(att_ref[rows, :], wo_ref[...],
                                                                preferred_element_type=F32)
            upto = 1 + ((n_piece - 1) * (i + 1)) // n_sub
            while nxt < upto:
                piece(nxt)
                nxt += 1
        kf_ref[:, :, 0:w, :] = kf_ref[:, :, tq:tq + w, :]
        vt_ref[:, :, 0:w] = vt_ref[:, :, tq:tq + w]

    @pl.when(step % 2 == 0)
    def _():
        mix(qkva_ref, qkvb_ref)

    @pl.when(step % 2 == 1)
    def _():
        mix(qkvb_ref, qkva_ref)


def _slab_lane_dims():
    hh = AT_HEAD_DIM // 2
    lane = np.arange(LANES)
    return (lane // hh) % 2, lane % hh + hh * (lane // AT_HEAD_DIM)


def _attn_mixer(x, nm, w_qkv, positions, q_norm, k_norm, sinks, wo):
    bsz, s, d = x.shape
    tq = min(TS_MIX, s)
    hd = AT_HEAD_DIM
    qw, kw = AT_HEADS * hd, AT_KV_HEADS * hd
    grp = AT_HEADS // AT_KV_HEADS
    inv_freq = ROPE_THETA ** (-jnp.arange(hd // 2, dtype=F32) * (2.0 / hd))
    freq = jnp.tile(inv_freq, LANES // (hd // 2))[None, :]
    head_in_slab, dim = _slab_lane_dims()
    seg = jnp.asarray((head_in_slab[:, None] == head_in_slab[None, :]).astype(np.float32), dtype=BF16)
    n_slab = (qw + kw) // LANES
    w_qk = w_qkv[:, :qw + kw].reshape(d, n_slab, 2, 2, hd // 2).transpose(0, 1, 3, 2, 4).reshape(d, qw + kw)
    w_qkv = jnp.concatenate([w_qk, w_qkv[:, qw + kw:]], axis=1)
    qn = q_norm.astype(F32)[dim][None, :]
    kn = k_norm.astype(F32)[dim][None, :]
    pos = positions.reshape(bsz, s // tq, tq // WINDOW, WINDOW)
    sk = (sinks.astype(F32) * LOG2_E).reshape(AT_KV_HEADS, grp)[:, np.array([0, 2, 1, 3])]
    sk = jnp.repeat(sk, WINDOW, axis=1).reshape(1, AT_HEADS * WINDOW)
    consts_a = (nm, w_qkv)
    consts_b = (freq, qn, kn, seg, sk, wo)
    return pl.pallas_call(
        _attn_kernel,
        out_shape=jax.ShapeDtypeStruct((bsz, s, d), F32),
        grid=(bsz, s // tq),
        in_specs=([pl.BlockSpec((1, tq, d), lambda i, j: (i, j, 0)),
                   pl.BlockSpec((1, tq, d), _next_block_map(bsz, s // tq))] + [_resident(c.shape) for c in consts_a]
                  + [pl.BlockSpec((1, 1, tq // WINDOW, WINDOW), lambda i, j: (i, j, 0, 0))]
                  + [_resident(c.shape) for c in consts_b]),
        out_specs=pl.BlockSpec((1, tq, d), lambda i, j: (i, j, 0)),
        scratch_shapes=[pltpu.VMEM((tq, qw + 2 * kw), F32), pltpu.VMEM((tq, qw + 2 * kw), F32),
                        pltpu.VMEM((qw // LANES, tq, LANES), BF16),
                        pltpu.VMEM((AT_KV_HEADS, 2, tq + WINDOW, LANES), BF16),
                        pltpu.VMEM((AT_KV_HEADS, hd, tq + WINDOW), BF16),
                        pltpu.VMEM((tq, qw), BF16)],
        compiler_params=_params(2),
        name="attn_mixer",
    )(x, x, *consts_a, pos, *consts_b)


def _post_kernel(x_ref, p_ref, nf_ref, wu_ref, cw_ref, cb_ref, wd_ref, np_ref, wg_ref, wp_ref, o_ref,
                 gtail_ref, act_ref):
    tm = x_ref.shape[1]
    ff = wd_ref.shape[0]

    @pl.when(pl.program_id(1) == 0)
    def _():
        gtail_ref[...] = jnp.zeros_like(gtail_ref)

    x1 = x_ref[0]
    h = _rms(x1, nf_ref[...]).astype(BF16)
    for c in range(ff // FF_CHUNK):
        cs = slice(c * FF_CHUNK, (c + 1) * FF_CHUNK)
        g = jnp.dot(h, wu_ref[:, cs], preferred_element_type=F32)
        u = jnp.dot(h, wu_ref[:, ff + c * FF_CHUNK:ff + (c + 1) * FF_CHUNK], preferred_element_type=F32)
        gc = _causal_conv(g, gtail_ref[:, cs], cw_ref.at[:, cs], cb_ref.at[:, cs])
        gtail_ref[:, cs] = g[tm - SUBLANES:]
        act_ref[:, cs] = (jax.nn.gelu(gc, approximate=True) * u).astype(act_ref.dtype)
    x2 = x1 + jnp.dot(act_ref[...], wd_ref[...], preferred_element_type=F32)
    gate = jax.nn.sigmoid(jnp.dot(_rms(x2, np_ref[...]).astype(BF16), wg_ref[...], preferred_element_type=F32))
    pe = jnp.dot(p_ref[0, 0].astype(BF16), wp_ref[...], preferred_element_type=F32)
    o_ref[0] = x2 + gate * pe


def _post_mixer(x, p, layer, nf, wu, cw, cb, wd, npl, wg, wp):
    bsz, s, d = x.shape
    tm = min(TM_POST, s)
    ff = wd.shape[1]
    stacked = (nf, wu, cw, cb, wd, npl, wg, wp)
    return pl.pallas_call(
        _post_kernel,
        out_shape=jax.ShapeDtypeStruct((bsz, s, d), F32),
        grid=(bsz, s // tm),
        in_specs=[pl.BlockSpec((1, tm, d), lambda i, j: (i, j, 0)),
                  pl.BlockSpec((1, 1, tm, p.shape[-1]), lambda i, j: (layer, i, j, 0))]
                 + [_layer_block(a, layer) for a in stacked],
        out_specs=pl.BlockSpec((1, tm, d), lambda i, j: (i, j, 0)),
        scratch_shapes=[pltpu.VMEM((SUBLANES, ff), F32), pltpu.VMEM((tm, ff), BF16)],
        compiler_params=_params(2),
        name=f"post_mixer_{layer}",
    )(x, p, *stacked)


def _block_diag(w):
    n, r, _ = w.shape
    eye = jnp.eye(n, dtype=w.dtype)
    return (eye[:, None, :, None] * w[:, :, None, :]).reshape(n * r, n * r)


def kernel(x, p, positions, norm_mix, norm_ffn, norm_ple, hy_w_in, hy_b_in, rg_conv_w, rg_conv_b, rg_w_a, rg_b_a, rg_w_x, rg_b_x, rg_lambda, ml_conv_w, ml_conv_b, ml_norm, hy_w_out, at_w_qkv, at_q_norm, at_k_norm, at_sinks, at_w_out, ff_w_up, ff_conv_w, ff_conv_b, ff_w_down, ple_w_gate, ple_w_proj):
    depth = p.shape[0]
    row = lambda v: v.astype(F32)[None, :]
    post = (norm_ffn.astype(F32)[:, None, :], ff_w_up.astype(BF16), ff_conv_w.astype(F32),
            ff_conv_b.astype(F32)[:, None, :], ff_w_down.astype(BF16), norm_ple.astype(F32)[:, None, :],
            ple_w_gate.astype(BF16), ple_w_proj.astype(BF16))
    for layer in range(depth):
        if layer % 2 == 0:
            e = layer // 2
            pad = LANES - 2 * ML_HEADS
            x = _hybrid_mixer(x, row(norm_mix[layer]), jnp.pad(hy_w_in[e], ((0, 0), (0, pad))).astype(BF16),
                              row(jnp.pad(hy_b_in[e], (0, pad))),
                              rg_conv_w[e], row(rg_conv_b[e]), _block_diag(rg_w_a[e]).astype(BF16), row(rg_b_a[e]),
                              _block_diag(rg_w_x[e]).astype(BF16), row(rg_b_x[e]), row(rg_lambda[e]),
                              ml_conv_w[e], row(ml_conv_b[e]), row(ml_norm[e]), hy_w_out[e].astype(BF16))
        else:
            o = layer // 2
            x = _attn_mixer(x, row(norm_mix[layer]), at_w_qkv[o].astype(BF16), positions, at_q_norm[o],
                            at_k_norm[o], at_sinks[o], at_w_out[o].astype(BF16))
        x = _post_mixer(x, p, layer, *post)
    return x
```

```python
import jax
import jax.numpy as jnp
import numpy as np
from jax import lax
from jax.experimental import pallas as pl
from jax.experimental.pallas import tpu as pltpu

F32 = jnp.float32
BF16 = jnp.bfloat16

D_MODEL = 1024
PLE_DIM = 256
RG_WIDTH = 512
RG_BLOCKS = 8
RG_C = 8.0
ML_HEADS = 4
ML_HEAD_DIM = 128
ML_WIDTH = ML_HEADS * ML_HEAD_DIM
ML_CHUNK = 128
AT_HEADS = 16
AT_KV_HEADS = 4
AT_HEAD_DIM = 64
WINDOW = 128
ROPE_THETA = 10000.0
FF_DIM = 3 * D_MODEL
EPS = 1e-6
LOG2_E = 1.4426950408889634

LANES = 128
SUBLANES = 8
VMEM_LIMIT_BYTES = 56 * 1024 * 1024

TS_MIX = 512
TM_POST = 512
PROJ_CHUNK = 512
RG_SLAB0 = 0
Q_SLAB0 = RG_WIDTH // LANES
K_SLAB0 = Q_SLAB0 + ML_WIDTH // LANES
N_CONV_SLAB = K_SLAB0 + ML_WIDTH // LANES


def _params(n_axes):
    return pltpu.CompilerParams(dimension_semantics=("arbitrary",) * n_axes,
                                vmem_limit_bytes=VMEM_LIMIT_BYTES)


def _resident(shape, index=None):
    n = len(shape)
    idx = (0,) * n if index is None else tuple(index)
    return pl.BlockSpec(shape, lambda *_: idx, pipeline_mode=pl.Buffered(1))


def _layer_block(arr, layer):
    shape = (None,) + arr.shape[1:]
    return _resident(shape, (layer,) + (0,) * (arr.ndim - 1))


def _rms(x, g):
    return x * lax.rsqrt(jnp.mean(x * x, axis=-1, keepdims=True) + EPS) * g


def _slab_conv(zs_ref, slab0, n_slab, r0, n_rows, w_ref, b_ref):
    k_w = w_ref.shape[0]
    base = SUBLANES + r0
    outs = []
    for s in range(n_slab):
        cs = slice(s * LANES, (s + 1) * LANES)
        acc = zs_ref[slab0 + s, base:base + n_rows, :] * w_ref[k_w - 1:k_w, cs] + b_ref[:, cs]
        for j in range(k_w - 1):
            d = k_w - 1 - j
            acc = acc + zs_ref[slab0 + s, base - d:base - d + n_rows, :] * w_ref[j:j + 1, cs]
        outs.append(acc)
    return jnp.concatenate(outs, axis=1)


def _log_sigmoid(x):
    return jnp.minimum(x, 0.0) - jnp.log1p(jnp.exp(-jnp.abs(x)))


def _rg_gates(zs_ref, cw_ref, cb_ref, wa_ref, ba_ref, wx_ref, bx_ref, lam_ref, a_ref, u_ref):
    ts = a_ref.shape[0]
    xc = _slab_conv(zs_ref, RG_SLAB0, RG_WIDTH // LANES, 0, ts, cw_ref, cb_ref)
    xb = xc.astype(BF16)
    r = jax.nn.sigmoid(jnp.dot(xb, wa_ref[...], preferred_element_type=F32) + ba_ref[...])
    i = jax.nn.sigmoid(jnp.dot(xb, wx_ref[...], preferred_element_type=F32) + bx_ref[...])
    lam = lam_ref[...]
    softplus_neg_lam = jnp.maximum(-lam, 0.0) + jnp.log1p(jnp.exp(-jnp.abs(lam)))
    log_a = (-RG_C) * r * softplus_neg_lam
    a = jnp.exp(log_a)
    a_ref[...] = a
    v = 1.0 - a * a
    u_ref[...] = jnp.where(v > 0.0, v * lax.rsqrt(v), 0.0) * (i * xc)


def _rg_scan(z_ref, y_ref, hc_ref, a_ref, u_ref):
    ts = z_ref.shape[0]
    row = lax.broadcasted_iota(jnp.int32, (SUBLANES, RG_WIDTH), 0)
    hc = hc_ref[...]
    for k in range(ts // SUBLANES):
        rows = slice(k * SUBLANES, (k + 1) * SUBLANES)
        a = a_ref[rows, :]
        u = u_ref[rows, :]
        for d in (1, 2, 4):
            keep = row >= d
            u = u + a * jnp.where(keep, pltpu.roll(u, d, 0), 0.0)
            a = a * jnp.where(keep, pltpu.roll(a, d, 0), 1.0)
        h = u + a * hc
        u_ref[rows, :] = h
        hc = jnp.broadcast_to(h[SUBLANES - 1:SUBLANES, :], (SUBLANES, RG_WIDTH))
    hc_ref[...] = hc
    gate = jax.nn.gelu(z_ref[:, RG_WIDTH:2 * RG_WIDTH], approximate=True)
    y_ref[:, 0:RG_WIDTH] = (u_ref[...] * gate).astype(y_ref.dtype)


def _lane_col(x, h):
    return jnp.broadcast_to(x[:, h:h + 1], x.shape)


def _ml_chunk(rows, z_ref, zs_ref, gt_ref, cw_ref, cb_ref, nw_ref, y_ref, c_ref, n_ref, m_ref):
    lc = ML_CHUNK
    dh = ML_HEAD_DIM
    q0 = 2 * RG_WIDTH
    k0, v0, o0 = q0 + ML_WIDTH, q0 + 2 * ML_WIDTH, q0 + 3 * ML_WIDTH
    row = lax.broadcasted_iota(jnp.int32, (lc, LANES), 0)

    n_sl = ML_WIDTH // LANES
    q = jax.nn.silu(_slab_conv(zs_ref, Q_SLAB0, n_sl, rows.start, lc, cw_ref.at[:, :ML_WIDTH], cb_ref.at[:, :ML_WIDTH]))
    k = jax.nn.silu(_slab_conv(zs_ref, K_SLAB0, n_sl, rows.start, lc, cw_ref.at[:, ML_WIDTH:], cb_ref.at[:, ML_WIDTH:]))
    k = k * (dh ** -0.5)

    gates = gt_ref[rows, :]
    b = _log_sigmoid(gates)
    d = 1
    while d < lc:
        b = b + jnp.where(row >= d, pltpu.roll(b, d, 0), 0.0)
        d *= 2
    b = pltpu.roll(b, LANES - ML_HEADS, 1)
    r = gates - b
    b_last = b[lc - 1:lc, :]
    g_loc = b_last + r
    m_loc = jnp.max(g_loc, axis=0, keepdims=True)
    w_loc = jnp.exp(g_loc - m_loc)
    m_st = m_ref[0:1, :]
    m_new = jnp.maximum(b_last + m_st, m_loc)
    a_prev = jnp.exp(b_last + m_st - m_new)
    a_loc = jnp.exp(m_loc - m_new)
    rmax = r
    d = 1
    while d < lc:
        rmax = jnp.maximum(rmax, jnp.where(row >= d, pltpu.roll(rmax, d, 0), -jnp.inf))
        d *= 2
    e = -jnp.maximum(m_st, rmax)
    a_inter = jnp.exp(m_st + e)
    exp_neg_mt = jnp.exp(e - b)
    r_t = r.T

    heads = [slice(h * dh, (h + 1) * dh) for h in range(ML_HEADS)]
    wide = lambda f: jnp.concatenate([f(h) for h in range(ML_HEADS)], axis=1)
    nt = (((1,), (1,)), ((), ()))
    row_w = lax.broadcasted_iota(jnp.int32, (lc, ML_WIDTH), 0)
    col_w = lax.broadcasted_iota(jnp.int32, (lc, ML_WIDTH), 1)
    causal = row_w >= (col_w % dh)
    qb = q.astype(BF16)
    kb = k.astype(BF16)
    vb = z_ref[rows, v0:o0].astype(BF16)
    c_prev = [c_ref[h] for h in range(ML_HEADS)]
    n_prev = wide(lambda h: n_ref[h:h + 1, :])
    s = wide(lambda h: lax.dot_general(qb[:, heads[h]], kb[:, heads[h]], nt, preferred_element_type=F32))
    w_intra = jnp.where(causal, jnp.exp(wide(lambda h: _lane_col(e, h)) + wide(lambda h: r_t[h:h + 1, :])), 0.0)
    s_qk = s * w_intra
    sb = s_qk.astype(BF16)
    a_in = wide(lambda h: _lane_col(a_inter, h))
    num = (wide(lambda h: jnp.dot(sb[:, heads[h]], vb[:, heads[h]], preferred_element_type=F32))
           + a_in * wide(lambda h: jnp.dot(qb[:, heads[h]], c_prev[h].astype(BF16), preferred_element_type=F32)))
    t = s_qk + a_in * (q * n_prev)
    den = wide(lambda h: jnp.broadcast_to(jnp.sum(t[:, heads[h]], axis=-1, keepdims=True), (lc, dh)))
    hm = num / jnp.maximum(jnp.abs(den), wide(lambda h: _lane_col(exp_neg_mt, h)))
    ms = wide(lambda h: jnp.broadcast_to(jnp.mean(hm[:, heads[h]] * hm[:, heads[h]], axis=-1, keepdims=True), (lc, dh)))
    hm = hm * lax.rsqrt(ms + EPS) * nw_ref[...]
    y = jax.nn.sigmoid(z_ref[rows, o0:o0 + ML_WIDTH]) * hm
    y_ref[rows, RG_WIDTH:RG_WIDTH + ML_WIDTH] = y.astype(y_ref.dtype)

    kw = k * wide(lambda h: _lane_col(w_loc, h))
    for h in range(ML_HEADS):
        c_loc = jnp.dot(kw[:, heads[h]].T.astype(BF16), vb[:, heads[h]], preferred_element_type=F32)
        n_loc = jnp.sum(kw[:, heads[h]], axis=0, keepdims=True)
        ap = a_prev[:, h:h + 1]
        al = a_loc[:, h:h + 1]
        c_ref[h] = ap * c_prev[h] + al * c_loc
        n_ref[h:h + 1, :] = ap * n_ref[h:h + 1, :] + al * n_loc
    m_ref[...] = jnp.broadcast_to(m_new, m_ref.shape)


def _hybrid_kernel(x_ref, xn_ref, nm_ref, win_ref, bin_ref,
                   rcw_ref, rcb_ref, wa_ref, ba_ref, wx_ref, bx_ref, lam_ref,
                   mcw_ref, mcb_ref, mnw_ref, wo_ref, o_ref,
                   za_ref, zb_ref, zsa_ref, zsb_ref, gta_ref, gtb_ref, y_ref, tail_ref, hc_ref, a_ref, u_ref,
                   c_ref, n_ref, m_ref):
    step = pl.program_id(0) * pl.num_programs(1) + pl.program_id(1)

    @pl.when(pl.program_id(1) == 0)
    def _():
        for ref in (tail_ref, hc_ref, c_ref, n_ref):
            ref[...] = jnp.zeros_like(ref)
        m_ref[...] = jnp.full_like(m_ref, -jnp.inf)

    n_main = 2 * RG_WIDTH + 4 * ML_WIDTH
    n_piece = n_main // PROJ_CHUNK + 1
    ts = x_ref.shape[1]
    conv_slab = {0: RG_SLAB0, 2 * RG_WIDTH // PROJ_CHUNK: Q_SLAB0, (2 * RG_WIDTH + ML_WIDTH) // PROJ_CHUNK: K_SLAB0}

    def project(src_ref, z_ref, zs_ref, gt_ref):
        h = _rms(src_ref[0], nm_ref[...]).astype(BF16)

        def piece(c):
            if c < n_main // PROJ_CHUNK:
                cs = slice(c * PROJ_CHUNK, (c + 1) * PROJ_CHUNK)
                res = jnp.dot(h, win_ref[:, cs], preferred_element_type=F32) + bin_ref[:, cs]
                if c in conv_slab:
                    for sl in range(PROJ_CHUNK // LANES):
                        zs_ref[conv_slab[c] + sl, SUBLANES:SUBLANES + ts, :] = res[:, sl * LANES:(sl + 1) * LANES]
                else:
                    z_ref[:, cs] = res
            else:
                gs = slice(n_main, n_main + LANES)
                gt_ref[...] = jnp.dot(h, win_ref[:, gs], preferred_element_type=F32) + bin_ref[:, gs]
        return piece

    @pl.when(step == 0)
    def _():
        piece = project(x_ref, za_ref, zsa_ref, gta_ref)
        for c in range(n_piece):
            piece(c)

    def mix(z_ref, zs_ref, gt_ref, zn_ref, zsn_ref, gtn_ref):
        piece = project(xn_ref, zn_ref, zsn_ref, gtn_ref)
        n_chunk = ts // ML_CHUNK
        zs_ref[:, 0:SUBLANES, :] = tail_ref[...]
        piece(0)
        _rg_gates(zs_ref, rcw_ref, rcb_ref, wa_ref, ba_ref, wx_ref, bx_ref, lam_ref, a_ref, u_ref)
        piece(1)
        piece(2)
        _rg_scan(z_ref, y_ref, hc_ref, a_ref, u_ref)
        nxt = 3
        for c in range(n_chunk):
            if nxt < n_piece:
                piece(nxt)
                nxt += 1
            rows = slice(c * ML_CHUNK, (c + 1) * ML_CHUNK)
            _ml_chunk(rows, z_ref, zs_ref, gt_ref, mcw_ref, mcb_ref, mnw_ref, y_ref, c_ref, n_ref, m_ref)
            o_ref[0, rows, :] = x_ref[0, rows, :] + jnp.dot(y_ref[rows, :], wo_ref[...], preferred_element_type=F32)
        for c in range(nxt, n_piece):
            piece(c)
        tail_ref[...] = zs_ref[:, ts:ts + SUBLANES, :]

    @pl.when(step % 2 == 0)
    def _():
        mix(za_ref, zsa_ref, gta_ref, zb_ref, zsb_ref, gtb_ref)

    @pl.when(step % 2 == 1)
    def _():
        mix(zb_ref, zsb_ref, gtb_ref, za_ref, zsa_ref, gta_ref)


def _next_block_map(bsz, n_blk):
    def index(i, j):
        f = jnp.minimum(i * n_blk + j + 1, bsz * n_blk - 1)
        return f // n_blk, f % n_blk, 0
    return index


def _hybrid_mixer(x, nm, w_in, b_in, rcw, rcb, wa, ba, wx, bx, lam, mcw, mcb, mnw, wo):
    bsz, s, d = x.shape
    ts = min(TS_MIX, s)
    n_main = w_in.shape[1] - LANES
    rw, mw = RG_WIDTH, ML_WIDTH
    consts = (nm, w_in, b_in, rcw, rcb, wa, ba, wx, bx, lam, mcw, mcb, mnw, wo)
    return pl.pallas_call(
        _hybrid_kernel,
        out_shape=jax.ShapeDtypeStruct((bsz, s, d), F32),
        grid=(bsz, s // ts),
        in_specs=[pl.BlockSpec((1, ts, d), lambda i, j: (i, j, 0)),
                  pl.BlockSpec((1, ts, d), _next_block_map(bsz, s // ts))] + [_resident(c.shape) for c in consts],
        out_specs=pl.BlockSpec((1, ts, d), lambda i, j: (i, j, 0)),
        scratch_shapes=[pltpu.VMEM((ts, n_main), F32), pltpu.VMEM((ts, n_main), F32),
                        pltpu.VMEM((N_CONV_SLAB, SUBLANES + ts, LANES), F32),
                        pltpu.VMEM((N_CONV_SLAB, SUBLANES + ts, LANES), F32),
                        pltpu.VMEM((ts, LANES), F32), pltpu.VMEM((ts, LANES), F32),
                        pltpu.VMEM((ts, rw + mw), BF16),
                        pltpu.VMEM((N_CONV_SLAB, SUBLANES, LANES), F32), pltpu.VMEM((SUBLANES, rw), F32),
                        pltpu.VMEM((ts, rw), F32), pltpu.VMEM((ts, rw), F32),
                        pltpu.VMEM((ML_HEADS, ML_HEAD_DIM, ML_HEAD_DIM), F32),
                        pltpu.VMEM((SUBLANES, ML_HEAD_DIM), F32),
                        pltpu.VMEM((SUBLANES, LANES), F32)],
        compiler_params=_params(2),
        name="hybrid_mixer",
    )(x, x, *consts)


def _attn_kernel(x_ref, xn_ref, nm_ref, wqkv_ref, pos_ref, freq_ref, qn_ref, kn_ref, seg_ref, sink_ref, wo_ref,
                 o_ref, qkva_ref, qkvb_ref, qs_ref, kf_ref, vt_ref, att_ref):
    tq = x_ref.shape[1]
    w = WINDOW
    hd = AT_HEAD_DIM
    hh = hd // 2
    n_sub = tq // w
    qw = AT_HEADS * hd
    kw = AT_KV_HEADS * hd
    n_qslab = qw // LANES
    n_kslab = kw // LANES
    n_pack = LANES // hh
    rp = w // n_pack
    first = pl.program_id(1) == 0
    step = pl.program_id(0) * pl.num_programs(1) + pl.program_id(1)

    @pl.when(first)
    def _():
        kf_ref[:, :, 0:w, :] = jnp.zeros((AT_KV_HEADS, 2, w, LANES), kf_ref.dtype)
        vt_ref[:, :, 0:w] = jnp.zeros((AT_KV_HEADS, hd, w), vt_ref.dtype)

    lane = lax.broadcasted_iota(jnp.int32, (w, LANES), 1)
    slot_a = (lane // hh) % 2 == 0
    sin_sign = jnp.where(lane < hd, -1.0, 1.0)
    lane_grp = lax.broadcasted_iota(jnp.int32, (rp, LANES), 1) // hh
    seg = seg_ref[...]
    n_piece = (qw + 2 * kw) // PROJ_CHUNK

    def project(src_ref, dst_ref):
        h = _rms(src_ref[0], nm_ref[...]).astype(BF16)

        def piece(c):
            cs = slice(c * PROJ_CHUNK, (c + 1) * PROJ_CHUNK)
            dst_ref[:, cs] = jnp.dot(h, wqkv_ref[:, cs], preferred_element_type=F32)
        return piece

    @pl.when(step == 0)
    def _():
        piece = project(x_ref, qkva_ref)
        for c in range(n_piece):
            piece(c)

    def norm_rope(t, gain, cos, sin):
        ms = jnp.dot((t * t).astype(BF16), seg, preferred_element_type=F32) * (1.0 / hd)
        tn = t * lax.rsqrt(ms + EPS) * gain
        return tn * cos + pltpu.roll(tn, hd, 1) * sin

    def spread(t):
        parts = []
        for g in range(n_pack):
            z = jnp.where(lane_grp == g, t, 0.0)
            y = z
            for k in range(1, n_pack):
                y = y + pltpu.roll(z, k * hh, 1)
            parts.append(y)
        return jnp.concatenate(parts, axis=0)

    def prepare(i, qkv_ref):
        rows = slice(i * w, (i + 1) * w)
        krows = slice(w + i * w, w + (i + 1) * w)
        pos = pos_ref[0, 0, i:i + 1, :].astype(F32)
        pos_col = jnp.broadcast_to(pos, (w, LANES)).T
        packed = pos_col[0:rp]
        for g in range(1, n_pack):
            packed = jnp.where(lane_grp == g, pos_col[g * rp:(g + 1) * rp], packed)
        ang = packed * freq_ref[...]
        cos = spread(jnp.cos(ang))
        sin = spread(jnp.sin(ang)) * sin_sign
        for sl in range(n_qslab):
            xq = norm_rope(qkv_ref[rows, sl * LANES:(sl + 1) * LANES], qn_ref[...], cos, sin) * (LOG2_E * hd ** -0.5)
            qs_ref[sl, rows, :] = xq.astype(qs_ref.dtype)
        for sl in range(n_kslab):
            xk = norm_rope(qkv_ref[rows, qw + sl * LANES:qw + (sl + 1) * LANES], kn_ref[...], cos, sin)
            kf_ref[2 * sl, 0, krows, :] = jnp.where(slot_a, xk, 0.0).astype(kf_ref.dtype)
            kf_ref[2 * sl, 1, krows, :] = jnp.where(slot_a, 0.0, pltpu.roll(xk, hh, 1)).astype(kf_ref.dtype)
            kf_ref[2 * sl + 1, 0, krows, :] = jnp.where(slot_a, pltpu.roll(xk, LANES - hh, 1), 0.0).astype(kf_ref.dtype)
            kf_ref[2 * sl + 1, 1, krows, :] = jnp.where(slot_a, 0.0, xk).astype(kf_ref.dtype)
            v0 = qw + kw + sl * LANES
            vt = qkv_ref[rows, v0:v0 + LANES].T.astype(vt_ref.dtype)
            vt_ref[2 * sl, :, krows] = vt[:hd]
            vt_ref[2 * sl + 1, :, krows] = vt[hd:]

    grp = AT_HEADS // AT_KV_HEADS
    row_a = lax.broadcasted_iota(jnp.int32, (w, AT_HEADS * w), 0)
    lane_a = lax.broadcasted_iota(jnp.int32, (w, AT_HEADS * w), 1)
    from_prev = row_a > (lane_a % w)
    nt = (((1,), (1,)), ((), ()))

    def attend(i):
        rows = slice(i * w, (i + 1) * w)
        win = slice(i * w, (i + 2) * w)
        parts = []
        for hk in range(AT_KV_HEADS):
            q2 = jnp.concatenate([qs_ref[2 * hk, rows, :], qs_ref[2 * hk + 1, rows, :]], axis=0)
            parts.append(lax.dot_general(kf_ref[hk, 0, win, :], q2, nt, preferred_element_type=F32))
            parts.append(lax.dot_general(kf_ref[hk, 1, win, :], q2, nt, preferred_element_type=F32))
        s = jnp.concatenate(parts, axis=1)
        s_prev = s[:w]
        if i == 0:
            s_prev = s_prev + jnp.where(first, -jnp.inf, 0.0)
        logits = jnp.where(from_prev, s_prev, s[w:])
        sink = sink_ref[...]
        m = jnp.maximum(jnp.max(logits, axis=0, keepdims=True), sink)
        p = jnp.exp2(logits - m)
        inv = 1.0 / (jnp.sum(p, axis=0, keepdims=True) + jnp.exp2(sink - m))
        pp = jnp.concatenate([jnp.where(from_prev, p, 0.0), jnp.where(from_prev, 0.0, p)],
                             axis=0).astype(BF16)
        for hk in range(AT_KV_HEADS):
            cols = slice(hk * grp * w, (hk + 1) * grp * w)
            r = jnp.dot(vt_ref[hk, :, win], pp[:, cols], preferred_element_type=F32) * inv[:, cols]
            for pair in range(grp // 2):
                out_t = jnp.concatenate([r[:, pair * w:(pair + 1) * w],
                                         r[:, (2 + pair) * w:(3 + pair) * w]], axis=0)
                sl = 2 * hk + pair
                att_ref[rows, sl * LANES:(sl + 1) * LANES] = out_t.T.astype(att_ref.dtype)

    def mix(qkv_ref, qkvn_ref):
        piece = project(xn_ref, qkvn_ref)
        piece(0)
        for i in range(n_sub):
            prepare(i, qkv_ref)
        nxt = 1
        for i in range(n_sub):
            attend(i)
            if i % 2 == 1:
                rows = slice((i - 1) * w, (i + 1) * w)
                o_ref[0, rows, :] = x_ref[0, rows, :] + jnp.dot(att_ref[rows, :], wo_ref[...],
                                                                preferred_element_type=F32)
            upto = 1 + ((n_piece - 1) * (i + 1)) // n_sub
            while nxt < upto:
                piece(nxt)
                nxt += 1
        kf_ref[:, :, 0:w, :] = kf_ref[:, :, tq:tq + w, :]
        vt_ref[:, :, 0:w] = vt_ref[:, :, tq:tq + w]

    @pl.when(step % 2 == 0)
    def _():
        mix(qkva_ref, qkvb_ref)

    @pl.when(step % 2 == 1)
    def _():
        mix(qkvb_ref, qkva_ref)


def _slab_lane_dims():
    hh = AT_HEAD_DIM // 2
    lane = np.arange(LANES)
    return (lane // hh) % 2, lane % hh + hh * (lane // AT_HEAD_DIM)


def _attn_mixer(x, nm, w_qkv, positions, q_norm, k_norm, sinks, wo):
    bsz, s, d = x.shape
    tq = min(TS_MIX, s)
    hd = AT_HEAD_DIM
    qw, kw = AT_HEADS * hd, AT_KV_HEADS * hd
    grp = AT_HEADS // AT_KV_HEADS
    inv_freq = ROPE_THETA ** (-jnp.arange(hd // 2, dtype=F32) * (2.0 / hd))
    freq = jnp.tile(inv_freq, LANES // (hd // 2))[None, :]
    head_in_slab, dim = _slab_lane_dims()
    seg = jnp.asarray((head_in_slab[:, None] == head_in_slab[None, :]).astype(np.float32), dtype=BF16)
    n_slab = (qw + kw) // LANES
    w_qk = w_qkv[:, :qw + kw].reshape(d, n_slab, 2, 2, hd // 2).transpose(0, 1, 3, 2, 4).reshape(d, qw + kw)
    w_qkv = jnp.concatenate([w_qk, w_qkv[:, qw + kw:]], axis=1)
    qn = q_norm.astype(F32)[dim][None, :]
    kn = k_norm.astype(F32)[dim][None, :]
    pos = positions.reshape(bsz, s // tq, tq // WINDOW, WINDOW)
    sk = (sinks.astype(F32) * LOG2_E).reshape(AT_KV_HEADS, grp)[:, np.array([0, 2, 1, 3])]
    sk = jnp.repeat(sk, WINDOW, axis=1).reshape(1, AT_HEADS * WINDOW)
    consts_a = (nm, w_qkv)
    consts_b = (freq, qn, kn, seg, sk, wo)
    return pl.pallas_call(
        _attn_kernel,
        out_shape=jax.ShapeDtypeStruct((bsz, s, d), F32),
        grid=(bsz, s // tq),
        in_specs=([pl.BlockSpec((1, tq, d), lambda i, j: (i, j, 0)),
                   pl.BlockSpec((1, tq, d), _next_block_map(bsz, s // tq))] + [_resident(c.shape) for c in consts_a]
                  + [pl.BlockSpec((1, 1, tq // WINDOW, WINDOW), lambda i, j: (i, j, 0, 0))]
                  + [_resident(c.shape) for c in consts_b]),
        out_specs=pl.BlockSpec((1, tq, d), lambda i, j: (i, j, 0)),
        scratch_shapes=[pltpu.VMEM((tq, qw + 2 * kw), F32), pltpu.VMEM((tq, qw + 2 * kw), F32),
                        pltpu.VMEM((qw // LANES, tq, LANES), BF16),
                        pltpu.VMEM((AT_KV_HEADS, 2, tq + WINDOW, LANES), BF16),
                        pltpu.VMEM((AT_KV_HEADS, hd, tq + WINDOW), BF16),
                        pltpu.VMEM((tq, qw), BF16)],
        compiler_params=_params(2),
        name="attn_mixer",
    )(x, x, *consts_a, pos, *consts_b)


def _post_kernel(x_ref, p_ref, nf_ref, wu_ref, cw_ref, cb_ref, wd_ref, np_ref, wg_ref, wp_ref, o_ref,
                 gs_ref, gtail_ref, act_ref):
    tm = x_ref.shape[1]
    ff = wd_ref.shape[0]
    n_slab = ff // LANES

    @pl.when(pl.program_id(1) == 0)
    def _():
        gtail_ref[...] = jnp.zeros_like(gtail_ref)

    x1 = x_ref[0]
    h = _rms(x1, nf_ref[...]).astype(BF16)
    gs_ref[:, 0:SUBLANES, :] = gtail_ref[...]
    g = jnp.dot(h, wu_ref[:, 0:ff], preferred_element_type=F32)
    for sl in range(n_slab):
        gs_ref[sl, SUBLANES:SUBLANES + tm, :] = g[:, sl * LANES:(sl + 1) * LANES]
    u = jnp.dot(h, wu_ref[:, ff:2 * ff], preferred_element_type=F32)
    gc = _slab_conv(gs_ref, 0, n_slab, 0, tm, cw_ref, cb_ref)
    gtail_ref[...] = gs_ref[:, tm:tm + SUBLANES, :]
    act_ref[...] = (jax.nn.gelu(gc, approximate=True) * u).astype(act_ref.dtype)
    x2 = x1 + jnp.dot(act_ref[...], wd_ref[...], preferred_element_type=F32)
    gate = jax.nn.sigmoid(jnp.dot(_rms(x2, np_ref[...]).astype(BF16), wg_ref[...], preferred_element_type=F32))
    pe = jnp.dot(p_ref[0, 0].astype(BF16), wp_ref[...], preferred_element_type=F32)
    o_ref[0] = x2 + gate * pe


def _post_mixer(x, p, layer, nf, wu, cw, cb, wd, npl, wg, wp):
    bsz, s, d = x.shape
    tm = min(TM_POST, s)
    ff = wd.shape[1]
    stacked = (nf, wu, cw, cb, wd, npl, wg, wp)
    return pl.pallas_call(
        _post_kernel,
        out_shape=jax.ShapeDtypeStruct((bsz, s, d), F32),
        grid=(bsz, s // tm),
        in_specs=[pl.BlockSpec((1, tm, d), lambda i, j: (i, j, 0)),
                  pl.BlockSpec((1, 1, tm, p.shape[-1]), lambda i, j: (layer, i, j, 0))]
                 + [_layer_block(a, layer) for a in stacked],
        out_specs=pl.BlockSpec((1, tm, d), lambda i, j: (i, j, 0)),
        scratch_shapes=[pltpu.VMEM((ff // LANES, SUBLANES + tm, LANES), F32),
                        pltpu.VMEM((ff // LANES, SUBLANES, LANES), F32), pltpu.VMEM((tm, ff), BF16)],
        compiler_params=_params(2),
        name=f"post_mixer_{layer}",
    )(x, p, *stacked)


def _block_diag(w):
    n, r, _ = w.shape
    eye = jnp.eye(n, dtype=w.dtype)
    return (eye[:, None, :, None] * w[:, :, None, :]).reshape(n * r, n * r)


def kernel(x, p, positions, norm_mix, norm_ffn, norm_ple, hy_w_in, hy_b_in, rg_conv_w, rg_conv_b, rg_w_a, rg_b_a, rg_w_x, rg_b_x, rg_lambda, ml_conv_w, ml_conv_b, ml_norm, hy_w_out, at_w_qkv, at_q_norm, at_k_norm, at_sinks, at_w_out, ff_w_up, ff_conv_w, ff_conv_b, ff_w_down, ple_w_gate, ple_w_proj):
    depth = p.shape[0]
    row = lambda v: v.astype(F32)[None, :]
    post = (norm_ffn.astype(F32)[:, None, :], ff_w_up.astype(BF16), ff_conv_w.astype(F32),
            ff_conv_b.astype(F32)[:, None, :], ff_w_down.astype(BF16), norm_ple.astype(F32)[:, None, :],
            ple_w_gate.astype(BF16), ple_w_proj.astype(BF16))
    for layer in range(depth):
        if layer % 2 == 0:
            e = layer // 2
            pad = LANES - 2 * ML_HEADS
            x = _hybrid_mixer(x, row(norm_mix[layer]), jnp.pad(hy_w_in[e], ((0, 0), (0, pad))).astype(BF16),
                              row(jnp.pad(hy_b_in[e], (0, pad))),
                              rg_conv_w[e], row(rg_conv_b[e]), _block_diag(rg_w_a[e]).astype(BF16), row(rg_b_a[e]),
                              _block_diag(rg_w_x[e]).astype(BF16), row(rg_b_x[e]), row(rg_lambda[e]),
                              ml_conv_w[e], row(ml_conv_b[e]), row(ml_norm[e]), hy_w_out[e].astype(BF16))
        else:
            o = layer // 2
            x = _attn_mixer(x, row(norm_mix[layer]), at_w_qkv[o].astype(BF16), positions, at_q_norm[o],
                            at_k_norm[o], at_sinks[o], at_w_out[o].astype(BF16))
        x = _post_mixer(x, p, layer, *post)
    return x
```

```python
import math

import jax
import jax.numpy as jnp
import numpy as np
from jax import lax
from jax.experimental import pallas as pl
from jax.experimental.pallas import tpu as pltpu

F32 = jnp.float32
BF16 = jnp.bfloat16

RG_WIDTH = 512
RG_C = 8.0
ML_HEADS = 4
ML_HEAD_DIM = 128
ML_WIDTH = ML_HEADS * ML_HEAD_DIM
ML_CHUNK = 128
AT_HEADS = 16
AT_KV_HEADS = 4
AT_HEAD_DIM = 64
WINDOW = 128
ROPE_THETA = 10000.0
EPS = 1e-6
LOG2_E = 1.4426950408889634
GELU_C1 = -2.0 * math.sqrt(2.0 / math.pi) * LOG2_E
GELU_C3 = 0.044715 * GELU_C1

LANES = 128
SUBLANES = 8
VMEM_LIMIT_BYTES = 56 * 1024 * 1024

TS_MIX = 512
TM_POST = 512
PROJ_CHUNK = 512
RG_SLAB0 = 0
Q_SLAB0 = RG_WIDTH // LANES
K_SLAB0 = Q_SLAB0 + ML_WIDTH // LANES
N_CONV_SLAB = K_SLAB0 + ML_WIDTH // LANES


def _params(n_axes):
    return pltpu.CompilerParams(dimension_semantics=("arbitrary",) * n_axes,
                                vmem_limit_bytes=VMEM_LIMIT_BYTES)


def _resident(shape, index=None):
    n = len(shape)
    idx = (0,) * n if index is None else tuple(index)
    return pl.BlockSpec(shape, lambda *_: idx, pipeline_mode=pl.Buffered(1))


def _layer_block(arr, layer):
    shape = (None,) + arr.shape[1:]
    return _resident(shape, (layer,) + (0,) * (arr.ndim - 1))


def _gelu_tanh(x):
    return x / (1.0 + jnp.exp2(x * (GELU_C1 + GELU_C3 * (x * x))))


def _rms(x, g):
    return x * lax.rsqrt(jnp.mean(x * x, axis=-1, keepdims=True) + EPS) * g


def _slab_conv(zs_ref, slab0, n_slab, r0, n_rows, w_ref, b_ref):
    k_w = w_ref.shape[0]
    base = SUBLANES + r0
    outs = []
    for s in range(n_slab):
        cs = slice(s * LANES, (s + 1) * LANES)
        acc = zs_ref[slab0 + s, base:base + n_rows, :] * w_ref[k_w - 1:k_w, cs] + b_ref[:, cs]
        for j in range(k_w - 1):
            d = k_w - 1 - j
            acc = acc + zs_ref[slab0 + s, base - d:base - d + n_rows, :] * w_ref[j:j + 1, cs]
        outs.append(acc)
    return jnp.concatenate(outs, axis=1)


def _log_sigmoid(x):
    return jnp.minimum(x, 0.0) - jnp.log1p(jnp.exp(-jnp.abs(x)))


def _rg_gates(zs_ref, cw_ref, cb_ref, wa_ref, ba_ref, wx_ref, bx_ref, lam_ref, a_ref, u_ref):
    ts = a_ref.shape[0]
    xc = _slab_conv(zs_ref, RG_SLAB0, RG_WIDTH // LANES, 0, ts, cw_ref, cb_ref)
    xb = xc.astype(BF16)
    r = jax.nn.sigmoid(jnp.dot(xb, wa_ref[...], preferred_element_type=F32) + ba_ref[...])
    i = jax.nn.sigmoid(jnp.dot(xb, wx_ref[...], preferred_element_type=F32) + bx_ref[...])
    lam = lam_ref[...]
    softplus_neg_lam = jnp.maximum(-lam, 0.0) + jnp.log1p(jnp.exp(-jnp.abs(lam)))
    log_a = (-RG_C) * r * softplus_neg_lam
    a = jnp.exp(log_a)
    a_ref[...] = a
    v = 1.0 - a * a
    u_ref[...] = jnp.where(v > 0.0, v * lax.rsqrt(v), 0.0) * (i * xc)


def _rg_scan(z_ref, y_ref, hc_ref, a_ref, u_ref):
    ts = z_ref.shape[0]
    row = lax.broadcasted_iota(jnp.int32, (SUBLANES, RG_WIDTH), 0)
    hc = hc_ref[...]
    for k in range(ts // SUBLANES):
        rows = slice(k * SUBLANES, (k + 1) * SUBLANES)
        a = a_ref[rows, :]
        u = u_ref[rows, :]
        for d in (1, 2, 4):
            keep = row >= d
            u = u + a * jnp.where(keep, pltpu.roll(u, d, 0), 0.0)
            a = a * jnp.where(keep, pltpu.roll(a, d, 0), 1.0)
        h = u + a * hc
        u_ref[rows, :] = h
        hc = jnp.broadcast_to(h[SUBLANES - 1:SUBLANES, :], (SUBLANES, RG_WIDTH))
    hc_ref[...] = hc
    gate = _gelu_tanh(z_ref[:, RG_WIDTH:2 * RG_WIDTH])
    y_ref[:, 0:RG_WIDTH] = (u_ref[...] * gate).astype(y_ref.dtype)


def _lane_col(x, h):
    return jnp.broadcast_to(x[:, h:h + 1], x.shape)


def _ml_chunk(rows, z_ref, zs_ref, gt_ref, cw_ref, cb_ref, nw_ref, y_ref, c_ref, n_ref, m_ref):
    lc = ML_CHUNK
    dh = ML_HEAD_DIM
    q0 = 2 * RG_WIDTH
    k0, v0, o0 = q0 + ML_WIDTH, q0 + 2 * ML_WIDTH, q0 + 3 * ML_WIDTH
    row = lax.broadcasted_iota(jnp.int32, (lc, LANES), 0)

    n_sl = ML_WIDTH // LANES
    q = jax.nn.silu(_slab_conv(zs_ref, Q_SLAB0, n_sl, rows.start, lc, cw_ref.at[:, :ML_WIDTH], cb_ref.at[:, :ML_WIDTH]))
    k = jax.nn.silu(_slab_conv(zs_ref, K_SLAB0, n_sl, rows.start, lc, cw_ref.at[:, ML_WIDTH:], cb_ref.at[:, ML_WIDTH:]))
    k = k * (dh ** -0.5)

    gates = gt_ref[rows, :]
    b = _log_sigmoid(gates)
    d = 1
    while d < lc:
        b = b + jnp.where(row >= d, pltpu.roll(b, d, 0), 0.0)
        d *= 2
    b = pltpu.roll(b, LANES - ML_HEADS, 1)
    r = gates - b
    b_last = b[lc - 1:lc, :]
    g_loc = b_last + r
    m_loc = jnp.max(g_loc, axis=0, keepdims=True)
    w_loc = jnp.exp(g_loc - m_loc)
    m_st = m_ref[0:1, :]
    m_new = jnp.maximum(b_last + m_st, m_loc)
    a_prev = jnp.exp(b_last + m_st - m_new)
    a_loc = jnp.exp(m_loc - m_new)
    rmax = r
    d = 1
    while d < lc:
        rmax = jnp.maximum(rmax, jnp.where(row >= d, pltpu.roll(rmax, d, 0), -jnp.inf))
        d *= 2
    e = -jnp.maximum(m_st, rmax)
    a_inter = jnp.exp(m_st + e)
    exp_neg_mt = jnp.exp(e - b)
    r_t = r.T

    heads = [slice(h * dh, (h + 1) * dh) for h in range(ML_HEADS)]
    wide = lambda f: jnp.concatenate([f(h) for h in range(ML_HEADS)], axis=1)
    nt = (((1,), (1,)), ((), ()))
    row_w = lax.broadcasted_iota(jnp.int32, (lc, ML_WIDTH), 0)
    col_w = lax.broadcasted_iota(jnp.int32, (lc, ML_WIDTH), 1)
    causal = row_w >= (col_w % dh)
    qb = q.astype(BF16)
    kb = k.astype(BF16)
    vb = z_ref[rows, v0:o0].astype(BF16)
    c_prev = [c_ref[h] for h in range(ML_HEADS)]
    n_prev = wide(lambda h: n_ref[h:h + 1, :])
    s = wide(lambda h: lax.dot_general(qb[:, heads[h]], kb[:, heads[h]], nt, preferred_element_type=F32))
    w_intra = jnp.where(causal, jnp.exp(wide(lambda h: _lane_col(e, h)) + wide(lambda h: r_t[h:h + 1, :])), 0.0)
    s_qk = s * w_intra
    sb = s_qk.astype(BF16)
    a_in = wide(lambda h: _lane_col(a_inter, h))
    num = (wide(lambda h: jnp.dot(sb[:, heads[h]], vb[:, heads[h]], preferred_element_type=F32))
           + a_in * wide(lambda h: jnp.dot(qb[:, heads[h]], c_prev[h].astype(BF16), preferred_element_type=F32)))
    t = s_qk + a_in * (q * n_prev)
    den = wide(lambda h: jnp.broadcast_to(jnp.sum(t[:, heads[h]], axis=-1, keepdims=True), (lc, dh)))
    hm = num / jnp.maximum(jnp.abs(den), wide(lambda h: _lane_col(exp_neg_mt, h)))
    ms = wide(lambda h: jnp.broadcast_to(jnp.mean(hm[:, heads[h]] * hm[:, heads[h]], axis=-1, keepdims=True), (lc, dh)))
    hm = hm * lax.rsqrt(ms + EPS) * nw_ref[...]
    y = jax.nn.sigmoid(z_ref[rows, o0:o0 + ML_WIDTH]) * hm
    y_ref[rows, RG_WIDTH:RG_WIDTH + ML_WIDTH] = y.astype(y_ref.dtype)

    kw = k * wide(lambda h: _lane_col(w_loc, h))
    for h in range(ML_HEADS):
        c_loc = jnp.dot(kw[:, heads[h]].T.astype(BF16), vb[:, heads[h]], preferred_element_type=F32)
        n_loc = jnp.sum(kw[:, heads[h]], axis=0, keepdims=True)
        ap = a_prev[:, h:h + 1]
        al = a_loc[:, h:h + 1]
        c_ref[h] = ap * c_prev[h] + al * c_loc
        n_ref[h:h + 1, :] = ap * n_ref[h:h + 1, :] + al * n_loc
    m_ref[...] = jnp.broadcast_to(m_new, m_ref.shape)


def _hybrid_kernel(x_ref, xn_ref, nm_ref, win_ref, bin_ref,
                   rcw_ref, rcb_ref, wa_ref, ba_ref, wx_ref, bx_ref, lam_ref,
                   mcw_ref, mcb_ref, mnw_ref, wo_ref, o_ref,
                   za_ref, zb_ref, zsa_ref, zsb_ref, gta_ref, gtb_ref, y_ref, tail_ref, hc_ref, a_ref, u_ref,
                   c_ref, n_ref, m_ref):
    step = pl.program_id(0) * pl.num_programs(1) + pl.program_id(1)

    @pl.when(pl.program_id(1) == 0)
    def _():
        for ref in (tail_ref, hc_ref, c_ref, n_ref):
            ref[...] = jnp.zeros_like(ref)
        m_ref[...] = jnp.full_like(m_ref, -jnp.inf)

    n_main = 2 * RG_WIDTH + 4 * ML_WIDTH
    n_piece = n_main // PROJ_CHUNK + 1
    ts = x_ref.shape[1]
    conv_slab = {0: RG_SLAB0, 2 * RG_WIDTH // PROJ_CHUNK: Q_SLAB0, (2 * RG_WIDTH + ML_WIDTH) // PROJ_CHUNK: K_SLAB0}

    def project(src_ref, z_ref, zs_ref, gt_ref):
        h = _rms(src_ref[0], nm_ref[...]).astype(BF16)

        def piece(c):
            if c < n_main // PROJ_CHUNK:
                cs = slice(c * PROJ_CHUNK, (c + 1) * PROJ_CHUNK)
                res = jnp.dot(h, win_ref[:, cs], preferred_element_type=F32) + bin_ref[:, cs]
                if c in conv_slab:
                    for sl in range(PROJ_CHUNK // LANES):
                        zs_ref[conv_slab[c] + sl, SUBLANES:SUBLANES + ts, :] = res[:, sl * LANES:(sl + 1) * LANES]
                else:
                    z_ref[:, cs] = res
            else:
                gs = slice(n_main, n_main + LANES)
                gt_ref[...] = jnp.dot(h, win_ref[:, gs], preferred_element_type=F32) + bin_ref[:, gs]
        return piece

    @pl.when(step == 0)
    def _():
        piece = project(x_ref, za_ref, zsa_ref, gta_ref)
        for c in range(n_piece):
            piece(c)

    def mix(z_ref, zs_ref, gt_ref, zn_ref, zsn_ref, gtn_ref):
        piece = project(xn_ref, zn_ref, zsn_ref, gtn_ref)
        n_chunk = ts // ML_CHUNK
        zs_ref[:, 0:SUBLANES, :] = tail_ref[...]
        piece(0)
        _rg_gates(zs_ref, rcw_ref, rcb_ref, wa_ref, ba_ref, wx_ref, bx_ref, lam_ref, a_ref, u_ref)
        piece(1)
        piece(2)
        _rg_scan(z_ref, y_ref, hc_ref, a_ref, u_ref)
        nxt = 3
        for c in range(n_chunk):
            if nxt < n_piece:
                piece(nxt)
                nxt += 1
            rows = slice(c * ML_CHUNK, (c + 1) * ML_CHUNK)
            _ml_chunk(rows, z_ref, zs_ref, gt_ref, mcw_ref, mcb_ref, mnw_ref, y_ref, c_ref, n_ref, m_ref)
            o_ref[0, rows, :] = x_ref[0, rows, :] + jnp.dot(y_ref[rows, :], wo_ref[...], preferred_element_type=F32)
        for c in range(nxt, n_piece):
            piece(c)
        tail_ref[...] = zs_ref[:, ts:ts + SUBLANES, :]

    @pl.when(step % 2 == 0)
    def _():
        mix(za_ref, zsa_ref, gta_ref, zb_ref, zsb_ref, gtb_ref)

    @pl.when(step % 2 == 1)
    def _():
        mix(zb_ref, zsb_ref, gtb_ref, za_ref, zsa_ref, gta_ref)


def _next_block_map(bsz, n_blk):
    def index(i, j):
        f = jnp.minimum(i * n_blk + j + 1, bsz * n_blk - 1)
        return f // n_blk, f % n_blk, 0
    return index


def _hybrid_mixer(x, nm, w_in, b_in, rcw, rcb, wa, ba, wx, bx, lam, mcw, mcb, mnw, wo):
    bsz, s, d = x.shape
    ts = min(TS_MIX, s)
    n_main = w_in.shape[1] - LANES
    rw, mw = RG_WIDTH, ML_WIDTH
    consts = (nm, w_in, b_in, rcw, rcb, wa, ba, wx, bx, lam, mcw, mcb, mnw, wo)
    return pl.pallas_call(
        _hybrid_kernel,
        out_shape=jax.ShapeDtypeStruct((bsz, s, d), F32),
        grid=(bsz, s // ts),
        in_specs=[pl.BlockSpec((1, ts, d), lambda i, j: (i, j, 0)),
                  pl.BlockSpec((1, ts, d), _next_block_map(bsz, s // ts))] + [_resident(c.shape) for c in consts],
        out_specs=pl.BlockSpec((1, ts, d), lambda i, j: (i, j, 0)),
        scratch_shapes=[pltpu.VMEM((ts, n_main), F32), pltpu.VMEM((ts, n_main), F32),
                        pltpu.VMEM((N_CONV_SLAB, SUBLANES + ts, LANES), F32),
                        pltpu.VMEM((N_CONV_SLAB, SUBLANES + ts, LANES), F32),
                        pltpu.VMEM((ts, LANES), F32), pltpu.VMEM((ts, LANES), F32),
                        pltpu.VMEM((ts, rw + mw), BF16),
                        pltpu.VMEM((N_CONV_SLAB, SUBLANES, LANES), F32), pltpu.VMEM((SUBLANES, rw), F32),
                        pltpu.VMEM((ts, rw), F32), pltpu.VMEM((ts, rw), F32),
                        pltpu.VMEM((ML_HEADS, ML_HEAD_DIM, ML_HEAD_DIM), F32),
                        pltpu.VMEM((SUBLANES, ML_HEAD_DIM), F32),
                        pltpu.VMEM((SUBLANES, LANES), F32)],
        compiler_params=_params(2),
        name="hybrid_mixer",
    )(x, x, *consts)


def _attn_kernel(x_ref, xn_ref, nm_ref, wqkv_ref, pos_ref, freq_ref, qn_ref, kn_ref, seg_ref, sink_ref, wo_ref,
                 o_ref, qkva_ref, qkvb_ref, qs_ref, kf_ref, vt_ref, att_ref):
    tq = x_ref.shape[1]
    w = WINDOW
    hd = AT_HEAD_DIM
    hh = hd // 2
    n_sub = tq // w
    qw = AT_HEADS * hd
    kw = AT_KV_HEADS * hd
    n_qslab = qw // LANES
    n_kslab = kw // LANES
    n_pack = LANES // hh
    rp = w // n_pack
    first = pl.program_id(1) == 0
    step = pl.program_id(0) * pl.num_programs(1) + pl.program_id(1)

    @pl.when(first)
    def _():
        kf_ref[:, :, 0:w, :] = jnp.zeros((AT_KV_HEADS, 2, w, LANES), kf_ref.dtype)
        vt_ref[:, :, 0:w] = jnp.zeros((AT_KV_HEADS, hd, w), vt_ref.dtype)

    lane = lax.broadcasted_iota(jnp.int32, (w, LANES), 1)
    slot_a = (lane // hh) % 2 == 0
    sin_sign = jnp.where(lane < hd, -1.0, 1.0)
    lane_grp = lax.broadcasted_iota(jnp.int32, (rp, LANES), 1) // hh
    seg = seg_ref[...]
    n_piece = (qw + 2 * kw) // PROJ_CHUNK

    def project(src_ref, dst_ref):
        h = _rms(src_ref[0], nm_ref[...]).astype(BF16)

        def piece(c):
            cs = slice(c * PROJ_CHUNK, (c + 1) * PROJ_CHUNK)
            dst_ref[:, cs] = jnp.dot(h, wqkv_ref[:, cs], preferred_element_type=F32)
        return piece

    @pl.when(step == 0)
    def _():
        piece = project(x_ref, qkva_ref)
        for c in range(n_piece):
            piece(c)

    def norm_rope(t, gain, cos, sin):
        ms = jnp.dot((t * t).astype(BF16), seg, preferred_element_type=F32) * (1.0 / hd)
        tn = t * lax.rsqrt(ms + EPS) * gain
        return tn * cos + pltpu.roll(tn, hd, 1) * sin

    def spread(t):
        parts = []
        for g in range(n_pack):
            z = jnp.where(lane_grp == g, t, 0.0)
            y = z
            for k in range(1, n_pack):
                y = y + pltpu.roll(z, k * hh, 1)
            parts.append(y)
        return jnp.concatenate(parts, axis=0)

    def prepare(i, qkv_ref):
        rows = slice(i * w, (i + 1) * w)
        krows = slice(w + i * w, w + (i + 1) * w)
        pos = pos_ref[0, 0, i:i + 1, :].astype(F32)
        pos_col = jnp.broadcast_to(pos, (w, LANES)).T
        packed = pos_col[0:rp]
        for g in range(1, n_pack):
            packed = jnp.where(lane_grp == g, pos_col[g * rp:(g + 1) * rp], packed)
        ang = packed * freq_ref[...]
        cos = spread(jnp.cos(ang))
        sin = spread(jnp.sin(ang)) * sin_sign
        for sl in range(n_qslab):
            xq = norm_rope(qkv_ref[rows, sl * LANES:(sl + 1) * LANES], qn_ref[...], cos, sin) * (LOG2_E * hd ** -0.5)
            qs_ref[sl, rows, :] = xq.astype(qs_ref.dtype)
        for sl in range(n_kslab):
            xk = norm_rope(qkv_ref[rows, qw + sl * LANES:qw + (sl + 1) * LANES], kn_ref[...], cos, sin)
            kf_ref[2 * sl, 0, krows, :] = jnp.where(slot_a, xk, 0.0).astype(kf_ref.dtype)
            kf_ref[2 * sl, 1, krows, :] = jnp.where(slot_a, 0.0, pltpu.roll(xk, hh, 1)).astype(kf_ref.dtype)
            kf_ref[2 * sl + 1, 0, krows, :] = jnp.where(slot_a, pltpu.roll(xk, LANES - hh, 1), 0.0).astype(kf_ref.dtype)
            kf_ref[2 * sl + 1, 1, krows, :] = jnp.where(slot_a, 0.0, xk).astype(kf_ref.dtype)
            v0 = qw + kw + sl * LANES
            vt = qkv_ref[rows, v0:v0 + LANES].T.astype(vt_ref.dtype)
            vt_ref[2 * sl, :, krows] = vt[:hd]
            vt_ref[2 * sl + 1, :, krows] = vt[hd:]

    grp = AT_HEADS // AT_KV_HEADS
    row_a = lax.broadcasted_iota(jnp.int32, (w, AT_HEADS * w), 0)
    lane_a = lax.broadcasted_iota(jnp.int32, (w, AT_HEADS * w), 1)
    from_prev = row_a > (lane_a % w)
    nt = (((1,), (1,)), ((), ()))

    def attend(i):
        rows = slice(i * w, (i + 1) * w)
        win = slice(i * w, (i + 2) * w)
        parts = []
        for hk in range(AT_KV_HEADS):
            q2 = jnp.concatenate([qs_ref[2 * hk, rows, :], qs_ref[2 * hk + 1, rows, :]], axis=0)
            parts.append(lax.dot_general(kf_ref[hk, 0, win, :], q2, nt, preferred_element_type=F32))
            parts.append(lax.dot_general(kf_ref[hk, 1, win, :], q2, nt, preferred_element_type=F32))
        s = jnp.concatenate(parts, axis=1)
        s_prev = s[:w]
        if i == 0:
            s_prev = s_prev + jnp.where(first, -jnp.inf, 0.0)
        logits = jnp.where(from_prev, s_prev, s[w:])
        sink = sink_ref[...]
        m = jnp.maximum(jnp.max(logits, axis=0, keepdims=True), sink)
        p = jnp.exp2(logits - m)
        inv = 1.0 / (jnp.sum(p, axis=0, keepdims=True) + jnp.exp2(sink - m))
        pp = jnp.concatenate([jnp.where(from_prev, p, 0.0), jnp.where(from_prev, 0.0, p)],
                             axis=0).astype(BF16)
        for hk in range(AT_KV_HEADS):
            cols = slice(hk * grp * w, (hk + 1) * grp * w)
            r = jnp.dot(vt_ref[hk, :, win], pp[:, cols], preferred_element_type=F32) * inv[:, cols]
            for pair in range(grp // 2):
                out_t = jnp.concatenate([r[:, pair * w:(pair + 1) * w],
                                         r[:, (2 + pair) * w:(3 + pair) * w]], axis=0)
                sl = 2 * hk + pair
                att_ref[rows, sl * LANES:(sl + 1) * LANES] = out_t.T.astype(att_ref.dtype)

    def mix(qkv_ref, qkvn_ref):
        piece = project(xn_ref, qkvn_ref)
        piece(0)
        for i in range(n_sub):
            prepare(i, qkv_ref)
        nxt = 1
        for i in range(n_sub):
            attend(i)
            if i % 2 == 1:
                rows = slice((i - 1) * w, (i + 1) * w)
                o_ref[0, rows, :] = x_ref[0, rows, :] + jnp.dot(att_ref[rows, :], wo_ref[...],
                                                                preferred_element_type=F32)
            upto = 1 + ((n_piece - 1) * (i + 1)) // n_sub
            while nxt < upto:
                piece(nxt)
                nxt += 1
        kf_ref[:, :, 0:w, :] = kf_ref[:, :, tq:tq + w, :]
        vt_ref[:, :, 0:w] = vt_ref[:, :, tq:tq + w]

    @pl.when(step % 2 == 0)
    def _():
        mix(qkva_ref, qkvb_ref)

    @pl.when(step % 2 == 1)
    def _():
        mix(qkvb_ref, qkva_ref)


def _slab_lane_dims():
    hh = AT_HEAD_DIM // 2
    lane = np.arange(LANES)
    return (lane // hh) % 2, lane % hh + hh * (lane // AT_HEAD_DIM)


def _attn_mixer(x, nm, w_qkv, positions, q_norm, k_norm, sinks, wo):
    bsz, s, d = x.shape
    tq = min(TS_MIX, s)
    hd = AT_HEAD_DIM
    qw, kw = AT_HEADS * hd, AT_KV_HEADS * hd
    grp = AT_HEADS // AT_KV_HEADS
    inv_freq = ROPE_THETA ** (-jnp.arange(hd // 2, dtype=F32) * (2.0 / hd))
    freq = jnp.tile(inv_freq, LANES // (hd // 2))[None, :]
    head_in_slab, dim = _slab_lane_dims()
    seg = jnp.asarray((head_in_slab[:, None] == head_in_slab[None, :]).astype(np.float32), dtype=BF16)
    n_slab = (qw + kw) // LANES
    w_qk = w_qkv[:, :qw + kw].reshape(d, n_slab, 2, 2, hd // 2).transpose(0, 1, 3, 2, 4).reshape(d, qw + kw)
    w_qkv = jnp.concatenate([w_qk, w_qkv[:, qw + kw:]], axis=1)
    qn = q_norm.astype(F32)[dim][None, :]
    kn = k_norm.astype(F32)[dim][None, :]
    pos = positions.reshape(bsz, s // tq, tq // WINDOW, WINDOW)
    sk = (sinks.astype(F32) * LOG2_E).reshape(AT_KV_HEADS, grp)[:, np.array([0, 2, 1, 3])]
    sk = jnp.repeat(sk, WINDOW, axis=1).reshape(1, AT_HEADS * WINDOW)
    consts_a = (nm, w_qkv)
    consts_b = (freq, qn, kn, seg, sk, wo)
    return pl.pallas_call(
        _attn_kernel,
        out_shape=jax.ShapeDtypeStruct((bsz, s, d), F32),
        grid=(bsz, s // tq),
        in_specs=([pl.BlockSpec((1, tq, d), lambda i, j: (i, j, 0)),
                   pl.BlockSpec((1, tq, d), _next_block_map(bsz, s // tq))] + [_resident(c.shape) for c in consts_a]
                  + [pl.BlockSpec((1, 1, tq // WINDOW, WINDOW), lambda i, j: (i, j, 0, 0))]
                  + [_resident(c.shape) for c in consts_b]),
        out_specs=pl.BlockSpec((1, tq, d), lambda i, j: (i, j, 0)),
        scratch_shapes=[pltpu.VMEM((tq, qw + 2 * kw), F32), pltpu.VMEM((tq, qw + 2 * kw), F32),
                        pltpu.VMEM((qw // LANES, tq, LANES), BF16),
                        pltpu.VMEM((AT_KV_HEADS, 2, tq + WINDOW, LANES), BF16),
                        pltpu.VMEM((AT_KV_HEADS, hd, tq + WINDOW), BF16),
                        pltpu.VMEM((tq, qw), BF16)],
        compiler_params=_params(2),
        name="attn_mixer",
    )(x, x, *consts_a, pos, *consts_b)


def _post_kernel(x_ref, p_ref, nf_ref, wu_ref, cw_ref, cb_ref, wd_ref, np_ref, wg_ref, wp_ref, o_ref,
                 gs_ref, gtail_ref, act_ref):
    tm = x_ref.shape[1]
    ff = wd_ref.shape[0]
    n_slab = ff // LANES

    @pl.when(pl.program_id(1) == 0)
    def _():
        gtail_ref[...] = jnp.zeros_like(gtail_ref)

    x1 = x_ref[0]
    h = _rms(x1, nf_ref[...]).astype(BF16)
    gs_ref[:, 0:SUBLANES, :] = gtail_ref[...]
    g = jnp.dot(h, wu_ref[:, 0:ff], preferred_element_type=F32)
    for sl in range(n_slab):
        gs_ref[sl, SUBLANES:SUBLANES + tm, :] = g[:, sl * LANES:(sl + 1) * LANES]
    u = jnp.dot(h, wu_ref[:, ff:2 * ff], preferred_element_type=F32)
    gc = _slab_conv(gs_ref, 0, n_slab, 0, tm, cw_ref, cb_ref)
    gtail_ref[...] = gs_ref[:, tm:tm + SUBLANES, :]
    act_ref[...] = (_gelu_tanh(gc) * u).astype(act_ref.dtype)
    x2 = x1 + jnp.dot(act_ref[...], wd_ref[...], preferred_element_type=F32)
    gate = jax.nn.sigmoid(jnp.dot(_rms(x2, np_ref[...]).astype(BF16), wg_ref[...], preferred_element_type=F32))
    pe = jnp.dot(p_ref[0, 0].astype(BF16), wp_ref[...], preferred_element_type=F32)
    o_ref[0] = x2 + gate * pe


def _post_mixer(x, p, layer, nf, wu, cw, cb, wd, npl, wg, wp):
    bsz, s, d = x.shape
    tm = min(TM_POST, s)
    ff = wd.shape[1]
    stacked = (nf, wu, cw, cb, wd, npl, wg, wp)
    return pl.pallas_call(
        _post_kernel,
        out_shape=jax.ShapeDtypeStruct((bsz, s, d), F32),
        grid=(bsz, s // tm),
        in_specs=[pl.BlockSpec((1, tm, d), lambda i, j: (i, j, 0)),
                  pl.BlockSpec((1, 1, tm, p.shape[-1]), lambda i, j: (layer, i, j, 0))]
                 + [_layer_block(a, layer) for a in stacked],
        out_specs=pl.BlockSpec((1, tm, d), lambda i, j: (i, j, 0)),
        scratch_shapes=[pltpu.VMEM((ff // LANES, SUBLANES + tm, LANES), F32),
                        pltpu.VMEM((ff // LANES, SUBLANES, LANES), F32), pltpu.VMEM((tm, ff), BF16)],
        compiler_params=_params(2),
        name=f"post_mixer_{layer}",
    )(x, p, *stacked)


def _block_diag(w):
    n, r, _ = w.shape
    eye = jnp.eye(n, dtype=w.dtype)
    return (eye[:, None, :, None] * w[:, :, None, :]).reshape(n * r, n * r)


def kernel(x, p, positions, norm_mix, norm_ffn, norm_ple, hy_w_in, hy_b_in, rg_conv_w, rg_conv_b, rg_w_a, rg_b_a, rg_w_x, rg_b_x, rg_lambda, ml_conv_w, ml_conv_b, ml_norm, hy_w_out, at_w_qkv, at_q_norm, at_k_norm, at_sinks, at_w_out, ff_w_up, ff_conv_w, ff_conv_b, ff_w_down, ple_w_gate, ple_w_proj):
    depth = p.shape[0]
    row = lambda v: v.astype(F32)[None, :]
    post = (norm_ffn.astype(F32)[:, None, :], ff_w_up.astype(BF16), ff_conv_w.astype(F32),
            ff_conv_b.astype(F32)[:, None, :], ff_w_down.astype(BF16), norm_ple.astype(F32)[:, None, :],
            ple_w_gate.astype(BF16), ple_w_proj.astype(BF16))
    for layer in range(depth):
        if layer % 2 == 0:
            e = layer // 2
            pad = LANES - 2 * ML_HEADS
            x = _hybrid_mixer(x, row(norm_mix[layer]), jnp.pad(hy_w_in[e], ((0, 0), (0, pad))).astype(BF16),
                              row(jnp.pad(hy_b_in[e], (0, pad))),
                              rg_conv_w[e], row(rg_conv_b[e]), _block_diag(rg_w_a[e]).astype(BF16), row(rg_b_a[e]),
                              _block_diag(rg_w_x[e]).astype(BF16), row(rg_b_x[e]), row(rg_lambda[e]),
                              ml_conv_w[e], row(ml_conv_b[e]), row(ml_norm[e]), hy_w_out[e].astype(BF16))
        else:
            o = layer // 2
            x = _attn_mixer(x, row(norm_mix[layer]), at_w_qkv[o].astype(BF16), positions, at_q_norm[o],
                            at_k_norm[o], at_sinks[o], at_w_out[o].astype(BF16))
        x = _post_mixer(x, p, layer, *post)
    return x
```

```python
import jax
import jax.numpy as jnp
import numpy as np
from jax import lax
from jax.experimental import pallas as pl
from jax.experimental.pallas import tpu as pltpu

F32 = jnp.float32
BF16 = jnp.bfloat16

RG_WIDTH = 512
RG_C = 8.0
ML_HEADS = 4
ML_HEAD_DIM = 128
ML_WIDTH = ML_HEADS * ML_HEAD_DIM
ML_CHUNK = 128
AT_HEADS = 16
AT_KV_HEADS = 4
AT_HEAD_DIM = 64
WINDOW = 128
ROPE_THETA = 10000.0
EPS = 1e-6
LOG2_E = 1.4426950408889634

LANES = 128
SUBLANES = 8
VMEM_LIMIT_BYTES = 56 * 1024 * 1024

TS_MIX = 512
TM_POST = 512
PROJ_CHUNK = 512
RG_SLAB0 = 0
Q_SLAB0 = RG_WIDTH // LANES
K_SLAB0 = Q_SLAB0 + ML_WIDTH // LANES
N_CONV_SLAB = K_SLAB0 + ML_WIDTH // LANES


def _params(n_axes):
    return pltpu.CompilerParams(dimension_semantics=("arbitrary",) * n_axes,
                                vmem_limit_bytes=VMEM_LIMIT_BYTES)


def _resident(shape, index=None):
    n = len(shape)
    idx = (0,) * n if index is None else tuple(index)
    return pl.BlockSpec(shape, lambda *_: idx, pipeline_mode=pl.Buffered(1))


def _layer_block(arr, layer):
    shape = (None,) + arr.shape[1:]
    return _resident(shape, (layer,) + (0,) * (arr.ndim - 1))


def _rms(x, g):
    return x * lax.rsqrt(jnp.mean(x * x, axis=-1, keepdims=True) + EPS) * g


def _slab_conv(zs_ref, slab0, n_slab, r0, n_rows, w_ref, b_ref):
    k_w = w_ref.shape[0]
    base = SUBLANES + r0
    outs = []
    for s in range(n_slab):
        cs = slice(s * LANES, (s + 1) * LANES)
        acc = zs_ref[slab0 + s, base:base + n_rows, :] * w_ref[k_w - 1:k_w, cs] + b_ref[:, cs]
        for j in range(k_w - 1):
            d = k_w - 1 - j
            acc = acc + zs_ref[slab0 + s, base - d:base - d + n_rows, :] * w_ref[j:j + 1, cs]
        outs.append(acc)
    return jnp.concatenate(outs, axis=1)


def _log_sigmoid(x):
    return jnp.minimum(x, 0.0) - jnp.log1p(jnp.exp(-jnp.abs(x)))


def _rg_gates(zs_ref, cw_ref, cb_ref, wa_ref, ba_ref, wx_ref, bx_ref, lam_ref, a_ref, u_ref):
    ts = a_ref.shape[0]
    xc = _slab_conv(zs_ref, RG_SLAB0, RG_WIDTH // LANES, 0, ts, cw_ref, cb_ref)
    xb = xc.astype(BF16)
    r = jax.nn.sigmoid(jnp.dot(xb, wa_ref[...], preferred_element_type=F32) + ba_ref[...])
    i = jax.nn.sigmoid(jnp.dot(xb, wx_ref[...], preferred_element_type=F32) + bx_ref[...])
    lam = lam_ref[...]
    softplus_neg_lam = jnp.maximum(-lam, 0.0) + jnp.log1p(jnp.exp(-jnp.abs(lam)))
    log_a = (-RG_C) * r * softplus_neg_lam
    a = jnp.exp(log_a)
    a_ref[...] = a
    v = 1.0 - a * a
    u_ref[...] = jnp.where(v > 0.0, v * lax.rsqrt(v), 0.0) * (i * xc)


def _rg_scan(z_ref, y_ref, hc_ref, a_ref, u_ref):
    ts = z_ref.shape[0]
    row = lax.broadcasted_iota(jnp.int32, (SUBLANES, RG_WIDTH), 0)
    hc = hc_ref[...]
    for k in range(ts // SUBLANES):
        rows = slice(k * SUBLANES, (k + 1) * SUBLANES)
        a = a_ref[rows, :]
        u = u_ref[rows, :]
        for d in (1, 2, 4):
            keep = row >= d
            u = u + a * jnp.where(keep, pltpu.roll(u, d, 0), 0.0)
            a = a * jnp.where(keep, pltpu.roll(a, d, 0), 1.0)
        h = u + a * hc
        u_ref[rows, :] = h
        hc = jnp.broadcast_to(h[SUBLANES - 1:SUBLANES, :], (SUBLANES, RG_WIDTH))
    hc_ref[...] = hc
    gate = jax.nn.gelu(z_ref[:, RG_WIDTH:2 * RG_WIDTH], approximate=True)
    y_ref[:, 0:RG_WIDTH] = (u_ref[...] * gate).astype(y_ref.dtype)


def _lane_col(x, h):
    return jnp.broadcast_to(x[:, h:h + 1], x.shape)


def _ml_chunk(rows, z_ref, zs_ref, gt_ref, cw_ref, cb_ref, nw_ref, y_ref, c_ref, n_ref, m_ref):
    lc = ML_CHUNK
    dh = ML_HEAD_DIM
    q0 = 2 * RG_WIDTH
    k0, v0, o0 = q0 + ML_WIDTH, q0 + 2 * ML_WIDTH, q0 + 3 * ML_WIDTH
    row = lax.broadcasted_iota(jnp.int32, (lc, LANES), 0)

    n_sl = ML_WIDTH // LANES
    q = jax.nn.silu(_slab_conv(zs_ref, Q_SLAB0, n_sl, rows.start, lc, cw_ref.at[:, :ML_WIDTH], cb_ref.at[:, :ML_WIDTH]))
    k = jax.nn.silu(_slab_conv(zs_ref, K_SLAB0, n_sl, rows.start, lc, cw_ref.at[:, ML_WIDTH:], cb_ref.at[:, ML_WIDTH:]))
    k = k * (dh ** -0.5)

    gates = gt_ref[rows, :]
    b = _log_sigmoid(gates)
    d = 1
    while d < lc:
        b = b + jnp.where(row >= d, pltpu.roll(b, d, 0), 0.0)
        d *= 2
    b = pltpu.roll(b, LANES - ML_HEADS, 1)
    r = gates - b
    b_last = b[lc - 1:lc, :]
    g_loc = b_last + r
    m_loc = jnp.max(g_loc, axis=0, keepdims=True)
    w_loc = jnp.exp(g_loc - m_loc)
    m_st = m_ref[0:1, :]
    m_new = jnp.maximum(b_last + m_st, m_loc)
    a_prev = jnp.exp(b_last + m_st - m_new)
    a_loc = jnp.exp(m_loc - m_new)
    rmax = r
    d = 1
    while d < lc:
        rmax = jnp.maximum(rmax, jnp.where(row >= d, pltpu.roll(rmax, d, 0), -jnp.inf))
        d *= 2
    e = -jnp.maximum(m_st, rmax)
    a_inter = jnp.exp(m_st + e)
    exp_neg_mt = jnp.exp(e - b)
    r_t = r.T

    heads = [slice(h * dh, (h + 1) * dh) for h in range(ML_HEADS)]
    wide = lambda f: jnp.concatenate([f(h) for h in range(ML_HEADS)], axis=1)
    nt = (((1,), (1,)), ((), ()))
    row_w = lax.broadcasted_iota(jnp.int32, (lc, ML_WIDTH), 0)
    col_w = lax.broadcasted_iota(jnp.int32, (lc, ML_WIDTH), 1)
    causal = row_w >= (col_w % dh)
    qb = q.astype(BF16)
    kb = k.astype(BF16)
    vb = z_ref[rows, v0:o0].astype(BF16)
    c_prev = [c_ref[h] for h in range(ML_HEADS)]
    n_prev = wide(lambda h: n_ref[h:h + 1, :])
    s = wide(lambda h: lax.dot_general(qb[:, heads[h]], kb[:, heads[h]], nt, preferred_element_type=F32))
    w_intra = jnp.where(causal, jnp.exp(wide(lambda h: _lane_col(e, h)) + wide(lambda h: r_t[h:h + 1, :])), 0.0)
    s_qk = s * w_intra
    sb = s_qk.astype(BF16)
    a_in = wide(lambda h: _lane_col(a_inter, h))
    num = (wide(lambda h: jnp.dot(sb[:, heads[h]], vb[:, heads[h]], preferred_element_type=F32))
           + a_in * wide(lambda h: jnp.dot(qb[:, heads[h]], c_prev[h].astype(BF16), preferred_element_type=F32)))
    t = s_qk + a_in * (q * n_prev)
    den = wide(lambda h: jnp.broadcast_to(jnp.sum(t[:, heads[h]], axis=-1, keepdims=True), (lc, dh)))
    hm = num / jnp.maximum(jnp.abs(den), wide(lambda h: _lane_col(exp_neg_mt, h)))
    ms = wide(lambda h: jnp.broadcast_to(jnp.mean(hm[:, heads[h]] * hm[:, heads[h]], axis=-1, keepdims=True), (lc, dh)))
    hm = hm * lax.rsqrt(ms + EPS) * nw_ref[...]
    y = jax.nn.sigmoid(z_ref[rows, o0:o0 + ML_WIDTH]) * hm
    y_ref[rows, RG_WIDTH:RG_WIDTH + ML_WIDTH] = y.astype(y_ref.dtype)

    kw = k * wide(lambda h: _lane_col(w_loc, h))
    for h in range(ML_HEADS):
        c_loc = jnp.dot(kw[:, heads[h]].T.astype(BF16), vb[:, heads[h]], preferred_element_type=F32)
        n_loc = jnp.sum(kw[:, heads[h]], axis=0, keepdims=True)
        ap = a_prev[:, h:h + 1]
        al = a_loc[:, h:h + 1]
        c_ref[h] = ap * c_prev[h] + al * c_loc
        n_ref[h:h + 1, :] = ap * n_ref[h:h + 1, :] + al * n_loc
    m_ref[...] = jnp.broadcast_to(m_new, m_ref.shape)


def _hybrid_kernel(x_ref, xn_ref, nm_ref, win_ref, bin_ref,
                   rcw_ref, rcb_ref, wa_ref, ba_ref, wx_ref, bx_ref, lam_ref,
                   mcw_ref, mcb_ref, mnw_ref, wo_ref, o_ref,
                   za_ref, zb_ref, zsa_ref, zsb_ref, gta_ref, gtb_ref, y_ref, tail_ref, hc_ref, a_ref, u_ref,
                   c_ref, n_ref, m_ref):
    step = pl.program_id(0) * pl.num_programs(1) + pl.program_id(1)

    @pl.when(pl.program_id(1) == 0)
    def _():
        for ref in (tail_ref, hc_ref, c_ref, n_ref):
            ref[...] = jnp.zeros_like(ref)
        m_ref[...] = jnp.full_like(m_ref, -jnp.inf)

    n_main = 2 * RG_WIDTH + 4 * ML_WIDTH
    n_piece = n_main // PROJ_CHUNK + 1
    ts = x_ref.shape[1]
    conv_slab = {0: RG_SLAB0, 2 * RG_WIDTH // PROJ_CHUNK: Q_SLAB0, (2 * RG_WIDTH + ML_WIDTH) // PROJ_CHUNK: K_SLAB0}

    def project(src_ref, z_ref, zs_ref, gt_ref):
        h = _rms(src_ref[0], nm_ref[...]).astype(BF16)

        def piece(c):
            if c < n_main // PROJ_CHUNK:
                cs = slice(c * PROJ_CHUNK, (c + 1) * PROJ_CHUNK)
                res = jnp.dot(h, win_ref[:, cs], preferred_element_type=F32) + bin_ref[:, cs]
                if c in conv_slab:
                    for sl in range(PROJ_CHUNK // LANES):
                        zs_ref[conv_slab[c] + sl, SUBLANES:SUBLANES + ts, :] = res[:, sl * LANES:(sl + 1) * LANES]
                else:
                    z_ref[:, cs] = res
            else:
                gs = slice(n_main, n_main + LANES)
                gt_ref[...] = jnp.dot(h, win_ref[:, gs], preferred_element_type=F32) + bin_ref[:, gs]
        return piece

    @pl.when(step == 0)
    def _():
        piece = project(x_ref, za_ref, zsa_ref, gta_ref)
        for c in range(n_piece):
            piece(c)

    def mix(z_ref, zs_ref, gt_ref, zn_ref, zsn_ref, gtn_ref):
        piece = project(xn_ref, zn_ref, zsn_ref, gtn_ref)
        n_chunk = ts // ML_CHUNK
        zs_ref[:, 0:SUBLANES, :] = tail_ref[...]
        piece(0)
        _rg_gates(zs_ref, rcw_ref, rcb_ref, wa_ref, ba_ref, wx_ref, bx_ref, lam_ref, a_ref, u_ref)
        piece(1)
        piece(2)
        _rg_scan(z_ref, y_ref, hc_ref, a_ref, u_ref)
        nxt = 3
        for c in range(n_chunk):
            if nxt < n_piece:
                piece(nxt)
                nxt += 1
            rows = slice(c * ML_CHUNK, (c + 1) * ML_CHUNK)
            _ml_chunk(rows, z_ref, zs_ref, gt_ref, mcw_ref, mcb_ref, mnw_ref, y_ref, c_ref, n_ref, m_ref)
            o_ref[0, rows, :] = x_ref[0, rows, :] + jnp.dot(y_ref[rows, :], wo_ref[...], preferred_element_type=F32)
        for c in range(nxt, n_piece):
            piece(c)
        tail_ref[...] = zs_ref[:, ts:ts + SUBLANES, :]

    @pl.when(step % 2 == 0)
    def _():
        mix(za_ref, zsa_ref, gta_ref, zb_ref, zsb_ref, gtb_ref)

    @pl.when(step % 2 == 1)
    def _():
        mix(zb_ref, zsb_ref, gtb_ref, za_ref, zsa_ref, gta_ref)


def _next_block_map(bsz, n_blk):
    def index(i, j):
        f = jnp.minimum(i * n_blk + j + 1, bsz * n_blk - 1)
        return f // n_blk, f % n_blk, 0
    return index


def _hybrid_mixer(x, nm, w_in, b_in, rcw, rcb, wa, ba, wx, bx, lam, mcw, mcb, mnw, wo):
    bsz, s, d = x.shape
    ts = min(TS_MIX, s)
    n_main = w_in.shape[1] - LANES
    rw, mw = RG_WIDTH, ML_WIDTH
    consts = (nm, w_in, b_in, rcw, rcb, wa, ba, wx, bx, lam, mcw, mcb, mnw, wo)
    return pl.pallas_call(
        _hybrid_kernel,
        out_shape=jax.ShapeDtypeStruct((bsz, s, d), F32),
        grid=(bsz, s // ts),
        in_specs=[pl.BlockSpec((1, ts, d), lambda i, j: (i, j, 0)),
                  pl.BlockSpec((1, ts, d), _next_block_map(bsz, s // ts))] + [_resident(c.shape) for c in consts],
        out_specs=pl.BlockSpec((1, ts, d), lambda i, j: (i, j, 0)),
        scratch_shapes=[pltpu.VMEM((ts, n_main), F32), pltpu.VMEM((ts, n_main), F32),
                        pltpu.VMEM((N_CONV_SLAB, SUBLANES + ts, LANES), F32),
                        pltpu.VMEM((N_CONV_SLAB, SUBLANES + ts, LANES), F32),
                        pltpu.VMEM((ts, LANES), F32), pltpu.VMEM((ts, LANES), F32),
                        pltpu.VMEM((ts, rw + mw), BF16),
                        pltpu.VMEM((N_CONV_SLAB, SUBLANES, LANES), F32), pltpu.VMEM((SUBLANES, rw), F32),
                        pltpu.VMEM((ts, rw), F32), pltpu.VMEM((ts, rw), F32),
                        pltpu.VMEM((ML_HEADS, ML_HEAD_DIM, ML_HEAD_DIM), F32),
                        pltpu.VMEM((SUBLANES, ML_HEAD_DIM), F32),
                        pltpu.VMEM((SUBLANES, LANES), F32)],
        compiler_params=_params(2),
        name="hybrid_mixer",
    )(x, x, *consts)


def _attn_kernel(x_ref, xn_ref, nm_ref, wqkv_ref, pos_ref, freq_ref, qn_ref, kn_ref, seg_ref, sink_ref, wo_ref,
                 o_ref, qkva_ref, qkvb_ref, qs_ref, kf_ref, vt_ref, att_ref):
    tq = x_ref.shape[1]
    w = WINDOW
    hd = AT_HEAD_DIM
    hh = hd // 2
    n_sub = tq // w
    qw = AT_HEADS * hd
    kw = AT_KV_HEADS * hd
    n_qslab = qw // LANES
    n_kslab = kw // LANES
    n_pack = LANES // hh
    rp = w // n_pack
    first = pl.program_id(1) == 0
    step = pl.program_id(0) * pl.num_programs(1) + pl.program_id(1)

    @pl.when(first)
    def _():
        kf_ref[:, :, 0:w, :] = jnp.zeros((AT_KV_HEADS, 2, w, LANES), kf_ref.dtype)
        vt_ref[:, :, 0:w] = jnp.zeros((AT_KV_HEADS, hd, w), vt_ref.dtype)

    lane = lax.broadcasted_iota(jnp.int32, (w, LANES), 1)
    slot_a = (lane // hh) % 2 == 0
    sin_sign = jnp.where(lane < hd, -1.0, 1.0)
    lane_grp = lax.broadcasted_iota(jnp.int32, (rp, LANES), 1) // hh
    seg = seg_ref[...]
    n_piece = (qw + 2 * kw) // PROJ_CHUNK

    def project(src_ref, dst_ref):
        h = _rms(src_ref[0], nm_ref[...]).astype(BF16)

        def piece(c):
            cs = slice(c * PROJ_CHUNK, (c + 1) * PROJ_CHUNK)
            dst_ref[:, cs] = jnp.dot(h, wqkv_ref[:, cs], preferred_element_type=F32)
        return piece

    @pl.when(step == 0)
    def _():
        piece = project(x_ref, qkva_ref)
        for c in range(n_piece):
            piece(c)

    def norm_rope(t, gain, cos, sin):
        ms = jnp.dot((t * t).astype(BF16), seg, preferred_element_type=F32) * (1.0 / hd)
        tn = t * lax.rsqrt(ms + EPS) * gain
        return tn * cos + pltpu.roll(tn, hd, 1) * sin

    def spread(t):
        parts = []
        for g in range(n_pack):
            z = jnp.where(lane_grp == g, t, 0.0)
            y = z
            for k in range(1, n_pack):
                y = y + pltpu.roll(z, k * hh, 1)
            parts.append(y)
        return jnp.concatenate(parts, axis=0)

    def prepare(i, qkv_ref):
        rows = slice(i * w, (i + 1) * w)
        krows = slice(w + i * w, w + (i + 1) * w)
        pos = pos_ref[0, 0, i:i + 1, :].astype(F32)
        pos_col = jnp.broadcast_to(pos, (w, LANES)).T
        packed = pos_col[0:rp]
        for g in range(1, n_pack):
            packed = jnp.where(lane_grp == g, pos_col[g * rp:(g + 1) * rp], packed)
        ang = packed * freq_ref[...]
        cos = spread(jnp.cos(ang))
        sin = spread(jnp.sin(ang)) * sin_sign
        for sl in range(n_qslab):
            xq = norm_rope(qkv_ref[rows, sl * LANES:(sl + 1) * LANES], qn_ref[...], cos, sin) * (LOG2_E * hd ** -0.5)
            qs_ref[sl, rows, :] = xq.astype(qs_ref.dtype)
        for sl in range(n_kslab):
            xk = norm_rope(qkv_ref[rows, qw + sl * LANES:qw + (sl + 1) * LANES], kn_ref[...], cos, sin)
            kf_ref[2 * sl, 0, krows, :] = jnp.where(slot_a, xk, 0.0).astype(kf_ref.dtype)
            kf_ref[2 * sl, 1, krows, :] = jnp.where(slot_a, 0.0, pltpu.roll(xk, hh, 1)).astype(kf_ref.dtype)
            kf_ref[2 * sl + 1, 0, krows, :] = jnp.where(slot_a, pltpu.roll(xk, LANES - hh, 1), 0.0).astype(kf_ref.dtype)
            kf_ref[2 * sl + 1, 1, krows, :] = jnp.where(slot_a, 0.0, xk).astype(kf_ref.dtype)
            v0 = qw + kw + sl * LANES
            vt = qkv_ref[rows, v0:v0 + LANES].T.astype(vt_ref.dtype)
            vt_ref[2 * sl, :, krows] = vt[:hd]
            vt_ref[2 * sl + 1, :, krows] = vt[hd:]

    grp = AT_HEADS // AT_KV_HEADS
    row_a = lax.broadcasted_iota(jnp.int32, (w, AT_HEADS * w), 0)
    lane_a = lax.broadcasted_iota(jnp.int32, (w, AT_HEADS * w), 1)
    from_prev = row_a > (lane_a % w)
    nt = (((1,), (1,)), ((), ()))

    def attend(i):
        rows = slice(i * w, (i + 1) * w)
        win = slice(i * w, (i + 2) * w)
        fp = from_prev[:, :2 * w]
        parts = []
        for hk in range(AT_KV_HEADS):
            q2 = jnp.concatenate([qs_ref[2 * hk, rows, :], qs_ref[2 * hk + 1, rows, :]], axis=0)
            for half in range(2):
                sp = lax.dot_general(kf_ref[hk, half, win, :], q2, nt, preferred_element_type=F32)
                s_prev = sp[:w]
                if i == 0:
                    s_prev = s_prev + jnp.where(first, -jnp.inf, 0.0)
                parts.append(jnp.where(fp, s_prev, sp[w:]))
        logits = jnp.concatenate(parts, axis=1)
        sink = sink_ref[...]
        m = jnp.maximum(jnp.max(logits, axis=0, keepdims=True), sink)
        p = jnp.exp2(logits - m)
        inv = 1.0 / (jnp.sum(p, axis=0, keepdims=True) + jnp.exp2(sink - m))
        pb = p.astype(BF16)
        zero = jnp.zeros_like(pb)
        pp = jnp.concatenate([jnp.where(from_prev, pb, zero), jnp.where(from_prev, zero, pb)], axis=0)
        for hk in range(AT_KV_HEADS):
            cols = slice(hk * grp * w, (hk + 1) * grp * w)
            r = jnp.dot(vt_ref[hk, :, win], pp[:, cols], preferred_element_type=F32) * inv[:, cols]
            for pair in range(grp // 2):
                out_t = jnp.concatenate([r[:, pair * w:(pair + 1) * w],
                                         r[:, (2 + pair) * w:(3 + pair) * w]], axis=0)
                sl = 2 * hk + pair
                att_ref[rows, sl * LANES:(sl + 1) * LANES] = out_t.T.astype(att_ref.dtype)

    def mix(qkv_ref, qkvn_ref):
        piece = project(xn_ref, qkvn_ref)
        piece(0)
        for i in range(n_sub):
            prepare(i, qkv_ref)
        nxt = 1
        for i in range(n_sub):
            attend(i)
            if i % 2 == 1:
                rows = slice((i - 1) * w, (i + 1) * w)
                o_ref[0, rows, :] = x_ref[0, rows, :] + jnp.dot(att_ref[rows, :], wo_ref[...],
                                                                preferred_element_type=F32)
            upto = 1 + ((n_piece - 1) * (i + 1)) // n_sub
            while nxt < upto:
                piece(nxt)
                nxt += 1
        kf_ref[:, :, 0:w, :] = kf_ref[:, :, tq:tq + w, :]
        vt_ref[:, :, 0:w] = vt_ref[:, :, tq:tq + w]

    @pl.when(step % 2 == 0)
    def _():
        mix(qkva_ref, qkvb_ref)

    @pl.when(step % 2 == 1)
    def _():
        mix(qkvb_ref, qkva_ref)


def _slab_lane_dims():
    hh = AT_HEAD_DIM // 2
    lane = np.arange(LANES)
    return (lane // hh) % 2, lane % hh + hh * (lane // AT_HEAD_DIM)


def _attn_mixer(x, nm, w_qkv, positions, q_norm, k_norm, sinks, wo):
    bsz, s, d = x.shape
    tq = min(TS_MIX, s)
    hd = AT_HEAD_DIM
    qw, kw = AT_HEADS * hd, AT_KV_HEADS * hd
    grp = AT_HEADS // AT_KV_HEADS
    inv_freq = ROPE_THETA ** (-jnp.arange(hd // 2, dtype=F32) * (2.0 / hd))
    freq = jnp.tile(inv_freq, LANES // (hd // 2))[None, :]
    head_in_slab, dim = _slab_lane_dims()
    seg = jnp.asarray((head_in_slab[:, None] == head_in_slab[None, :]).astype(np.float32), dtype=BF16)
    n_slab = (qw + kw) // LANES
    w_qk = w_qkv[:, :qw + kw].reshape(d, n_slab, 2, 2, hd // 2).transpose(0, 1, 3, 2, 4).reshape(d, qw + kw)
    w_qkv = jnp.concatenate([w_qk, w_qkv[:, qw + kw:]], axis=1)
    qn = q_norm.astype(F32)[dim][None, :]
    kn = k_norm.astype(F32)[dim][None, :]
    pos = positions.reshape(bsz, s // tq, tq // WINDOW, WINDOW)
    sk = (sinks.astype(F32) * LOG2_E).reshape(AT_KV_HEADS, grp)[:, np.array([0, 2, 1, 3])]
    sk = jnp.repeat(sk, WINDOW, axis=1).reshape(1, AT_HEADS * WINDOW)
    consts_a = (nm, w_qkv)
    consts_b = (freq, qn, kn, seg, sk, wo)
    return pl.pallas_call(
        _attn_kernel,
        out_shape=jax.ShapeDtypeStruct((bsz, s, d), F32),
        grid=(bsz, s // tq),
        in_specs=([pl.BlockSpec((1, tq, d), lambda i, j: (i, j, 0)),
                   pl.BlockSpec((1, tq, d), _next_block_map(bsz, s // tq))] + [_resident(c.shape) for c in consts_a]
                  + [pl.BlockSpec((1, 1, tq // WINDOW, WINDOW), lambda i, j: (i, j, 0, 0))]
                  + [_resident(c.shape) for c in consts_b]),
        out_specs=pl.BlockSpec((1, tq, d), lambda i, j: (i, j, 0)),
        scratch_shapes=[pltpu.VMEM((tq, qw + 2 * kw), F32), pltpu.VMEM((tq, qw + 2 * kw), F32),
                        pltpu.VMEM((qw // LANES, tq, LANES), BF16),
                        pltpu.VMEM((AT_KV_HEADS, 2, tq + WINDOW, LANES), BF16),
                        pltpu.VMEM((AT_KV_HEADS, hd, tq + WINDOW), BF16),
                        pltpu.VMEM((tq, qw), BF16)],
        compiler_params=_params(2),
        name="attn_mixer",
    )(x, x, *consts_a, pos, *consts_b)


def _post_kernel(x_ref, p_ref, nf_ref, wu_ref, cw_ref, cb_ref, wd_ref, np_ref, wg_ref, wp_ref, o_ref,
                 gs_ref, gtail_ref, act_ref):
    tm = x_ref.shape[1]
    ff = wd_ref.shape[0]
    n_slab = ff // LANES

    @pl.when(pl.program_id(1) == 0)
    def _():
        gtail_ref[...] = jnp.zeros_like(gtail_ref)

    x1 = x_ref[0]
    h = _rms(x1, nf_ref[...]).astype(BF16)
    gs_ref[:, 0:SUBLANES, :] = gtail_ref[...]
    g = jnp.dot(h, wu_ref[:, 0:ff], preferred_element_type=F32)
    for sl in range(n_slab):
        gs_ref[sl, SUBLANES:SUBLANES + tm, :] = g[:, sl * LANES:(sl + 1) * LANES]
    u = jnp.dot(h, wu_ref[:, ff:2 * ff], preferred_element_type=F32)
    gc = _slab_conv(gs_ref, 0, n_slab, 0, tm, cw_ref, cb_ref)
    gtail_ref[...] = gs_ref[:, tm:tm + SUBLANES, :]
    act_ref[...] = (jax.nn.gelu(gc, approximate=True) * u).astype(act_ref.dtype)
    x2 = x1 + jnp.dot(act_ref[...], wd_ref[...], preferred_element_type=F32)
    gate = jax.nn.sigmoid(jnp.dot(_rms(x2, np_ref[...]).astype(BF16), wg_ref[...], preferred_element_type=F32))
    pe = jnp.dot(p_ref[0, 0].astype(BF16), wp_ref[...], preferred_element_type=F32)
    o_ref[0] = x2 + gate * pe


def _post_mixer(x, p, layer, nf, wu, cw, cb, wd, npl, wg, wp):
    bsz, s, d = x.shape
    tm = min(TM_POST, s)
    ff = wd.shape[1]
    stacked = (nf, wu, cw, cb, wd, npl, wg, wp)
    return pl.pallas_call(
        _post_kernel,
        out_shape=jax.ShapeDtypeStruct((bsz, s, d), F32),
        grid=(bsz, s // tm),
        in_specs=[pl.BlockSpec((1, tm, d), lambda i, j: (i, j, 0)),
                  pl.BlockSpec((1, 1, tm, p.shape[-1]), lambda i, j: (layer, i, j, 0))]
                 + [_layer_block(a, layer) for a in stacked],
        out_specs=pl.BlockSpec((1, tm, d), lambda i, j: (i, j, 0)),
        scratch_shapes=[pltpu.VMEM((ff // LANES, SUBLANES + tm, LANES), F32),
                        pltpu.VMEM((ff // LANES, SUBLANES, LANES), F32), pltpu.VMEM((tm, ff), BF16)],
        compiler_params=_params(2),
        name=f"post_mixer_{layer}",
    )(x, p, *stacked)


def _block_diag(w):
    n, r, _ = w.shape
    eye = jnp.eye(n, dtype=w.dtype)
    return (eye[:, None, :, None] * w[:, :, None, :]).reshape(n * r, n * r)


def kernel(x, p, positions, norm_mix, norm_ffn, norm_ple, hy_w_in, hy_b_in, rg_conv_w, rg_conv_b, rg_w_a, rg_b_a, rg_w_x, rg_b_x, rg_lambda, ml_conv_w, ml_conv_b, ml_norm, hy_w_out, at_w_qkv, at_q_norm, at_k_norm, at_sinks, at_w_out, ff_w_up, ff_conv_w, ff_conv_b, ff_w_down, ple_w_gate, ple_w_proj):
    depth = p.shape[0]
    row = lambda v: v.astype(F32)[None, :]
    post = (norm_ffn.astype(F32)[:, None, :], ff_w_up.astype(BF16), ff_conv_w.astype(F32),
            ff_conv_b.astype(F32)[:, None, :], ff_w_down.astype(BF16), norm_ple.astype(F32)[:, None, :],
            ple_w_gate.astype(BF16), ple_w_proj.astype(BF16))
    for layer in range(depth):
        if layer % 2 == 0:
            e = layer // 2
            pad = LANES - 2 * ML_HEADS
            x = _hybrid_mixer(x, row(norm_mix[layer]), jnp.pad(hy_w_in[e], ((0, 0), (0, pad))).astype(BF16),
                              row(jnp.pad(hy_b_in[e], (0, pad))),
                              rg_conv_w[e], row(rg_conv_b[e]), _block_diag(rg_w_a[e]).astype(BF16), row(rg_b_a[e]),
                              _block_diag(rg_w_x[e]).astype(BF16), row(rg_b_x[e]), row(rg_lambda[e]),
                              ml_conv_w[e], row(ml_conv_b[e]), row(ml_norm[e]), hy_w_out[e].astype(BF16))
        else:
            o = layer // 2
            x = _attn_mixer(x, row(norm_mix[layer]), at_w_qkv[o].astype(BF16), positions, at_q_norm[o],
                            at_k_norm[o], at_sinks[o], at_w_out[o].astype(BF16))
        x = _post_mixer(x, p, layer, *post)
    return x
```

```python
import jax
import jax.numpy as jnp
import numpy as np
from jax import lax
from jax.experimental import pallas as pl
from jax.experimental.pallas import tpu as pltpu

F32 = jnp.float32
BF16 = jnp.bfloat16

RG_WIDTH = 512
RG_C = 8.0
ML_HEADS = 4
ML_HEAD_DIM = 128
ML_WIDTH = ML_HEADS * ML_HEAD_DIM
ML_CHUNK = 128
AT_HEADS = 16
AT_KV_HEADS = 4
AT_HEAD_DIM = 64
WINDOW = 128
ROPE_THETA = 10000.0
EPS = 1e-6
LOG2_E = 1.4426950408889634

LANES = 128
SUBLANES = 8
VMEM_LIMIT_BYTES = 56 * 1024 * 1024

TS_MIX = 512
TM_POST = 512
PROJ_CHUNK = 512
RG_SLAB0 = 0
Q_SLAB0 = RG_WIDTH // LANES
K_SLAB0 = Q_SLAB0 + ML_WIDTH // LANES
N_CONV_SLAB = K_SLAB0 + ML_WIDTH // LANES


def _params(n_axes):
    return pltpu.CompilerParams(dimension_semantics=("arbitrary",) * n_axes,
                                vmem_limit_bytes=VMEM_LIMIT_BYTES)


def _resident(shape, index=None):
    n = len(shape)
    idx = (0,) * n if index is None else tuple(index)
    return pl.BlockSpec(shape, lambda *_: idx, pipeline_mode=pl.Buffered(1))


def _layer_block(arr, layer):
    shape = (None,) + arr.shape[1:]
    return _resident(shape, (layer,) + (0,) * (arr.ndim - 1))


def _rms(x, g):
    return x * lax.rsqrt(jnp.mean(x * x, axis=-1, keepdims=True) + EPS) * g


def _slab_conv(zs_ref, slab0, n_slab, r0, n_rows, w_ref, b_ref):
    k_w = w_ref.shape[0]
    base = SUBLANES + r0
    outs = []
    for s in range(n_slab):
        cs = slice(s * LANES, (s + 1) * LANES)
        acc = zs_ref[slab0 + s, base:base + n_rows, :] * w_ref[k_w - 1:k_w, cs] + b_ref[:, cs]
        for j in range(k_w - 1):
            d = k_w - 1 - j
            acc = acc + zs_ref[slab0 + s, base - d:base - d + n_rows, :] * w_ref[j:j + 1, cs]
        outs.append(acc)
    return jnp.concatenate(outs, axis=1)


def _log_sigmoid(x):
    return jnp.minimum(x, 0.0) - jnp.log1p(jnp.exp(-jnp.abs(x)))


def _rg_gates(zs_ref, cw_ref, cb_ref, wa_ref, ba_ref, wx_ref, bx_ref, lam_ref, a_ref, u_ref):
    ts = a_ref.shape[0]
    xc = _slab_conv(zs_ref, RG_SLAB0, RG_WIDTH // LANES, 0, ts, cw_ref, cb_ref)
    xb = xc.astype(BF16)
    r = jax.nn.sigmoid(jnp.dot(xb, wa_ref[...], preferred_element_type=F32) + ba_ref[...])
    i = jax.nn.sigmoid(jnp.dot(xb, wx_ref[...], preferred_element_type=F32) + bx_ref[...])
    lam = lam_ref[...]
    softplus_neg_lam = jnp.maximum(-lam, 0.0) + jnp.log1p(jnp.exp(-jnp.abs(lam)))
    log_a = (-RG_C) * r * softplus_neg_lam
    a = jnp.exp(log_a)
    a_ref[...] = a
    v = 1.0 - a * a
    u_ref[...] = jnp.where(v > 0.0, v * lax.rsqrt(v), 0.0) * (i * xc)


def _rg_scan(z_ref, y_ref, hc_ref, a_ref, u_ref):
    ts = z_ref.shape[0]
    row = lax.broadcasted_iota(jnp.int32, (SUBLANES, RG_WIDTH), 0)
    hc = hc_ref[...]
    for k in range(ts // SUBLANES):
        rows = slice(k * SUBLANES, (k + 1) * SUBLANES)
        a = a_ref[rows, :]
        u = u_ref[rows, :]
        for d in (1, 2, 4):
            keep = row >= d
            u = u + a * jnp.where(keep, pltpu.roll(u, d, 0), 0.0)
            a = a * jnp.where(keep, pltpu.roll(a, d, 0), 1.0)
        h = u + a * hc
        u_ref[rows, :] = h
        hc = jnp.broadcast_to(h[SUBLANES - 1:SUBLANES, :], (SUBLANES, RG_WIDTH))
    hc_ref[...] = hc
    gate = jax.nn.gelu(z_ref[:, RG_WIDTH:2 * RG_WIDTH], approximate=True)
    y_ref[:, 0:RG_WIDTH] = (u_ref[...] * gate).astype(y_ref.dtype)


def _lane_col(x, h):
    return jnp.broadcast_to(x[:, h:h + 1], x.shape)


def _ml_chunk(rows, z_ref, zs_ref, gt_ref, cw_ref, cb_ref, nw_ref, y_ref, c_ref, n_ref, m_ref, after_dots=None):
    lc = ML_CHUNK
    dh = ML_HEAD_DIM
    q0 = 2 * RG_WIDTH
    k0, v0, o0 = q0 + ML_WIDTH, q0 + 2 * ML_WIDTH, q0 + 3 * ML_WIDTH
    row = lax.broadcasted_iota(jnp.int32, (lc, LANES), 0)

    n_sl = ML_WIDTH // LANES
    q = jax.nn.silu(_slab_conv(zs_ref, Q_SLAB0, n_sl, rows.start, lc, cw_ref.at[:, :ML_WIDTH], cb_ref.at[:, :ML_WIDTH]))
    k = jax.nn.silu(_slab_conv(zs_ref, K_SLAB0, n_sl, rows.start, lc, cw_ref.at[:, ML_WIDTH:], cb_ref.at[:, ML_WIDTH:]))
    k = k * (dh ** -0.5)

    gates = gt_ref[rows, :]
    b = _log_sigmoid(gates)
    d = 1
    while d < lc:
        b = b + jnp.where(row >= d, pltpu.roll(b, d, 0), 0.0)
        d *= 2
    b = pltpu.roll(b, LANES - ML_HEADS, 1)
    r = gates - b
    b_last = b[lc - 1:lc, :]
    g_loc = b_last + r
    m_loc = jnp.max(g_loc, axis=0, keepdims=True)
    w_loc = jnp.exp(g_loc - m_loc)
    m_st = m_ref[0:1, :]
    m_new = jnp.maximum(b_last + m_st, m_loc)
    a_prev = jnp.exp(b_last + m_st - m_new)
    a_loc = jnp.exp(m_loc - m_new)
    rmax = r
    d = 1
    while d < lc:
        rmax = jnp.maximum(rmax, jnp.where(row >= d, pltpu.roll(rmax, d, 0), -jnp.inf))
        d *= 2
    e = -jnp.maximum(m_st, rmax)
    a_inter = jnp.exp(m_st + e)
    exp_neg_mt = jnp.exp(e - b)
    r_t = r.T

    heads = [slice(h * dh, (h + 1) * dh) for h in range(ML_HEADS)]
    wide = lambda f: jnp.concatenate([f(h) for h in range(ML_HEADS)], axis=1)
    nt = (((1,), (1,)), ((), ()))
    row_w = lax.broadcasted_iota(jnp.int32, (lc, ML_WIDTH), 0)
    col_w = lax.broadcasted_iota(jnp.int32, (lc, ML_WIDTH), 1)
    causal = row_w >= (col_w % dh)
    qb = q.astype(BF16)
    kb = k.astype(BF16)
    vb = z_ref[rows, v0:o0].astype(BF16)
    c_prev = [c_ref[h] for h in range(ML_HEADS)]
    n_prev = wide(lambda h: n_ref[h:h + 1, :])
    s = wide(lambda h: lax.dot_general(qb[:, heads[h]], kb[:, heads[h]], nt, preferred_element_type=F32))
    w_intra = jnp.where(causal, jnp.exp(wide(lambda h: _lane_col(e, h)) + wide(lambda h: r_t[h:h + 1, :])), 0.0)
    s_qk = s * w_intra
    sb = s_qk.astype(BF16)
    a_in = wide(lambda h: _lane_col(a_inter, h))
    num = (wide(lambda h: jnp.dot(sb[:, heads[h]], vb[:, heads[h]], preferred_element_type=F32))
           + a_in * wide(lambda h: jnp.dot(qb[:, heads[h]], c_prev[h].astype(BF16), preferred_element_type=F32)))
    if after_dots is not None:
        after_dots()
    t = s_qk + a_in * (q * n_prev)
    den = wide(lambda h: jnp.broadcast_to(jnp.sum(t[:, heads[h]], axis=-1, keepdims=True), (lc, dh)))
    hm = num / jnp.maximum(jnp.abs(den), wide(lambda h: _lane_col(exp_neg_mt, h)))
    ms = wide(lambda h: jnp.broadcast_to(jnp.mean(hm[:, heads[h]] * hm[:, heads[h]], axis=-1, keepdims=True), (lc, dh)))
    hm = hm * lax.rsqrt(ms + EPS) * nw_ref[...]
    y = jax.nn.sigmoid(z_ref[rows, o0:o0 + ML_WIDTH]) * hm
    y_ref[rows, RG_WIDTH:RG_WIDTH + ML_WIDTH] = y.astype(y_ref.dtype)

    kw = k * wide(lambda h: _lane_col(w_loc, h))
    for h in range(ML_HEADS):
        c_loc = jnp.dot(kw[:, heads[h]].T.astype(BF16), vb[:, heads[h]], preferred_element_type=F32)
        n_loc = jnp.sum(kw[:, heads[h]], axis=0, keepdims=True)
        ap = a_prev[:, h:h + 1]
        al = a_loc[:, h:h + 1]
        c_ref[h] = ap * c_prev[h] + al * c_loc
        n_ref[h:h + 1, :] = ap * n_ref[h:h + 1, :] + al * n_loc
    m_ref[...] = jnp.broadcast_to(m_new, m_ref.shape)


def _hybrid_kernel(x_ref, xn_ref, nm_ref, win_ref, bin_ref,
                   rcw_ref, rcb_ref, wa_ref, ba_ref, wx_ref, bx_ref, lam_ref,
                   mcw_ref, mcb_ref, mnw_ref, wo_ref, o_ref,
                   za_ref, zb_ref, zsa_ref, zsb_ref, gta_ref, gtb_ref, y_ref, tail_ref, hc_ref, a_ref, u_ref,
                   c_ref, n_ref, m_ref):
    step = pl.program_id(0) * pl.num_programs(1) + pl.program_id(1)

    @pl.when(pl.program_id(1) == 0)
    def _():
        for ref in (tail_ref, hc_ref, c_ref, n_ref):
            ref[...] = jnp.zeros_like(ref)
        m_ref[...] = jnp.full_like(m_ref, -jnp.inf)

    n_main = 2 * RG_WIDTH + 4 * ML_WIDTH
    n_piece = n_main // PROJ_CHUNK + 1
    ts = x_ref.shape[1]
    conv_slab = {0: RG_SLAB0, 2 * RG_WIDTH // PROJ_CHUNK: Q_SLAB0, (2 * RG_WIDTH + ML_WIDTH) // PROJ_CHUNK: K_SLAB0}

    def project(src_ref, z_ref, zs_ref, gt_ref):
        h = _rms(src_ref[0], nm_ref[...]).astype(BF16)

        def piece(c):
            if c < n_main // PROJ_CHUNK:
                cs = slice(c * PROJ_CHUNK, (c + 1) * PROJ_CHUNK)
                res = jnp.dot(h, win_ref[:, cs], preferred_element_type=F32) + bin_ref[:, cs]
                if c in conv_slab:
                    for sl in range(PROJ_CHUNK // LANES):
                        zs_ref[conv_slab[c] + sl, SUBLANES:SUBLANES + ts, :] = res[:, sl * LANES:(sl + 1) * LANES]
                else:
                    z_ref[:, cs] = res
            else:
                gs = slice(n_main, n_main + LANES)
                gt_ref[...] = jnp.dot(h, win_ref[:, gs], preferred_element_type=F32) + bin_ref[:, gs]
        return piece

    @pl.when(step == 0)
    def _():
        piece = project(x_ref, za_ref, zsa_ref, gta_ref)
        for c in range(n_piece):
            piece(c)

    def mix(z_ref, zs_ref, gt_ref, zn_ref, zsn_ref, gtn_ref):
        piece = project(xn_ref, zn_ref, zsn_ref, gtn_ref)
        n_chunk = ts // ML_CHUNK
        zs_ref[:, 0:SUBLANES, :] = tail_ref[...]
        piece(0)
        _rg_gates(zs_ref, rcw_ref, rcb_ref, wa_ref, ba_ref, wx_ref, bx_ref, lam_ref, a_ref, u_ref)
        piece(1)
        piece(2)
        _rg_scan(z_ref, y_ref, hc_ref, a_ref, u_ref)
        nxt = 3
        for c in range(n_chunk):
            rows = slice(c * ML_CHUNK, (c + 1) * ML_CHUNK)
            mid = (lambda k=nxt: piece(k)) if nxt < n_piece else None
            nxt += 1
            _ml_chunk(rows, z_ref, zs_ref, gt_ref, mcw_ref, mcb_ref, mnw_ref, y_ref, c_ref, n_ref, m_ref, mid)
            o_ref[0, rows, :] = x_ref[0, rows, :] + jnp.dot(y_ref[rows, :], wo_ref[...], preferred_element_type=F32)
        for c in range(nxt, n_piece):
            piece(c)
        tail_ref[...] = zs_ref[:, ts:ts + SUBLANES, :]

    @pl.when(step % 2 == 0)
    def _():
        mix(za_ref, zsa_ref, gta_ref, zb_ref, zsb_ref, gtb_ref)

    @pl.when(step % 2 == 1)
    def _():
        mix(zb_ref, zsb_ref, gtb_ref, za_ref, zsa_ref, gta_ref)


def _next_block_map(bsz, n_blk):
    def index(i, j):
        f = jnp.minimum(i * n_blk + j + 1, bsz * n_blk - 1)
        return f // n_blk, f % n_blk, 0
    return index


def _hybrid_mixer(x, nm, w_in, b_in, rcw, rcb, wa, ba, wx, bx, lam, mcw, mcb, mnw, wo):
    bsz, s, d = x.shape
    ts = min(TS_MIX, s)
    n_main = w_in.shape[1] - LANES
    rw, mw = RG_WIDTH, ML_WIDTH
    consts = (nm, w_in, b_in, rcw, rcb, wa, ba, wx, bx, lam, mcw, mcb, mnw, wo)
    return pl.pallas_call(
        _hybrid_kernel,
        out_shape=jax.ShapeDtypeStruct((bsz, s, d), F32),
        grid=(bsz, s // ts),
        in_specs=[pl.BlockSpec((1, ts, d), lambda i, j: (i, j, 0)),
                  pl.BlockSpec((1, ts, d), _next_block_map(bsz, s // ts))] + [_resident(c.shape) for c in consts],
        out_specs=pl.BlockSpec((1, ts, d), lambda i, j: (i, j, 0)),
        scratch_shapes=[pltpu.VMEM((ts, n_main), F32), pltpu.VMEM((ts, n_main), F32),
                        pltpu.VMEM((N_CONV_SLAB, SUBLANES + ts, LANES), F32),
                        pltpu.VMEM((N_CONV_SLAB, SUBLANES + ts, LANES), F32),
                        pltpu.VMEM((ts, LANES), F32), pltpu.VMEM((ts, LANES), F32),
                        pltpu.VMEM((ts, rw + mw), BF16),
                        pltpu.VMEM((N_CONV_SLAB, SUBLANES, LANES), F32), pltpu.VMEM((SUBLANES, rw), F32),
                        pltpu.VMEM((ts, rw), F32), pltpu.VMEM((ts, rw), F32),
                        pltpu.VMEM((ML_HEADS, ML_HEAD_DIM, ML_HEAD_DIM), F32),
                        pltpu.VMEM((SUBLANES, ML_HEAD_DIM), F32),
                        pltpu.VMEM((SUBLANES, LANES), F32)],
        compiler_params=_params(2),
        name="hybrid_mixer",
    )(x, x, *consts)


def _attn_kernel(x_ref, xn_ref, nm_ref, wqkv_ref, pos_ref, freq_ref, qn_ref, kn_ref, seg_ref, sink_ref, wo_ref,
                 o_ref, qkva_ref, qkvb_ref, qs_ref, kf_ref, vt_ref, att_ref):
    tq = x_ref.shape[1]
    w = WINDOW
    hd = AT_HEAD_DIM
    hh = hd // 2
    n_sub = tq // w
    qw = AT_HEADS * hd
    kw = AT_KV_HEADS * hd
    n_qslab = qw // LANES
    n_kslab = kw // LANES
    n_pack = LANES // hh
    rp = w // n_pack
    first = pl.program_id(1) == 0
    step = pl.program_id(0) * pl.num_programs(1) + pl.program_id(1)

    @pl.when(first)
    def _():
        kf_ref[:, :, 0:w, :] = jnp.zeros((AT_KV_HEADS, 2, w, LANES), kf_ref.dtype)
        vt_ref[:, :, 0:w] = jnp.zeros((AT_KV_HEADS, hd, w), vt_ref.dtype)

    lane = lax.broadcasted_iota(jnp.int32, (w, LANES), 1)
    slot_a = (lane // hh) % 2 == 0
    sin_sign = jnp.where(lane < hd, -1.0, 1.0)
    lane_grp = lax.broadcasted_iota(jnp.int32, (rp, LANES), 1) // hh
    seg = seg_ref[...]
    n_piece = (qw + 2 * kw) // PROJ_CHUNK

    def project(src_ref, dst_ref):
        h = _rms(src_ref[0], nm_ref[...]).astype(BF16)

        def piece(c):
            cs = slice(c * PROJ_CHUNK, (c + 1) * PROJ_CHUNK)
            dst_ref[:, cs] = jnp.dot(h, wqkv_ref[:, cs], preferred_element_type=F32)
        return piece

    @pl.when(step == 0)
    def _():
        piece = project(x_ref, qkva_ref)
        for c in range(n_piece):
            piece(c)

    def norm_rope(t, gain, cos, sin):
        ms = jnp.dot((t * t).astype(BF16), seg, preferred_element_type=F32) * (1.0 / hd)
        tn = t * lax.rsqrt(ms + EPS) * gain
        return tn * cos + pltpu.roll(tn, hd, 1) * sin

    def spread(t):
        parts = []
        for g in range(n_pack):
            z = jnp.where(lane_grp == g, t, 0.0)
            y = z
            for k in range(1, n_pack):
                y = y + pltpu.roll(z, k * hh, 1)
            parts.append(y)
        return jnp.concatenate(parts, axis=0)

    def prepare(i, qkv_ref):
        rows = slice(i * w, (i + 1) * w)
        krows = slice(w + i * w, w + (i + 1) * w)
        pos = pos_ref[0, 0, i:i + 1, :].astype(F32)
        pos_col = jnp.broadcast_to(pos, (w, LANES)).T
        packed = pos_col[0:rp]
        for g in range(1, n_pack):
            packed = jnp.where(lane_grp == g, pos_col[g * rp:(g + 1) * rp], packed)
        ang = packed * freq_ref[...]
        cos = spread(jnp.cos(ang))
        sin = spread(jnp.sin(ang)) * sin_sign
        for sl in range(n_qslab):
            xq = norm_rope(qkv_ref[rows, sl * LANES:(sl + 1) * LANES], qn_ref[...], cos, sin) * (LOG2_E * hd ** -0.5)
            qs_ref[sl, rows, :] = xq.astype(qs_ref.dtype)
        for sl in range(n_kslab):
            xk = norm_rope(qkv_ref[rows, qw + sl * LANES:qw + (sl + 1) * LANES], kn_ref[...], cos, sin)
            kf_ref[2 * sl, 0, krows, :] = jnp.where(slot_a, xk, 0.0).astype(kf_ref.dtype)
            kf_ref[2 * sl, 1, krows, :] = jnp.where(slot_a, 0.0, pltpu.roll(xk, hh, 1)).astype(kf_ref.dtype)
            kf_ref[2 * sl + 1, 0, krows, :] = jnp.where(slot_a, pltpu.roll(xk, LANES - hh, 1), 0.0).astype(kf_ref.dtype)
            kf_ref[2 * sl + 1, 1, krows, :] = jnp.where(slot_a, 0.0, xk).astype(kf_ref.dtype)
            v0 = qw + kw + sl * LANES
            vt = qkv_ref[rows, v0:v0 + LANES].T.astype(vt_ref.dtype)
            vt_ref[2 * sl, :, krows] = vt[:hd]
            vt_ref[2 * sl + 1, :, krows] = vt[hd:]

    grp = AT_HEADS // AT_KV_HEADS
    row_a = lax.broadcasted_iota(jnp.int32, (w, AT_HEADS * w), 0)
    lane_a = lax.broadcasted_iota(jnp.int32, (w, AT_HEADS * w), 1)
    from_prev = row_a > (lane_a % w)
    nt = (((1,), (1,)), ((), ()))

    def attend(i, after_scores=None):
        rows = slice(i * w, (i + 1) * w)
        win = slice(i * w, (i + 2) * w)
        fp = from_prev[:, :2 * w]
        parts = []
        for hk in range(AT_KV_HEADS):
            q2 = jnp.concatenate([qs_ref[2 * hk, rows, :], qs_ref[2 * hk + 1, rows, :]], axis=0)
            for half in range(2):
                sp = lax.dot_general(kf_ref[hk, half, win, :], q2, nt, preferred_element_type=F32)
                s_prev = sp[:w]
                if i == 0:
                    s_prev = s_prev + jnp.where(first, -jnp.inf, 0.0)
                parts.append(jnp.where(fp, s_prev, sp[w:]))
        logits = jnp.concatenate(parts, axis=1)
        if after_scores is not None:
            after_scores()
        sink = sink_ref[...]
        m = jnp.maximum(jnp.max(logits, axis=0, keepdims=True), sink)
        p = jnp.exp2(logits - m)
        inv = 1.0 / (jnp.sum(p, axis=0, keepdims=True) + jnp.exp2(sink - m))
        pb = p.astype(BF16)
        zero = jnp.zeros_like(pb)
        pp = jnp.concatenate([jnp.where(from_prev, pb, zero), jnp.where(from_prev, zero, pb)], axis=0)
        for hk in range(AT_KV_HEADS):
            cols = slice(hk * grp * w, (hk + 1) * grp * w)
            r = jnp.dot(vt_ref[hk, :, win], pp[:, cols], preferred_element_type=F32) * inv[:, cols]
            for pair in range(grp // 2):
                out_t = jnp.concatenate([r[:, pair * w:(pair + 1) * w],
                                         r[:, (2 + pair) * w:(3 + pair) * w]], axis=0)
                sl = 2 * hk + pair
                att_ref[rows, sl * LANES:(sl + 1) * LANES] = out_t.T.astype(att_ref.dtype)

    def mix(qkv_ref, qkvn_ref):
        piece = project(xn_ref, qkvn_ref)
        piece(0)
        for i in range(n_sub):
            prepare(i, qkv_ref)
        def out_proj(half):
            rows = slice(half * (tq // 2), (half + 1) * (tq // 2))
            o_ref[0, rows, :] = x_ref[0, rows, :] + jnp.dot(att_ref[rows, :], wo_ref[...],
                                                            preferred_element_type=F32)

        fillers = [lambda c=c: piece(c) for c in range(1, n_piece)]
        for i in range(n_sub):
            if i == n_sub // 2:
                fillers.insert(0, lambda: out_proj(0))
            attend(i, fillers.pop(0) if fillers else None)
        for f in fillers:
            f()
        out_proj(1)
        kf_ref[:, :, 0:w, :] = kf_ref[:, :, tq:tq + w, :]
        vt_ref[:, :, 0:w] = vt_ref[:, :, tq:tq + w]

    @pl.when(step % 2 == 0)
    def _():
        mix(qkva_ref, qkvb_ref)

    @pl.when(step % 2 == 1)
    def _():
        mix(qkvb_ref, qkva_ref)


def _slab_lane_dims():
    hh = AT_HEAD_DIM // 2
    lane = np.arange(LANES)
    return (lane // hh) % 2, lane % hh + hh * (lane // AT_HEAD_DIM)


def _attn_mixer(x, nm, w_qkv, positions, q_norm, k_norm, sinks, wo):
    bsz, s, d = x.shape
    tq = min(TS_MIX, s)
    assert tq % (2 * WINDOW) == 0 and s % tq == 0, (s, tq)
    hd = AT_HEAD_DIM
    qw, kw = AT_HEADS * hd, AT_KV_HEADS * hd
    grp = AT_HEADS // AT_KV_HEADS
    inv_freq = ROPE_THETA ** (-jnp.arange(hd // 2, dtype=F32) * (2.0 / hd))
    freq = jnp.tile(inv_freq, LANES // (hd // 2))[None, :]
    head_in_slab, dim = _slab_lane_dims()
    seg = jnp.asarray((head_in_slab[:, None] == head_in_slab[None, :]).astype(np.float32), dtype=BF16)
    n_slab = (qw + kw) // LANES
    w_qk = w_qkv[:, :qw + kw].reshape(d, n_slab, 2, 2, hd // 2).transpose(0, 1, 3, 2, 4).reshape(d, qw + kw)
    w_qkv = jnp.concatenate([w_qk, w_qkv[:, qw + kw:]], axis=1)
    qn = q_norm.astype(F32)[dim][None, :]
    kn = k_norm.astype(F32)[dim][None, :]
    pos = positions.reshape(bsz, s // tq, tq // WINDOW, WINDOW)
    sk = (sinks.astype(F32) * LOG2_E).reshape(AT_KV_HEADS, grp)[:, np.array([0, 2, 1, 3])]
    sk = jnp.repeat(sk, WINDOW, axis=1).reshape(1, AT_HEADS * WINDOW)
    consts_a = (nm, w_qkv)
    consts_b = (freq, qn, kn, seg, sk, wo)
    return pl.pallas_call(
        _attn_kernel,
        out_shape=jax.ShapeDtypeStruct((bsz, s, d), F32),
        grid=(bsz, s // tq),
        in_specs=([pl.BlockSpec((1, tq, d), lambda i, j: (i, j, 0)),
                   pl.BlockSpec((1, tq, d), _next_block_map(bsz, s // tq))] + [_resident(c.shape) for c in consts_a]
                  + [pl.BlockSpec((1, 1, tq // WINDOW, WINDOW), lambda i, j: (i, j, 0, 0))]
                  + [_resident(c.shape) for c in consts_b]),
        out_specs=pl.BlockSpec((1, tq, d), lambda i, j: (i, j, 0)),
        scratch_shapes=[pltpu.VMEM((tq, qw + 2 * kw), F32), pltpu.VMEM((tq, qw + 2 * kw), F32),
                        pltpu.VMEM((qw // LANES, tq, LANES), BF16),
                        pltpu.VMEM((AT_KV_HEADS, 2, tq + WINDOW, LANES), BF16),
                        pltpu.VMEM((AT_KV_HEADS, hd, tq + WINDOW), BF16),
                        pltpu.VMEM((tq, qw), BF16)],
        compiler_params=_params(2),
        name="attn_mixer",
    )(x, x, *consts_a, pos, *consts_b)


def _post_kernel(x_ref, p_ref, nf_ref, wu_ref, cw_ref, cb_ref, wd_ref, np_ref, wg_ref, wp_ref, o_ref,
                 gs_ref, gtail_ref, act_ref):
    tm = x_ref.shape[1]
    ff = wd_ref.shape[0]
    n_slab = ff // LANES

    @pl.when(pl.program_id(1) == 0)
    def _():
        gtail_ref[...] = jnp.zeros_like(gtail_ref)

    x1 = x_ref[0]
    h = _rms(x1, nf_ref[...]).astype(BF16)
    gs_ref[:, 0:SUBLANES, :] = gtail_ref[...]
    g = jnp.dot(h, wu_ref[:, 0:ff], preferred_element_type=F32)
    for sl in range(n_slab):
        gs_ref[sl, SUBLANES:SUBLANES + tm, :] = g[:, sl * LANES:(sl + 1) * LANES]
    u = jnp.dot(h, wu_ref[:, ff:2 * ff], preferred_element_type=F32)
    gc = _slab_conv(gs_ref, 0, n_slab, 0, tm, cw_ref, cb_ref)
    gtail_ref[...] = gs_ref[:, tm:tm + SUBLANES, :]
    act_ref[...] = (jax.nn.gelu(gc, approximate=True) * u).astype(act_ref.dtype)
    x2 = x1 + jnp.dot(act_ref[...], wd_ref[...], preferred_element_type=F32)
    gate = jax.nn.sigmoid(jnp.dot(_rms(x2, np_ref[...]).astype(BF16), wg_ref[...], preferred_element_type=F32))
    pe = jnp.dot(p_ref[0, 0].astype(BF16), wp_ref[...], preferred_element_type=F32)
    o_ref[0] = x2 + gate * pe


def _post_mixer(x, p, layer, nf, wu, cw, cb, wd, npl, wg, wp):
    bsz, s, d = x.shape
    tm = min(TM_POST, s)
    ff = wd.shape[1]
    stacked = (nf, wu, cw, cb, wd, npl, wg, wp)
    return pl.pallas_call(
        _post_kernel,
        out_shape=jax.ShapeDtypeStruct((bsz, s, d), F32),
        grid=(bsz, s // tm),
        in_specs=[pl.BlockSpec((1, tm, d), lambda i, j: (i, j, 0)),
                  pl.BlockSpec((1, 1, tm, p.shape[-1]), lambda i, j: (layer, i, j, 0))]
                 + [_layer_block(a, layer) for a in stacked],
        out_specs=pl.BlockSpec((1, tm, d), lambda i, j: (i, j, 0)),
        scratch_shapes=[pltpu.VMEM((ff // LANES, SUBLANES + tm, LANES), F32),
                        pltpu.VMEM((ff // LANES, SUBLANES, LANES), F32), pltpu.VMEM((tm, ff), BF16)],
        compiler_params=_params(2),
        name=f"post_mixer_{layer}",
    )(x, p, *stacked)


def _block_diag(w):
    n, r, _ = w.shape
    eye = jnp.eye(n, dtype=w.dtype)
    return (eye[:, None, :, None] * w[:, :, None, :]).reshape(n * r, n * r)


def kernel(x, p, positions, norm_mix, norm_ffn, norm_ple, hy_w_in, hy_b_in, rg_conv_w, rg_conv_b, rg_w_a, rg_b_a, rg_w_x, rg_b_x, rg_lambda, ml_conv_w, ml_conv_b, ml_norm, hy_w_out, at_w_qkv, at_q_norm, at_k_norm, at_sinks, at_w_out, ff_w_up, ff_conv_w, ff_conv_b, ff_w_down, ple_w_gate, ple_w_proj):
    depth = p.shape[0]
    row = lambda v: v.astype(F32)[None, :]
    post = (norm_ffn.astype(F32)[:, None, :], ff_w_up.astype(BF16), ff_conv_w.astype(F32),
            ff_conv_b.astype(F32)[:, None, :], ff_w_down.astype(BF16), norm_ple.astype(F32)[:, None, :],
            ple_w_gate.astype(BF16), ple_w_proj.astype(BF16))
    for layer in range(depth):
        if layer % 2 == 0:
            e = layer // 2
            pad = LANES - 2 * ML_HEADS
            x = _hybrid_mixer(x, row(norm_mix[layer]), jnp.pad(hy_w_in[e], ((0, 0), (0, pad))).astype(BF16),
                              row(jnp.pad(hy_b_in[e], (0, pad))),
                              rg_conv_w[e], row(rg_conv_b[e]), _block_diag(rg_w_a[e]).astype(BF16), row(rg_b_a[e]),
                              _block_diag(rg_w_x[e]).astype(BF16), row(rg_b_x[e]), row(rg_lambda[e]),
                              ml_conv_w[e], row(ml_conv_b[e]), row(ml_norm[e]), hy_w_out[e].astype(BF16))
        else:
            o = layer // 2
            x = _attn_mixer(x, row(norm_mix[layer]), at_w_qkv[o].astype(BF16), positions, at_q_norm[o],
                            at_k_norm[o], at_sinks[o], at_w_out[o].astype(BF16))
        x = _post_mixer(x, p, layer, *post)
    return x
```

```python
import jax
import jax.numpy as jnp
import numpy as np
from jax import lax
from jax.experimental import pallas as pl
from jax.experimental.pallas import tpu as pltpu

F32 = jnp.float32
BF16 = jnp.bfloat16

RG_WIDTH = 512
RG_C = 8.0
ML_HEADS = 4
ML_HEAD_DIM = 128
ML_WIDTH = ML_HEADS * ML_HEAD_DIM
ML_CHUNK = 128
AT_HEADS = 16
AT_KV_HEADS = 4
AT_HEAD_DIM = 64
WINDOW = 128
ROPE_THETA = 10000.0
EPS = 1e-6
LOG2_E = 1.4426950408889634

LANES = 128
SUBLANES = 8
VMEM_LIMIT_BYTES = 56 * 1024 * 1024

TS_MIX = 512
TM_POST = 512
PROJ_CHUNK = 512
RG_SLAB0 = 0
Q_SLAB0 = RG_WIDTH // LANES
K_SLAB0 = Q_SLAB0 + ML_WIDTH // LANES
N_CONV_SLAB = K_SLAB0 + ML_WIDTH // LANES


def _params(n_axes):
    return pltpu.CompilerParams(dimension_semantics=("arbitrary",) * n_axes,
                                vmem_limit_bytes=VMEM_LIMIT_BYTES)


def _resident(shape, index=None):
    n = len(shape)
    idx = (0,) * n if index is None else tuple(index)
    return pl.BlockSpec(shape, lambda *_: idx, pipeline_mode=pl.Buffered(1))


def _layer_block(arr, layer):
    shape = (None,) + arr.shape[1:]
    return _resident(shape, (layer,) + (0,) * (arr.ndim - 1))


def _rms(x, g):
    return x * lax.rsqrt(jnp.mean(x * x, axis=-1, keepdims=True) + EPS) * g


def _slab_conv(zs_ref, slab0, n_slab, r0, n_rows, w_ref, b_ref):
    k_w = w_ref.shape[0]
    base = SUBLANES + r0
    outs = []
    for s in range(n_slab):
        cs = slice(s * LANES, (s + 1) * LANES)
        acc = zs_ref[slab0 + s, base:base + n_rows, :] * w_ref[k_w - 1:k_w, cs] + b_ref[:, cs]
        for j in range(k_w - 1):
            d = k_w - 1 - j
            acc = acc + zs_ref[slab0 + s, base - d:base - d + n_rows, :] * w_ref[j:j + 1, cs]
        outs.append(acc)
    return jnp.concatenate(outs, axis=1)


def _log_sigmoid(x):
    return jnp.minimum(x, 0.0) - jnp.log1p(jnp.exp(-jnp.abs(x)))


def _rg_gates(zs_ref, cw_ref, cb_ref, wa_ref, ba_ref, wx_ref, bx_ref, lam_ref, a_ref, u_ref):
    ts = a_ref.shape[0]
    xc = _slab_conv(zs_ref, RG_SLAB0, RG_WIDTH // LANES, 0, ts, cw_ref, cb_ref)
    xb = xc.astype(BF16)
    r = jax.nn.sigmoid(jnp.dot(xb, wa_ref[...], preferred_element_type=F32) + ba_ref[...])
    i = jax.nn.sigmoid(jnp.dot(xb, wx_ref[...], preferred_element_type=F32) + bx_ref[...])
    lam = lam_ref[...]
    softplus_neg_lam = jnp.maximum(-lam, 0.0) + jnp.log1p(jnp.exp(-jnp.abs(lam)))
    log_a = (-RG_C) * r * softplus_neg_lam
    a = jnp.exp(log_a)
    a_ref[...] = a
    v = 1.0 - a * a
    u_ref[...] = jnp.where(v > 0.0, v * lax.rsqrt(v), 0.0) * (i * xc)


def _rg_scan(z_ref, y_ref, hc_ref, a_ref, u_ref):
    ts = z_ref.shape[0]
    row = lax.broadcasted_iota(jnp.int32, (SUBLANES, RG_WIDTH), 0)
    hc = hc_ref[...]
    for k in range(ts // SUBLANES):
        rows = slice(k * SUBLANES, (k + 1) * SUBLANES)
        a = a_ref[rows, :]
        u = u_ref[rows, :]
        for d in (1, 2, 4):
            keep = row >= d
            u = u + a * jnp.where(keep, pltpu.roll(u, d, 0), 0.0)
            a = a * jnp.where(keep, pltpu.roll(a, d, 0), 1.0)
        h = u + a * hc
        u_ref[rows, :] = h
        hc = jnp.broadcast_to(h[SUBLANES - 1:SUBLANES, :], (SUBLANES, RG_WIDTH))
    hc_ref[...] = hc
    gate = jax.nn.gelu(z_ref[:, RG_WIDTH:2 * RG_WIDTH], approximate=True)
    y_ref[:, 0:RG_WIDTH] = (u_ref[...] * gate).astype(y_ref.dtype)


def _lane_col(x, h):
    return jnp.broadcast_to(x[:, h:h + 1], x.shape)


def _ml_chunk(rows, z_ref, zs_ref, gt_ref, cw_ref, cb_ref, nw_ref, y_ref, c_ref, n_ref, m_ref, after_dots=None):
    lc = ML_CHUNK
    dh = ML_HEAD_DIM
    q0 = 2 * RG_WIDTH
    k0, v0, o0 = q0 + ML_WIDTH, q0 + 2 * ML_WIDTH, q0 + 3 * ML_WIDTH
    row = lax.broadcasted_iota(jnp.int32, (lc, LANES), 0)

    n_sl = ML_WIDTH // LANES
    q = jax.nn.silu(_slab_conv(zs_ref, Q_SLAB0, n_sl, rows.start, lc, cw_ref.at[:, :ML_WIDTH], cb_ref.at[:, :ML_WIDTH]))
    k = jax.nn.silu(_slab_conv(zs_ref, K_SLAB0, n_sl, rows.start, lc, cw_ref.at[:, ML_WIDTH:], cb_ref.at[:, ML_WIDTH:]))
    k = k * (dh ** -0.5)

    gates = gt_ref[rows, :]
    b = _log_sigmoid(gates)
    d = 1
    while d < lc:
        b = b + jnp.where(row >= d, pltpu.roll(b, d, 0), 0.0)
        d *= 2
    b = pltpu.roll(b, LANES - ML_HEADS, 1)
    r = gates - b
    b_last = b[lc - 1:lc, :]
    g_loc = b_last + r
    m_loc = jnp.max(g_loc, axis=0, keepdims=True)
    w_loc = jnp.exp(g_loc - m_loc)
    m_st = m_ref[0:1, :]
    m_new = jnp.maximum(b_last + m_st, m_loc)
    a_prev = jnp.exp(b_last + m_st - m_new)
    a_loc = jnp.exp(m_loc - m_new)
    rmax = r
    d = 1
    while d < lc:
        rmax = jnp.maximum(rmax, jnp.where(row >= d, pltpu.roll(rmax, d, 0), -jnp.inf))
        d *= 2
    e = -jnp.maximum(m_st, rmax)
    a_inter = jnp.exp(m_st + e)
    exp_neg_mt = jnp.exp(e - b)
    r_t = r.T

    heads = [slice(h * dh, (h + 1) * dh) for h in range(ML_HEADS)]
    wide = lambda f: jnp.concatenate([f(h) for h in range(ML_HEADS)], axis=1)
    nt = (((1,), (1,)), ((), ()))
    row_w = lax.broadcasted_iota(jnp.int32, (lc, ML_WIDTH), 0)
    col_w = lax.broadcasted_iota(jnp.int32, (lc, ML_WIDTH), 1)
    causal = row_w >= (col_w % dh)
    qb = q.astype(BF16)
    kb = k.astype(BF16)
    vb = z_ref[rows, v0:o0].astype(BF16)
    c_prev = [c_ref[h] for h in range(ML_HEADS)]
    n_prev = wide(lambda h: n_ref[h:h + 1, :])
    s = wide(lambda h: lax.dot_general(qb[:, heads[h]], kb[:, heads[h]], nt, preferred_element_type=F32))
    w_intra = jnp.where(causal, jnp.exp(wide(lambda h: _lane_col(e, h)) + wide(lambda h: r_t[h:h + 1, :])), 0.0)
    s_qk = s * w_intra
    sb = s_qk.astype(BF16)
    a_in = wide(lambda h: _lane_col(a_inter, h))
    num = (wide(lambda h: jnp.dot(sb[:, heads[h]], vb[:, heads[h]], preferred_element_type=F32))
           + a_in * wide(lambda h: jnp.dot(qb[:, heads[h]], c_prev[h].astype(BF16), preferred_element_type=F32)))
    if after_dots is not None:
        after_dots()
    t = s_qk + a_in * (q * n_prev)
    den = wide(lambda h: jnp.broadcast_to(jnp.sum(t[:, heads[h]], axis=-1, keepdims=True), (lc, dh)))
    hm = num / jnp.maximum(jnp.abs(den), wide(lambda h: _lane_col(exp_neg_mt, h)))
    ms = wide(lambda h: jnp.broadcast_to(jnp.mean(hm[:, heads[h]] * hm[:, heads[h]], axis=-1, keepdims=True), (lc, dh)))
    hm = hm * lax.rsqrt(ms + EPS) * nw_ref[...]
    y = jax.nn.sigmoid(z_ref[rows, o0:o0 + ML_WIDTH]) * hm
    y_ref[rows, RG_WIDTH:RG_WIDTH + ML_WIDTH] = y.astype(y_ref.dtype)

    kw = k * wide(lambda h: _lane_col(w_loc, h))
    for h in range(ML_HEADS):
        c_loc = jnp.dot(kw[:, heads[h]].T.astype(BF16), vb[:, heads[h]], preferred_element_type=F32)
        n_loc = jnp.sum(kw[:, heads[h]], axis=0, keepdims=True)
        ap = a_prev[:, h:h + 1]
        al = a_loc[:, h:h + 1]
        c_ref[h] = ap * c_prev[h] + al * c_loc
        n_ref[h:h + 1, :] = ap * n_ref[h:h + 1, :] + al * n_loc
    m_ref[...] = jnp.broadcast_to(m_new, m_ref.shape)


def _hybrid_kernel(x_ref, xn_ref, nm_ref, win_ref, bin_ref,
                   rcw_ref, rcb_ref, wa_ref, ba_ref, wx_ref, bx_ref, lam_ref,
                   mcw_ref, mcb_ref, mnw_ref, wo_ref, o_ref,
                   za_ref, zb_ref, zsa_ref, zsb_ref, gta_ref, gtb_ref, y_ref, tail_ref, hc_ref, a_ref, u_ref,
                   c_ref, n_ref, m_ref):
    step = pl.program_id(0) * pl.num_programs(1) + pl.program_id(1)

    @pl.when(pl.program_id(1) == 0)
    def _():
        for ref in (tail_ref, hc_ref, c_ref, n_ref):
            ref[...] = jnp.zeros_like(ref)
        m_ref[...] = jnp.full_like(m_ref, -jnp.inf)

    n_main = 2 * RG_WIDTH + 4 * ML_WIDTH
    n_piece = n_main // PROJ_CHUNK + 1
    ts = x_ref.shape[1]
    conv_slab = {0: RG_SLAB0, 2 * RG_WIDTH // PROJ_CHUNK: Q_SLAB0, (2 * RG_WIDTH + ML_WIDTH) // PROJ_CHUNK: K_SLAB0}

    def project(src_ref, z_ref, zs_ref, gt_ref):
        h = _rms(src_ref[0], nm_ref[...]).astype(BF16)

        def piece(c):
            if c < n_main // PROJ_CHUNK:
                cs = slice(c * PROJ_CHUNK, (c + 1) * PROJ_CHUNK)
                res = jnp.dot(h, win_ref[:, cs], preferred_element_type=F32) + bin_ref[:, cs]
                if c in conv_slab:
                    for sl in range(PROJ_CHUNK // LANES):
                        zs_ref[conv_slab[c] + sl, SUBLANES:SUBLANES + ts, :] = res[:, sl * LANES:(sl + 1) * LANES]
                else:
                    z_ref[:, cs] = res
            else:
                gs = slice(n_main, n_main + LANES)
                gt_ref[...] = jnp.dot(h, win_ref[:, gs], preferred_element_type=F32) + bin_ref[:, gs]
        return piece

    @pl.when(step == 0)
    def _():
        piece = project(x_ref, za_ref, zsa_ref, gta_ref)
        for c in range(n_piece):
            piece(c)

    def mix(z_ref, zs_ref, gt_ref, zn_ref, zsn_ref, gtn_ref):
        piece = project(xn_ref, zn_ref, zsn_ref, gtn_ref)
        n_chunk = ts // ML_CHUNK
        zs_ref[:, 0:SUBLANES, :] = tail_ref[...]
        piece(0)
        _rg_gates(zs_ref, rcw_ref, rcb_ref, wa_ref, ba_ref, wx_ref, bx_ref, lam_ref, a_ref, u_ref)
        piece(1)
        piece(2)
        _rg_scan(z_ref, y_ref, hc_ref, a_ref, u_ref)
        nxt = 3
        for c in range(n_chunk):
            rows = slice(c * ML_CHUNK, (c + 1) * ML_CHUNK)
            mid = (lambda k=nxt: piece(k)) if nxt < n_piece else None
            nxt += 1
            _ml_chunk(rows, z_ref, zs_ref, gt_ref, mcw_ref, mcb_ref, mnw_ref, y_ref, c_ref, n_ref, m_ref, mid)
            o_ref[0, rows, :] = x_ref[0, rows, :] + jnp.dot(y_ref[rows, :], wo_ref[...], preferred_element_type=F32)
        for c in range(nxt, n_piece):
            piece(c)
        tail_ref[...] = zs_ref[:, ts:ts + SUBLANES, :]

    @pl.when(step % 2 == 0)
    def _():
        mix(za_ref, zsa_ref, gta_ref, zb_ref, zsb_ref, gtb_ref)

    @pl.when(step % 2 == 1)
    def _():
        mix(zb_ref, zsb_ref, gtb_ref, za_ref, zsa_ref, gta_ref)


def _next_block_map(bsz, n_blk):
    def index(i, j):
        f = jnp.minimum(i * n_blk + j + 1, bsz * n_blk - 1)
        return f // n_blk, f % n_blk, 0
    return index


def _hybrid_mixer(x, nm, w_in, b_in, rcw, rcb, wa, ba, wx, bx, lam, mcw, mcb, mnw, wo):
    bsz, s, d = x.shape
    ts = min(TS_MIX, s)
    n_main = w_in.shape[1] - LANES
    rw, mw = RG_WIDTH, ML_WIDTH
    consts = (nm, w_in, b_in, rcw, rcb, wa, ba, wx, bx, lam, mcw, mcb, mnw, wo)
    return pl.pallas_call(
        _hybrid_kernel,
        out_shape=jax.ShapeDtypeStruct((bsz, s, d), F32),
        grid=(bsz, s // ts),
        in_specs=[pl.BlockSpec((1, ts, d), lambda i, j: (i, j, 0)),
                  pl.BlockSpec((1, ts, d), _next_block_map(bsz, s // ts))] + [_resident(c.shape) for c in consts],
        out_specs=pl.BlockSpec((1, ts, d), lambda i, j: (i, j, 0)),
        scratch_shapes=[pltpu.VMEM((ts, n_main), F32), pltpu.VMEM((ts, n_main), F32),
                        pltpu.VMEM((N_CONV_SLAB, SUBLANES + ts, LANES), F32),
                        pltpu.VMEM((N_CONV_SLAB, SUBLANES + ts, LANES), F32),
                        pltpu.VMEM((ts, LANES), F32), pltpu.VMEM((ts, LANES), F32),
                        pltpu.VMEM((ts, rw + mw), BF16),
                        pltpu.VMEM((N_CONV_SLAB, SUBLANES, LANES), F32), pltpu.VMEM((SUBLANES, rw), F32),
                        pltpu.VMEM((ts, rw), F32), pltpu.VMEM((ts, rw), F32),
                        pltpu.VMEM((ML_HEADS, ML_HEAD_DIM, ML_HEAD_DIM), F32),
                        pltpu.VMEM((SUBLANES, ML_HEAD_DIM), F32),
                        pltpu.VMEM((SUBLANES, LANES), F32)],
        compiler_params=_params(2),
        name="hybrid_mixer",
    )(x, x, *consts)


def _attn_kernel(x_ref, xn_ref, nm_ref, wqkv_ref, pos_ref, freq_ref, qn_ref, kn_ref, seg_ref, sink_ref, wo_ref,
                 o_ref, qkva_ref, qkvb_ref, qs_ref, kf_ref, vt_ref, att_ref):
    tq = xn_ref.shape[1]
    w = WINDOW
    hd = AT_HEAD_DIM
    hh = hd // 2
    n_sub = tq // w
    qw = AT_HEADS * hd
    kw = AT_KV_HEADS * hd
    n_qslab = qw // LANES
    n_kslab = kw // LANES
    n_pack = LANES // hh
    rp = w // n_pack
    first = pl.program_id(1) == 0
    step = pl.program_id(0) * pl.num_programs(1) + pl.program_id(1)

    @pl.when(first)
    def _():
        kf_ref[:, :, 0:w, :] = jnp.zeros((AT_KV_HEADS, 2, w, LANES), kf_ref.dtype)
        vt_ref[:, :, 0:w] = jnp.zeros((AT_KV_HEADS, hd, w), vt_ref.dtype)

    lane = lax.broadcasted_iota(jnp.int32, (w, LANES), 1)
    slot_a = (lane // hh) % 2 == 0
    sin_sign = jnp.where(lane < hd, -1.0, 1.0)
    lane_grp = lax.broadcasted_iota(jnp.int32, (rp, LANES), 1) // hh
    seg = seg_ref[...]
    n_piece = (qw + 2 * kw) // PROJ_CHUNK

    def project(src, dst_ref):
        h = _rms(src, nm_ref[...]).astype(BF16)

        def piece(c):
            cs = slice(c * PROJ_CHUNK, (c + 1) * PROJ_CHUNK)
            dst_ref[:, cs] = jnp.dot(h, wqkv_ref[:, cs], preferred_element_type=F32)
        return piece

    @pl.when(step == 0)
    def _():
        piece = project(x_ref[0, 0:tq, :], qkva_ref)
        for c in range(n_piece):
            piece(c)

    def norm_rope(t, gain, cos, sin):
        ms = jnp.dot((t * t).astype(BF16), seg, preferred_element_type=F32) * (1.0 / hd)
        tn = t * lax.rsqrt(ms + EPS) * gain
        return tn * cos + pltpu.roll(tn, hd, 1) * sin

    def spread(t):
        parts = []
        for g in range(n_pack):
            z = jnp.where(lane_grp == g, t, 0.0)
            y = z
            for k in range(1, n_pack):
                y = y + pltpu.roll(z, k * hh, 1)
            parts.append(y)
        return jnp.concatenate(parts, axis=0)

    def prepare(i, qkv_ref, blk):
        rows = slice(i * w, (i + 1) * w)
        krows = slice(w + i * w, w + (i + 1) * w)
        pi = blk * n_sub + i
        pos = pos_ref[0, 0, pi:pi + 1, :].astype(F32)
        pos_col = jnp.broadcast_to(pos, (w, LANES)).T
        packed = pos_col[0:rp]
        for g in range(1, n_pack):
            packed = jnp.where(lane_grp == g, pos_col[g * rp:(g + 1) * rp], packed)
        ang = packed * freq_ref[...]
        cos = spread(jnp.cos(ang))
        sin = spread(jnp.sin(ang)) * sin_sign
        for sl in range(n_qslab):
            xq = norm_rope(qkv_ref[rows, sl * LANES:(sl + 1) * LANES], qn_ref[...], cos, sin) * (LOG2_E * hd ** -0.5)
            qs_ref[sl, rows, :] = xq.astype(qs_ref.dtype)
        for sl in range(n_kslab):
            xk = norm_rope(qkv_ref[rows, qw + sl * LANES:qw + (sl + 1) * LANES], kn_ref[...], cos, sin)
            kf_ref[2 * sl, 0, krows, :] = jnp.where(slot_a, xk, 0.0).astype(kf_ref.dtype)
            kf_ref[2 * sl, 1, krows, :] = jnp.where(slot_a, 0.0, pltpu.roll(xk, hh, 1)).astype(kf_ref.dtype)
            kf_ref[2 * sl + 1, 0, krows, :] = jnp.where(slot_a, pltpu.roll(xk, LANES - hh, 1), 0.0).astype(kf_ref.dtype)
            kf_ref[2 * sl + 1, 1, krows, :] = jnp.where(slot_a, 0.0, xk).astype(kf_ref.dtype)
            v0 = qw + kw + sl * LANES
            vt = qkv_ref[rows, v0:v0 + LANES].T.astype(vt_ref.dtype)
            vt_ref[2 * sl, :, krows] = vt[:hd]
            vt_ref[2 * sl + 1, :, krows] = vt[hd:]

    grp = AT_HEADS // AT_KV_HEADS
    row_a = lax.broadcasted_iota(jnp.int32, (w, AT_HEADS * w), 0)
    lane_a = lax.broadcasted_iota(jnp.int32, (w, AT_HEADS * w), 1)
    from_prev = row_a > (lane_a % w)
    nt = (((1,), (1,)), ((), ()))

    def attend(i, after_scores, seq_start):
        rows = slice(i * w, (i + 1) * w)
        win = slice(i * w, (i + 2) * w)
        fp = from_prev[:, :2 * w]
        parts = []
        for hk in range(AT_KV_HEADS):
            q2 = jnp.concatenate([qs_ref[2 * hk, rows, :], qs_ref[2 * hk + 1, rows, :]], axis=0)
            for half in range(2):
                sp = lax.dot_general(kf_ref[hk, half, win, :], q2, nt, preferred_element_type=F32)
                s_prev = sp[:w]
                if i == 0 and seq_start is not None:
                    s_prev = s_prev + jnp.where(seq_start, -jnp.inf, 0.0)
                parts.append(jnp.where(fp, s_prev, sp[w:]))
        logits = jnp.concatenate(parts, axis=1)
        if after_scores is not None:
            after_scores()
        sink = sink_ref[...]
        m = jnp.maximum(jnp.max(logits, axis=0, keepdims=True), sink)
        p = jnp.exp2(logits - m)
        inv = 1.0 / (jnp.sum(p, axis=0, keepdims=True) + jnp.exp2(sink - m))
        pb = p.astype(BF16)
        zero = jnp.zeros_like(pb)
        pp = jnp.concatenate([jnp.where(from_prev, pb, zero), jnp.where(from_prev, zero, pb)], axis=0)
        for hk in range(AT_KV_HEADS):
            cols = slice(hk * grp * w, (hk + 1) * grp * w)
            r = jnp.dot(vt_ref[hk, :, win], pp[:, cols], preferred_element_type=F32) * inv[:, cols]
            for pair in range(grp // 2):
                out_t = jnp.concatenate([r[:, pair * w:(pair + 1) * w],
                                         r[:, (2 + pair) * w:(3 + pair) * w]], axis=0)
                sl = 2 * hk + pair
                att_ref[rows, sl * LANES:(sl + 1) * LANES] = out_t.T.astype(att_ref.dtype)

    def mix(blk, qkv_ref, qkvn_ref, x_next, seq_start):
        piece = project(x_next, qkvn_ref)
        piece(0)
        for i in range(n_sub):
            prepare(i, qkv_ref, blk)

        def out_proj(half):
            rows = slice(half * (tq // 2), (half + 1) * (tq // 2))
            xrows = slice(blk * tq + rows.start, blk * tq + rows.stop)
            o_ref[0, xrows, :] = x_ref[0, xrows, :] + jnp.dot(att_ref[rows, :], wo_ref[...],
                                                              preferred_element_type=F32)

        fillers = [lambda c=c: piece(c) for c in range(1, n_piece)]
        for i in range(n_sub):
            if i == n_sub // 2:
                fillers.insert(0, lambda: out_proj(0))
            attend(i, fillers.pop(0) if fillers else None, seq_start)
        for f in fillers:
            f()
        out_proj(1)
        kf_ref[:, :, 0:w, :] = kf_ref[:, :, tq:tq + w, :]
        vt_ref[:, :, 0:w] = vt_ref[:, :, tq:tq + w]

    mix(0, qkva_ref, qkvb_ref, x_ref[0, tq:2 * tq, :], first)
    mix(1, qkvb_ref, qkva_ref, xn_ref[0], None)


def _slab_lane_dims():
    hh = AT_HEAD_DIM // 2
    lane = np.arange(LANES)
    return (lane // hh) % 2, lane % hh + hh * (lane // AT_HEAD_DIM)


def _attn_mixer(x, nm, w_qkv, positions, q_norm, k_norm, sinks, wo):
    bsz, s, d = x.shape
    tq = min(TS_MIX, s)
    assert tq % (2 * WINDOW) == 0 and s % (2 * tq) == 0, (s, tq)
    n_blk, n_pair = s // tq, s // (2 * tq)

    def after_pair(i, j):
        f = jnp.minimum((i * n_pair + j) * 2 + 2, bsz * n_blk - 1)
        return f // n_blk, f % n_blk, 0
    hd = AT_HEAD_DIM
    qw, kw = AT_HEADS * hd, AT_KV_HEADS * hd
    grp = AT_HEADS // AT_KV_HEADS
    inv_freq = ROPE_THETA ** (-jnp.arange(hd // 2, dtype=F32) * (2.0 / hd))
    freq = jnp.tile(inv_freq, LANES // (hd // 2))[None, :]
    head_in_slab, dim = _slab_lane_dims()
    seg = jnp.asarray((head_in_slab[:, None] == head_in_slab[None, :]).astype(np.float32), dtype=BF16)
    n_slab = (qw + kw) // LANES
    w_qk = w_qkv[:, :qw + kw].reshape(d, n_slab, 2, 2, hd // 2).transpose(0, 1, 3, 2, 4).reshape(d, qw + kw)
    w_qkv = jnp.concatenate([w_qk, w_qkv[:, qw + kw:]], axis=1)
    qn = q_norm.astype(F32)[dim][None, :]
    kn = k_norm.astype(F32)[dim][None, :]
    pos = positions.reshape(bsz, n_pair, 2 * tq // WINDOW, WINDOW)
    sk = (sinks.astype(F32) * LOG2_E).reshape(AT_KV_HEADS, grp)[:, np.array([0, 2, 1, 3])]
    sk = jnp.repeat(sk, WINDOW, axis=1).reshape(1, AT_HEADS * WINDOW)
    consts_a = (nm, w_qkv)
    consts_b = (freq, qn, kn, seg, sk, wo)
    return pl.pallas_call(
        _attn_kernel,
        out_shape=jax.ShapeDtypeStruct((bsz, s, d), F32),
        grid=(bsz, n_pair),
        in_specs=([pl.BlockSpec((1, 2 * tq, d), lambda i, j: (i, j, 0)),
                   pl.BlockSpec((1, tq, d), after_pair)] + [_resident(c.shape) for c in consts_a]
                  + [pl.BlockSpec((1, 1, 2 * tq // WINDOW, WINDOW), lambda i, j: (i, j, 0, 0))]
                  + [_resident(c.shape) for c in consts_b]),
        out_specs=pl.BlockSpec((1, 2 * tq, d), lambda i, j: (i, j, 0)),
        scratch_shapes=[pltpu.VMEM((tq, qw + 2 * kw), F32), pltpu.VMEM((tq, qw + 2 * kw), F32),
                        pltpu.VMEM((qw // LANES, tq, LANES), BF16),
                        pltpu.VMEM((AT_KV_HEADS, 2, tq + WINDOW, LANES), BF16),
                        pltpu.VMEM((AT_KV_HEADS, hd, tq + WINDOW), BF16),
                        pltpu.VMEM((tq, qw), BF16)],
        compiler_params=_params(2),
        name="attn_mixer",
    )(x, x, *consts_a, pos, *consts_b)


def _post_kernel(x_ref, p_ref, nf_ref, wu_ref, cw_ref, cb_ref, wd_ref, np_ref, wg_ref, wp_ref, o_ref,
                 gs_ref, gtail_ref, act_ref):
    tm = x_ref.shape[1]
    ff = wd_ref.shape[0]
    n_slab = ff // LANES

    @pl.when(pl.program_id(1) == 0)
    def _():
        gtail_ref[...] = jnp.zeros_like(gtail_ref)

    x1 = x_ref[0]
    h = _rms(x1, nf_ref[...]).astype(BF16)
    gs_ref[:, 0:SUBLANES, :] = gtail_ref[...]
    g = jnp.dot(h, wu_ref[:, 0:ff], preferred_element_type=F32)
    for sl in range(n_slab):
        gs_ref[sl, SUBLANES:SUBLANES + tm, :] = g[:, sl * LANES:(sl + 1) * LANES]
    u = jnp.dot(h, wu_ref[:, ff:2 * ff], preferred_element_type=F32)
    gc = _slab_conv(gs_ref, 0, n_slab, 0, tm, cw_ref, cb_ref)
    gtail_ref[...] = gs_ref[:, tm:tm + SUBLANES, :]
    act_ref[...] = (jax.nn.gelu(gc, approximate=True) * u).astype(act_ref.dtype)
    x2 = x1 + jnp.dot(act_ref[...], wd_ref[...], preferred_element_type=F32)
    gate = jax.nn.sigmoid(jnp.dot(_rms(x2, np_ref[...]).astype(BF16), wg_ref[...], preferred_element_type=F32))
    pe = jnp.dot(p_ref[0, 0].astype(BF16), wp_ref[...], preferred_element_type=F32)
    o_ref[0] = x2 + gate * pe


def _post_mixer(x, p, layer, nf, wu, cw, cb, wd, npl, wg, wp):
    bsz, s, d = x.shape
    tm = min(TM_POST, s)
    ff = wd.shape[1]
    stacked = (nf, wu, cw, cb, wd, npl, wg, wp)
    return pl.pallas_call(
        _post_kernel,
        out_shape=jax.ShapeDtypeStruct((bsz, s, d), F32),
        grid=(bsz, s // tm),
        in_specs=[pl.BlockSpec((1, tm, d), lambda i, j: (i, j, 0)),
                  pl.BlockSpec((1, 1, tm, p.shape[-1]), lambda i, j: (layer, i, j, 0))]
                 + [_layer_block(a, layer) for a in stacked],
        out_specs=pl.BlockSpec((1, tm, d), lambda i, j: (i, j, 0)),
        scratch_shapes=[pltpu.VMEM((ff // LANES, SUBLANES + tm, LANES), F32),
                        pltpu.VMEM((ff // LANES, SUBLANES, LANES), F32), pltpu.VMEM((tm, ff), BF16)],
        compiler_params=_params(2),
        name=f"post_mixer_{layer}",
    )(x, p, *stacked)


def _block_diag(w):
    n, r, _ = w.shape
    eye = jnp.eye(n, dtype=w.dtype)
    return (eye[:, None, :, None] * w[:, :, None, :]).reshape(n * r, n * r)


def kernel(x, p, positions, norm_mix, norm_ffn, norm_ple, hy_w_in, hy_b_in, rg_conv_w, rg_conv_b, rg_w_a, rg_b_a, rg_w_x, rg_b_x, rg_lambda, ml_conv_w, ml_conv_b, ml_norm, hy_w_out, at_w_qkv, at_q_norm, at_k_norm, at_sinks, at_w_out, ff_w_up, ff_conv_w, ff_conv_b, ff_w_down, ple_w_gate, ple_w_proj):
    depth = p.shape[0]
    row = lambda v: v.astype(F32)[None, :]
    post = (norm_ffn.astype(F32)[:, None, :], ff_w_up.astype(BF16), ff_conv_w.astype(F32),
            ff_conv_b.astype(F32)[:, None, :], ff_w_down.astype(BF16), norm_ple.astype(F32)[:, None, :],
            ple_w_gate.astype(BF16), ple_w_proj.astype(BF16))
    for layer in range(depth):
        if layer % 2 == 0:
            e = layer // 2
            pad = LANES - 2 * ML_HEADS
            x = _hybrid_mixer(x, row(norm_mix[layer]), jnp.pad(hy_w_in[e], ((0, 0), (0, pad))).astype(BF16),
                              row(jnp.pad(hy_b_in[e], (0, pad))),
                              rg_conv_w[e], row(rg_conv_b[e]), _block_diag(rg_w_a[e]).astype(BF16), row(rg_b_a[e]),
                              _block_diag(rg_w_x[e]).astype(BF16), row(rg_b_x[e]), row(rg_lambda[e]),
                              ml_conv_w[e], row(ml_conv_b[e]), row(ml_norm[e]), hy_w_out[e].astype(BF16))
        else:
            o = layer // 2
            x = _attn_mixer(x, row(norm_mix[layer]), at_w_qkv[o].astype(BF16), positions, at_q_norm[o],
                            at_k_norm[o], at_sinks[o], at_w_out[o].astype(BF16))
        x = _post_mixer(x, p, layer, *post)
    return x
```

```python
import jax
import jax.numpy as jnp
import numpy as np
from jax import lax
from jax.experimental import pallas as pl
from jax.experimental.pallas import tpu as pltpu

F32 = jnp.float32
BF16 = jnp.bfloat16

RG_WIDTH = 512
RG_C = 8.0
ML_HEADS = 4
ML_HEAD_DIM = 128
ML_WIDTH = ML_HEADS * ML_HEAD_DIM
ML_CHUNK = 128
AT_HEADS = 16
AT_KV_HEADS = 4
AT_HEAD_DIM = 64
WINDOW = 128
ROPE_THETA = 10000.0
EPS = 1e-6
LOG2_E = 1.4426950408889634

LANES = 128
SUBLANES = 8
VMEM_LIMIT_BYTES = 56 * 1024 * 1024

TS_MIX = 512
TM_POST = 512
PROJ_CHUNK = 512
QKV_CHUNK = 768
RG_SLAB0 = 0
Q_SLAB0 = RG_WIDTH // LANES
K_SLAB0 = Q_SLAB0 + ML_WIDTH // LANES
N_CONV_SLAB = K_SLAB0 + ML_WIDTH // LANES


def _params(n_axes):
    return pltpu.CompilerParams(dimension_semantics=("arbitrary",) * n_axes,
                                vmem_limit_bytes=VMEM_LIMIT_BYTES)


def _resident(shape, index=None):
    n = len(shape)
    idx = (0,) * n if index is None else tuple(index)
    return pl.BlockSpec(shape, lambda *_: idx, pipeline_mode=pl.Buffered(1))


def _layer_block(arr, layer):
    shape = (None,) + arr.shape[1:]
    return _resident(shape, (layer,) + (0,) * (arr.ndim - 1))


def _rms(x, g):
    return x * lax.rsqrt(jnp.mean(x * x, axis=-1, keepdims=True) + EPS) * g


def _slab_conv(zs_ref, slab0, n_slab, r0, n_rows, w_ref, b_ref):
    k_w = w_ref.shape[0]
    base = SUBLANES + r0
    outs = []
    for s in range(n_slab):
        cs = slice(s * LANES, (s + 1) * LANES)
        acc = zs_ref[slab0 + s, base:base + n_rows, :] * w_ref[k_w - 1:k_w, cs] + b_ref[:, cs]
        for j in range(k_w - 1):
            d = k_w - 1 - j
            acc = acc + zs_ref[slab0 + s, base - d:base - d + n_rows, :] * w_ref[j:j + 1, cs]
        outs.append(acc)
    return jnp.concatenate(outs, axis=1)


def _log_sigmoid(x):
    return jnp.minimum(x, 0.0) - jnp.log1p(jnp.exp(-jnp.abs(x)))


def _rg_gates(zs_ref, cw_ref, cb_ref, wa_ref, ba_ref, wx_ref, bx_ref, lam_ref, a_ref, u_ref):
    ts = a_ref.shape[0]
    xc = _slab_conv(zs_ref, RG_SLAB0, RG_WIDTH // LANES, 0, ts, cw_ref, cb_ref)
    xb = xc.astype(BF16)
    r = jax.nn.sigmoid(jnp.dot(xb, wa_ref[...], preferred_element_type=F32) + ba_ref[...])
    i = jax.nn.sigmoid(jnp.dot(xb, wx_ref[...], preferred_element_type=F32) + bx_ref[...])
    lam = lam_ref[...]
    softplus_neg_lam = jnp.maximum(-lam, 0.0) + jnp.log1p(jnp.exp(-jnp.abs(lam)))
    log_a = (-RG_C) * r * softplus_neg_lam
    a = jnp.exp(log_a)
    a_ref[...] = a
    v = 1.0 - a * a
    u_ref[...] = jnp.where(v > 0.0, v * lax.rsqrt(v), 0.0) * (i * xc)


def _rg_scan(z_ref, y_ref, hc_ref, a_ref, u_ref):
    ts = z_ref.shape[0]
    row = lax.broadcasted_iota(jnp.int32, (SUBLANES, RG_WIDTH), 0)
    hc = hc_ref[...]
    for k in range(ts // SUBLANES):
        rows = slice(k * SUBLANES, (k + 1) * SUBLANES)
        a = a_ref[rows, :]
        u = u_ref[rows, :]
        for d in (1, 2, 4):
            keep = row >= d
            u = u + a * jnp.where(keep, pltpu.roll(u, d, 0), 0.0)
            a = a * jnp.where(keep, pltpu.roll(a, d, 0), 1.0)
        h = u + a * hc
        u_ref[rows, :] = h
        hc = jnp.broadcast_to(h[SUBLANES - 1:SUBLANES, :], (SUBLANES, RG_WIDTH))
    hc_ref[...] = hc
    gate = jax.nn.gelu(z_ref[:, RG_WIDTH:2 * RG_WIDTH], approximate=True)
    y_ref[:, 0:RG_WIDTH] = (u_ref[...] * gate).astype(y_ref.dtype)


def _lane_col(x, h):
    return jnp.broadcast_to(x[:, h:h + 1], x.shape)


def _ml_chunk(rows, z_ref, zs_ref, gt_ref, cw_ref, cb_ref, nw_ref, y_ref, c_ref, n_ref, m_ref, after_dots=None):
    lc = ML_CHUNK
    dh = ML_HEAD_DIM
    q0 = 2 * RG_WIDTH
    k0, v0, o0 = q0 + ML_WIDTH, q0 + 2 * ML_WIDTH, q0 + 3 * ML_WIDTH
    row = lax.broadcasted_iota(jnp.int32, (lc, LANES), 0)

    n_sl = ML_WIDTH // LANES
    q = jax.nn.silu(_slab_conv(zs_ref, Q_SLAB0, n_sl, rows.start, lc, cw_ref.at[:, :ML_WIDTH], cb_ref.at[:, :ML_WIDTH]))
    k = jax.nn.silu(_slab_conv(zs_ref, K_SLAB0, n_sl, rows.start, lc, cw_ref.at[:, ML_WIDTH:], cb_ref.at[:, ML_WIDTH:]))
    k = k * (dh ** -0.5)

    gates = gt_ref[rows, :]
    b = _log_sigmoid(gates)
    d = 1
    while d < lc:
        b = b + jnp.where(row >= d, pltpu.roll(b, d, 0), 0.0)
        d *= 2
    b = pltpu.roll(b, LANES - ML_HEADS, 1)
    r = gates - b
    b_last = b[lc - 1:lc, :]
    g_loc = b_last + r
    m_loc = jnp.max(g_loc, axis=0, keepdims=True)
    w_loc = jnp.exp(g_loc - m_loc)
    m_st = m_ref[0:1, :]
    m_new = jnp.maximum(b_last + m_st, m_loc)
    a_prev = jnp.exp(b_last + m_st - m_new)
    a_loc = jnp.exp(m_loc - m_new)
    rmax = r
    d = 1
    while d < lc:
        rmax = jnp.maximum(rmax, jnp.where(row >= d, pltpu.roll(rmax, d, 0), -jnp.inf))
        d *= 2
    e = -jnp.maximum(m_st, rmax)
    a_inter = jnp.exp(m_st + e)
    exp_neg_mt = jnp.exp(e - b)
    r_t = r.T

    heads = [slice(h * dh, (h + 1) * dh) for h in range(ML_HEADS)]
    wide = lambda f: jnp.concatenate([f(h) for h in range(ML_HEADS)], axis=1)
    nt = (((1,), (1,)), ((), ()))
    row_w = lax.broadcasted_iota(jnp.int32, (lc, ML_WIDTH), 0)
    col_w = lax.broadcasted_iota(jnp.int32, (lc, ML_WIDTH), 1)
    causal = row_w >= (col_w % dh)
    qb = q.astype(BF16)
    kb = k.astype(BF16)
    vb = z_ref[rows, v0:o0].astype(BF16)
    c_prev = [c_ref[h] for h in range(ML_HEADS)]
    n_prev = wide(lambda h: n_ref[h:h + 1, :])
    s = wide(lambda h: lax.dot_general(qb[:, heads[h]], kb[:, heads[h]], nt, preferred_element_type=F32))
    w_intra = jnp.where(causal, jnp.exp(wide(lambda h: _lane_col(e, h)) + wide(lambda h: r_t[h:h + 1, :])), 0.0)
    s_qk = s * w_intra
    sb = s_qk.astype(BF16)
    a_in = wide(lambda h: _lane_col(a_inter, h))
    num = (wide(lambda h: jnp.dot(sb[:, heads[h]], vb[:, heads[h]], preferred_element_type=F32))
           + a_in * wide(lambda h: jnp.dot(qb[:, heads[h]], c_prev[h].astype(BF16), preferred_element_type=F32)))
    if after_dots is not None:
        after_dots()
    t = s_qk + a_in * (q * n_prev)
    den = wide(lambda h: jnp.broadcast_to(jnp.sum(t[:, heads[h]], axis=-1, keepdims=True), (lc, dh)))
    hm = num / jnp.maximum(jnp.abs(den), wide(lambda h: _lane_col(exp_neg_mt, h)))
    ms = wide(lambda h: jnp.broadcast_to(jnp.mean(hm[:, heads[h]] * hm[:, heads[h]], axis=-1, keepdims=True), (lc, dh)))
    hm = hm * lax.rsqrt(ms + EPS) * nw_ref[...]
    y = jax.nn.sigmoid(z_ref[rows, o0:o0 + ML_WIDTH]) * hm
    y_ref[rows, RG_WIDTH:RG_WIDTH + ML_WIDTH] = y.astype(y_ref.dtype)

    kw = k * wide(lambda h: _lane_col(w_loc, h))
    for h in range(ML_HEADS):
        c_loc = jnp.dot(kw[:, heads[h]].T.astype(BF16), vb[:, heads[h]], preferred_element_type=F32)
        n_loc = jnp.sum(kw[:, heads[h]], axis=0, keepdims=True)
        ap = a_prev[:, h:h + 1]
        al = a_loc[:, h:h + 1]
        c_ref[h] = ap * c_prev[h] + al * c_loc
        n_ref[h:h + 1, :] = ap * n_ref[h:h + 1, :] + al * n_loc
    m_ref[...] = jnp.broadcast_to(m_new, m_ref.shape)


def _hybrid_kernel(x_ref, xn_ref, nm_ref, win_ref, bin_ref,
                   rcw_ref, rcb_ref, wa_ref, ba_ref, wx_ref, bx_ref, lam_ref,
                   mcw_ref, mcb_ref, mnw_ref, wo_ref, o_ref,
                   za_ref, zb_ref, zsa_ref, zsb_ref, gta_ref, gtb_ref, y_ref, tail_ref, hc_ref, a_ref, u_ref,
                   c_ref, n_ref, m_ref):
    step = pl.program_id(0) * pl.num_programs(1) + pl.program_id(1)

    @pl.when(pl.program_id(1) == 0)
    def _():
        for ref in (tail_ref, hc_ref, c_ref, n_ref):
            ref[...] = jnp.zeros_like(ref)
        m_ref[...] = jnp.full_like(m_ref, -jnp.inf)

    n_main = 2 * RG_WIDTH + 4 * ML_WIDTH
    n_piece = n_main // PROJ_CHUNK + 1
    ts = x_ref.shape[1]
    conv_slab = {0: RG_SLAB0, 2 * RG_WIDTH // PROJ_CHUNK: Q_SLAB0, (2 * RG_WIDTH + ML_WIDTH) // PROJ_CHUNK: K_SLAB0}

    def project(src_ref, z_ref, zs_ref, gt_ref):
        h = _rms(src_ref[0], nm_ref[...]).astype(BF16)

        def piece(c):
            if c < n_main // PROJ_CHUNK:
                cs = slice(c * PROJ_CHUNK, (c + 1) * PROJ_CHUNK)
                res = jnp.dot(h, win_ref[:, cs], preferred_element_type=F32) + bin_ref[:, cs]
                if c in conv_slab:
                    for sl in range(PROJ_CHUNK // LANES):
                        zs_ref[conv_slab[c] + sl, SUBLANES:SUBLANES + ts, :] = res[:, sl * LANES:(sl + 1) * LANES]
                else:
                    z_ref[:, cs] = res
            else:
                gs = slice(n_main, n_main + LANES)
                gt_ref[...] = jnp.dot(h, win_ref[:, gs], preferred_element_type=F32) + bin_ref[:, gs]
        return piece

    @pl.when(step == 0)
    def _():
        piece = project(x_ref, za_ref, zsa_ref, gta_ref)
        for c in range(n_piece):
            piece(c)

    def mix(z_ref, zs_ref, gt_ref, zn_ref, zsn_ref, gtn_ref):
        piece = project(xn_ref, zn_ref, zsn_ref, gtn_ref)
        n_chunk = ts // ML_CHUNK
        zs_ref[:, 0:SUBLANES, :] = tail_ref[...]
        piece(0)
        _rg_gates(zs_ref, rcw_ref, rcb_ref, wa_ref, ba_ref, wx_ref, bx_ref, lam_ref, a_ref, u_ref)
        piece(1)
        piece(2)
        _rg_scan(z_ref, y_ref, hc_ref, a_ref, u_ref)
        nxt = 3
        for c in range(n_chunk):
            rows = slice(c * ML_CHUNK, (c + 1) * ML_CHUNK)
            mid = (lambda k=nxt: piece(k)) if nxt < n_piece else None
            nxt += 1
            _ml_chunk(rows, z_ref, zs_ref, gt_ref, mcw_ref, mcb_ref, mnw_ref, y_ref, c_ref, n_ref, m_ref, mid)
            o_ref[0, rows, :] = x_ref[0, rows, :] + jnp.dot(y_ref[rows, :], wo_ref[...], preferred_element_type=F32)
        for c in range(nxt, n_piece):
            piece(c)
        tail_ref[...] = zs_ref[:, ts:ts + SUBLANES, :]

    @pl.when(step % 2 == 0)
    def _():
        mix(za_ref, zsa_ref, gta_ref, zb_ref, zsb_ref, gtb_ref)

    @pl.when(step % 2 == 1)
    def _():
        mix(zb_ref, zsb_ref, gtb_ref, za_ref, zsa_ref, gta_ref)


def _next_block_map(bsz, n_blk):
    def index(i, j):
        f = jnp.minimum(i * n_blk + j + 1, bsz * n_blk - 1)
        return f // n_blk, f % n_blk, 0
    return index


def _hybrid_mixer(x, nm, w_in, b_in, rcw, rcb, wa, ba, wx, bx, lam, mcw, mcb, mnw, wo):
    bsz, s, d = x.shape
    ts = min(TS_MIX, s)
    n_main = w_in.shape[1] - LANES
    rw, mw = RG_WIDTH, ML_WIDTH
    consts = (nm, w_in, b_in, rcw, rcb, wa, ba, wx, bx, lam, mcw, mcb, mnw, wo)
    return pl.pallas_call(
        _hybrid_kernel,
        out_shape=jax.ShapeDtypeStruct((bsz, s, d), F32),
        grid=(bsz, s // ts),
        in_specs=[pl.BlockSpec((1, ts, d), lambda i, j: (i, j, 0)),
                  pl.BlockSpec((1, ts, d), _next_block_map(bsz, s // ts))] + [_resident(c.shape) for c in consts],
        out_specs=pl.BlockSpec((1, ts, d), lambda i, j: (i, j, 0)),
        scratch_shapes=[pltpu.VMEM((ts, n_main), F32), pltpu.VMEM((ts, n_main), F32),
                        pltpu.VMEM((N_CONV_SLAB, SUBLANES + ts, LANES), F32),
                        pltpu.VMEM((N_CONV_SLAB, SUBLANES + ts, LANES), F32),
                        pltpu.VMEM((ts, LANES), F32), pltpu.VMEM((ts, LANES), F32),
                        pltpu.VMEM((ts, rw + mw), BF16),
                        pltpu.VMEM((N_CONV_SLAB, SUBLANES, LANES), F32), pltpu.VMEM((SUBLANES, rw), F32),
                        pltpu.VMEM((ts, rw), F32), pltpu.VMEM((ts, rw), F32),
                        pltpu.VMEM((ML_HEADS, ML_HEAD_DIM, ML_HEAD_DIM), F32),
                        pltpu.VMEM((SUBLANES, ML_HEAD_DIM), F32),
                        pltpu.VMEM((SUBLANES, LANES), F32)],
        compiler_params=_params(2),
        name="hybrid_mixer",
    )(x, x, *consts)


def _attn_kernel(x_ref, xn_ref, nm_ref, wqkv_ref, pos_ref, freq_ref, qn_ref, kn_ref, seg_ref, sink_ref, wo_ref,
                 o_ref, qkva_ref, qkvb_ref, qs_ref, kf_ref, vt_ref, att_ref):
    tq = xn_ref.shape[1]
    w = WINDOW
    hd = AT_HEAD_DIM
    hh = hd // 2
    n_sub = tq // w
    qw = AT_HEADS * hd
    kw = AT_KV_HEADS * hd
    n_qslab = qw // LANES
    n_kslab = kw // LANES
    n_pack = LANES // hh
    rp = w // n_pack
    first = pl.program_id(1) == 0
    step = pl.program_id(0) * pl.num_programs(1) + pl.program_id(1)

    @pl.when(first)
    def _():
        kf_ref[:, :, 0:w, :] = jnp.zeros((AT_KV_HEADS, 2, w, LANES), kf_ref.dtype)
        vt_ref[:, :, 0:w] = jnp.zeros((AT_KV_HEADS, hd, w), vt_ref.dtype)

    lane = lax.broadcasted_iota(jnp.int32, (w, LANES), 1)
    slot_a = (lane // hh) % 2 == 0
    sin_sign = jnp.where(lane < hd, -1.0, 1.0)
    lane_grp = lax.broadcasted_iota(jnp.int32, (rp, LANES), 1) // hh
    seg = seg_ref[...]
    n_piece = (qw + 2 * kw) // QKV_CHUNK

    def project(src, dst_ref):
        h = _rms(src, nm_ref[...]).astype(BF16)

        def piece(c):
            cs = slice(c * QKV_CHUNK, (c + 1) * QKV_CHUNK)
            dst_ref[:, cs] = jnp.dot(h, wqkv_ref[:, cs], preferred_element_type=F32)
        return piece

    @pl.when(step == 0)
    def _():
        piece = project(x_ref[0, 0:tq, :], qkva_ref)
        for c in range(n_piece):
            piece(c)

    def norm_rope(t, gain, cos, sin):
        ms = jnp.dot((t * t).astype(BF16), seg, preferred_element_type=F32) * (1.0 / hd)
        tn = t * lax.rsqrt(ms + EPS) * gain
        return tn * cos + pltpu.roll(tn, hd, 1) * sin

    def spread(t):
        parts = []
        for g in range(n_pack):
            z = jnp.where(lane_grp == g, t, 0.0)
            y = z
            for k in range(1, n_pack):
                y = y + pltpu.roll(z, k * hh, 1)
            parts.append(y)
        return jnp.concatenate(parts, axis=0)

    def prepare(i, qkv_ref, blk):
        rows = slice(i * w, (i + 1) * w)
        krows = slice(w + i * w, w + (i + 1) * w)
        pi = blk * n_sub + i
        pos = pos_ref[0, 0, pi:pi + 1, :].astype(F32)
        pos_col = jnp.broadcast_to(pos, (w, LANES)).T
        packed = pos_col[0:rp]
        for g in range(1, n_pack):
            packed = jnp.where(lane_grp == g, pos_col[g * rp:(g + 1) * rp], packed)
        ang = packed * freq_ref[...]
        cos = spread(jnp.cos(ang))
        sin = spread(jnp.sin(ang)) * sin_sign
        for sl in range(n_qslab):
            xq = norm_rope(qkv_ref[rows, sl * LANES:(sl + 1) * LANES], qn_ref[...], cos, sin) * (LOG2_E * hd ** -0.5)
            qs_ref[sl, rows, :] = xq.astype(qs_ref.dtype)
        for sl in range(n_kslab):
            xk = norm_rope(qkv_ref[rows, qw + sl * LANES:qw + (sl + 1) * LANES], kn_ref[...], cos, sin)
            kf_ref[2 * sl, 0, krows, :] = jnp.where(slot_a, xk, 0.0).astype(kf_ref.dtype)
            kf_ref[2 * sl, 1, krows, :] = jnp.where(slot_a, 0.0, pltpu.roll(xk, hh, 1)).astype(kf_ref.dtype)
            kf_ref[2 * sl + 1, 0, krows, :] = jnp.where(slot_a, pltpu.roll(xk, LANES - hh, 1), 0.0).astype(kf_ref.dtype)
            kf_ref[2 * sl + 1, 1, krows, :] = jnp.where(slot_a, 0.0, xk).astype(kf_ref.dtype)
            v0 = qw + kw + sl * LANES
            vt = qkv_ref[rows, v0:v0 + LANES].T.astype(vt_ref.dtype)
            vt_ref[2 * sl, :, krows] = vt[:hd]
            vt_ref[2 * sl + 1, :, krows] = vt[hd:]

    grp = AT_HEADS // AT_KV_HEADS
    row_a = lax.broadcasted_iota(jnp.int32, (w, AT_HEADS * w), 0)
    lane_a = lax.broadcasted_iota(jnp.int32, (w, AT_HEADS * w), 1)
    from_prev = row_a > (lane_a % w)
    nt = (((1,), (1,)), ((), ()))

    def attend(i, after_scores, seq_start):
        rows = slice(i * w, (i + 1) * w)
        win = slice(i * w, (i + 2) * w)
        fp = from_prev[:, :2 * w]
        parts = []
        for hk in range(AT_KV_HEADS):
            q2 = jnp.concatenate([qs_ref[2 * hk, rows, :], qs_ref[2 * hk + 1, rows, :]], axis=0)
            for half in range(2):
                sp = lax.dot_general(kf_ref[hk, half, win, :], q2, nt, preferred_element_type=F32)
                s_prev = sp[:w]
                if i == 0 and seq_start is not None:
                    s_prev = s_prev + jnp.where(seq_start, -jnp.inf, 0.0)
                parts.append(jnp.where(fp, s_prev, sp[w:]))
        logits = jnp.concatenate(parts, axis=1)
        if after_scores is not None:
            after_scores()
        sink = sink_ref[...]
        m = jnp.maximum(jnp.max(logits, axis=0, keepdims=True), sink)
        p = jnp.exp2(logits - m)
        inv = 1.0 / (jnp.sum(p, axis=0, keepdims=True) + jnp.exp2(sink - m))
        pb = p.astype(BF16)
        zero = jnp.zeros_like(pb)
        pp = jnp.concatenate([jnp.where(from_prev, pb, zero), jnp.where(from_prev, zero, pb)], axis=0)
        for hk in range(AT_KV_HEADS):
            cols = slice(hk * grp * w, (hk + 1) * grp * w)
            r = jnp.dot(vt_ref[hk, :, win], pp[:, cols], preferred_element_type=F32) * inv[:, cols]
            for pair in range(grp // 2):
                out_t = jnp.concatenate([r[:, pair * w:(pair + 1) * w],
                                         r[:, (2 + pair) * w:(3 + pair) * w]], axis=0)
                sl = 2 * hk + pair
                att_ref[rows, sl * LANES:(sl + 1) * LANES] = out_t.T.astype(att_ref.dtype)

    def mix(blk, qkv_ref, qkvn_ref, x_next, seq_start):
        piece = project(x_next, qkvn_ref)
        piece(0)
        for i in range(n_sub):
            prepare(i, qkv_ref, blk)

        def out_proj(half):
            rows = slice(half * (tq // 2), (half + 1) * (tq // 2))
            xrows = slice(blk * tq + rows.start, blk * tq + rows.stop)
            o_ref[0, xrows, :] = x_ref[0, xrows, :] + jnp.dot(att_ref[rows, :], wo_ref[...],
                                                              preferred_element_type=F32)

        fillers = [lambda c=c: piece(c) for c in range(1, n_piece)]
        for i in range(n_sub):
            if i == n_sub // 2:
                fillers.insert(0, lambda: out_proj(0))
            attend(i, fillers.pop(0) if fillers else None, seq_start)
        for f in fillers:
            f()
        out_proj(1)
        kf_ref[:, :, 0:w, :] = kf_ref[:, :, tq:tq + w, :]
        vt_ref[:, :, 0:w] = vt_ref[:, :, tq:tq + w]

    mix(0, qkva_ref, qkvb_ref, x_ref[0, tq:2 * tq, :], first)
    mix(1, qkvb_ref, qkva_ref, xn_ref[0], None)


def _slab_lane_dims():
    hh = AT_HEAD_DIM // 2
    lane = np.arange(LANES)
    return (lane // hh) % 2, lane % hh + hh * (lane // AT_HEAD_DIM)


def _attn_mixer(x, nm, w_qkv, positions, q_norm, k_norm, sinks, wo):
    bsz, s, d = x.shape
    tq = min(TS_MIX, s)
    assert tq % (2 * WINDOW) == 0 and s % (2 * tq) == 0, (s, tq)
    n_blk, n_pair = s // tq, s // (2 * tq)

    def after_pair(i, j):
        f = jnp.minimum((i * n_pair + j) * 2 + 2, bsz * n_blk - 1)
        return f // n_blk, f % n_blk, 0
    hd = AT_HEAD_DIM
    qw, kw = AT_HEADS * hd, AT_KV_HEADS * hd
    grp = AT_HEADS // AT_KV_HEADS
    inv_freq = ROPE_THETA ** (-jnp.arange(hd // 2, dtype=F32) * (2.0 / hd))
    freq = jnp.tile(inv_freq, LANES // (hd // 2))[None, :]
    head_in_slab, dim = _slab_lane_dims()
    seg = jnp.asarray((head_in_slab[:, None] == head_in_slab[None, :]).astype(np.float32), dtype=BF16)
    n_slab = (qw + kw) // LANES
    w_qk = w_qkv[:, :qw + kw].reshape(d, n_slab, 2, 2, hd // 2).transpose(0, 1, 3, 2, 4).reshape(d, qw + kw)
    w_qkv = jnp.concatenate([w_qk, w_qkv[:, qw + kw:]], axis=1)
    qn = q_norm.astype(F32)[dim][None, :]
    kn = k_norm.astype(F32)[dim][None, :]
    pos = positions.reshape(bsz, n_pair, 2 * tq // WINDOW, WINDOW)
    sk = (sinks.astype(F32) * LOG2_E).reshape(AT_KV_HEADS, grp)[:, np.array([0, 2, 1, 3])]
    sk = jnp.repeat(sk, WINDOW, axis=1).reshape(1, AT_HEADS * WINDOW)
    consts_a = (nm, w_qkv)
    consts_b = (freq, qn, kn, seg, sk, wo)
    return pl.pallas_call(
        _attn_kernel,
        out_shape=jax.ShapeDtypeStruct((bsz, s, d), F32),
        grid=(bsz, n_pair),
        in_specs=([pl.BlockSpec((1, 2 * tq, d), lambda i, j: (i, j, 0)),
                   pl.BlockSpec((1, tq, d), after_pair)] + [_resident(c.shape) for c in consts_a]
                  + [pl.BlockSpec((1, 1, 2 * tq // WINDOW, WINDOW), lambda i, j: (i, j, 0, 0))]
                  + [_resident(c.shape) for c in consts_b]),
        out_specs=pl.BlockSpec((1, 2 * tq, d), lambda i, j: (i, j, 0)),
        scratch_shapes=[pltpu.VMEM((tq, qw + 2 * kw), F32), pltpu.VMEM((tq, qw + 2 * kw), F32),
                        pltpu.VMEM((qw // LANES, tq, LANES), BF16),
                        pltpu.VMEM((AT_KV_HEADS, 2, tq + WINDOW, LANES), BF16),
                        pltpu.VMEM((AT_KV_HEADS, hd, tq + WINDOW), BF16),
                        pltpu.VMEM((tq, qw), BF16)],
        compiler_params=_params(2),
        name="attn_mixer",
    )(x, x, *consts_a, pos, *consts_b)


def _post_kernel(x_ref, p_ref, nf_ref, wu_ref, cw_ref, cb_ref, wd_ref, np_ref, wg_ref, wp_ref, o_ref,
                 gs_ref, gtail_ref, act_ref):
    tm = x_ref.shape[1]
    ff = wd_ref.shape[0]
    n_slab = ff // LANES

    @pl.when(pl.program_id(1) == 0)
    def _():
        gtail_ref[...] = jnp.zeros_like(gtail_ref)

    x1 = x_ref[0]
    h = _rms(x1, nf_ref[...]).astype(BF16)
    gs_ref[:, 0:SUBLANES, :] = gtail_ref[...]
    g = jnp.dot(h, wu_ref[:, 0:ff], preferred_element_type=F32)
    for sl in range(n_slab):
        gs_ref[sl, SUBLANES:SUBLANES + tm, :] = g[:, sl * LANES:(sl + 1) * LANES]
    u = jnp.dot(h, wu_ref[:, ff:2 * ff], preferred_element_type=F32)
    gc = _slab_conv(gs_ref, 0, n_slab, 0, tm, cw_ref, cb_ref)
    gtail_ref[...] = gs_ref[:, tm:tm + SUBLANES, :]
    act_ref[...] = (jax.nn.gelu(gc, approximate=True) * u).astype(act_ref.dtype)
    x2 = x1 + jnp.dot(act_ref[...], wd_ref[...], preferred_element_type=F32)
    gate = jax.nn.sigmoid(jnp.dot(_rms(x2, np_ref[...]).astype(BF16), wg_ref[...], preferred_element_type=F32))
    pe = jnp.dot(p_ref[0, 0].astype(BF16), wp_ref[...], preferred_element_type=F32)
    o_ref[0] = x2 + gate * pe


def _post_mixer(x, p, layer, nf, wu, cw, cb, wd, npl, wg, wp):
    bsz, s, d = x.shape
    tm = min(TM_POST, s)
    ff = wd.shape[1]
    stacked = (nf, wu, cw, cb, wd, npl, wg, wp)
    return pl.pallas_call(
        _post_kernel,
        out_shape=jax.ShapeDtypeStruct((bsz, s, d), F32),
        grid=(bsz, s // tm),
        in_specs=[pl.BlockSpec((1, tm, d), lambda i, j: (i, j, 0)),
                  pl.BlockSpec((1, 1, tm, p.shape[-1]), lambda i, j: (layer, i, j, 0))]
                 + [_layer_block(a, layer) for a in stacked],
        out_specs=pl.BlockSpec((1, tm, d), lambda i, j: (i, j, 0)),
        scratch_shapes=[pltpu.VMEM((ff // LANES, SUBLANES + tm, LANES), F32),
                        pltpu.VMEM((ff // LANES, SUBLANES, LANES), F32), pltpu.VMEM((tm, ff), BF16)],
        compiler_params=_params(2),
        name=f"post_mixer_{layer}",
    )(x, p, *stacked)


def _block_diag(w):
    n, r, _ = w.shape
    eye = jnp.eye(n, dtype=w.dtype)
    return (eye[:, None, :, None] * w[:, :, None, :]).reshape(n * r, n * r)


def kernel(x, p, positions, norm_mix, norm_ffn, norm_ple, hy_w_in, hy_b_in, rg_conv_w, rg_conv_b, rg_w_a, rg_b_a, rg_w_x, rg_b_x, rg_lambda, ml_conv_w, ml_conv_b, ml_norm, hy_w_out, at_w_qkv, at_q_norm, at_k_norm, at_sinks, at_w_out, ff_w_up, ff_conv_w, ff_conv_b, ff_w_down, ple_w_gate, ple_w_proj):
    depth = p.shape[0]
    row = lambda v: v.astype(F32)[None, :]
    post = (norm_ffn.astype(F32)[:, None, :], ff_w_up.astype(BF16), ff_conv_w.astype(F32),
            ff_conv_b.astype(F32)[:, None, :], ff_w_down.astype(BF16), norm_ple.astype(F32)[:, None, :],
            ple_w_gate.astype(BF16), ple_w_proj.astype(BF16))
    for layer in range(depth):
        if layer % 2 == 0:
            e = layer // 2
            pad = LANES - 2 * ML_HEADS
            x = _hybrid_mixer(x, row(norm_mix[layer]), jnp.pad(hy_w_in[e], ((0, 0), (0, pad))).astype(BF16),
                              row(jnp.pad(hy_b_in[e], (0, pad))),
                              rg_conv_w[e], row(rg_conv_b[e]), _block_diag(rg_w_a[e]).astype(BF16), row(rg_b_a[e]),
                              _block_diag(rg_w_x[e]).astype(BF16), row(rg_b_x[e]), row(rg_lambda[e]),
                              ml_conv_w[e], row(ml_conv_b[e]), row(ml_norm[e]), hy_w_out[e].astype(BF16))
        else:
            o = layer // 2
            x = _attn_mixer(x, row(norm_mix[layer]), at_w_qkv[o].astype(BF16), positions, at_q_norm[o],
                            at_k_norm[o], at_sinks[o], at_w_out[o].astype(BF16))
        x = _post_mixer(x, p, layer, *post)
    return x
```
